```python
import jax, jax.numpy as jnp
from jax import lax
import numpy as np

D_MODEL = 2048
BATCH = 4
SEQ = 2048
DEPTH = 1
DEC_BATCH = 128
DEC_SEQ = 4
PAST_LEN = 16384
PAGE_SIZE = 128

N_META = 16
C_A = D_MODEL // 2
C_B = D_MODEL - C_A
MIX_WIDTH = C_A + C_B
K_A = 3
K_B = 31
N_EXPERTS = 32
TOP_K = 4
D_FF = D_MODEL
SWIGLU_LIMIT = 7.0
SWIGLU_ALPHA = 1.702
RMS_EPS = 1e-5
LN_EPS = 1e-5

kernel_name = "hybrid_shortconv_conformer_moe_step"


def rms_norm(x, g):
    xf = x.astype(jnp.float32)
    y = xf * lax.rsqrt(jnp.mean(xf * xf, axis=-1, keepdims=True) + RMS_EPS)
    return (y * g.astype(jnp.float32)).astype(x.dtype)


def layer_norm(x, g, b):
    xf = x.astype(jnp.float32)
    mu = jnp.mean(xf, axis=-1, keepdims=True)
    var = jnp.mean(jnp.square(xf - mu), axis=-1, keepdims=True)
    y = (xf - mu) * lax.rsqrt(var + LN_EPS)
    return (y * g.astype(jnp.float32) + b.astype(jnp.float32)).astype(x.dtype)


def causal_dwconv(x, prev, w):
    k, c = w.shape
    xp = jnp.concatenate([prev.astype(x.dtype), x], axis=1)
    out = lax.conv_general_dilated(
        xp, w[:, None, :].astype(x.dtype), window_strides=(1,), padding='VALID',
        dimension_numbers=('NWC', 'WIO', 'NWC'), feature_group_count=c)
    return out, xp[:, xp.shape[1] - (k - 1):]


def conv_mixer(h, sa, sb, w_in, conv_a_w, conv_b_w, conv_b_b, ln_b_g, ln_b_b, w_out):
    u = jnp.einsum('btd,dm->btm', h, w_in)
    bg, cg, v, ga, gb = jnp.split(u, [C_A, 2 * C_A, 3 * C_A, 3 * C_A + C_B], axis=-1)
    a_conv, sa_new = causal_dwconv(cg * v, sa, conv_a_w)
    y_a = bg * a_conv
    glu = ga * jax.nn.sigmoid(gb)
    b_conv, sb_new = causal_dwconv(glu, sb, conv_b_w)
    b_n = layer_norm(b_conv + conv_b_b, ln_b_g, ln_b_b)
    y_b = b_n * jax.nn.sigmoid(b_n)
    y = jnp.einsum('btm,md->btd', jnp.concatenate([y_a, y_b], axis=-1), w_out)
    return y, sa_new, sb_new


def moe(x2, w_router, b_router, w_gate_up, b_gate_up, w_down, b_down):
    logits = jnp.dot(x2, w_router).astype(jnp.float32) + b_router.astype(jnp.float32)
    top_v, top_i = lax.top_k(logits, TOP_K)
    gates = jax.nn.softmax(top_v, axis=-1)
    flat_e = top_i.reshape(-1)
    order = jnp.argsort(flat_e)
    sorted_e = flat_e[order]
    tok = order // TOP_K
    sizes = jnp.bincount(flat_e, length=N_EXPERTS).astype(jnp.int32)
    xs = x2[tok]
    hgu = lax.ragged_dot(xs, w_gate_up, sizes) + b_gate_up[sorted_e]
    gate, up = hgu[:, :D_FF], hgu[:, D_FF:]
    gate = jnp.minimum(gate, SWIGLU_LIMIT)
    up = jnp.clip(up, -SWIGLU_LIMIT, SWIGLU_LIMIT)
    act = gate * jax.nn.sigmoid(SWIGLU_ALPHA * gate) * (up + 1)
    o = lax.ragged_dot(act, w_down, sizes) + b_down[sorted_e]
    o = o * gates.reshape(-1)[order][:, None].astype(o.dtype)
    return jnp.zeros_like(x2).at[tok].add(o)


def block(x, sa, sb, norm1_g, w_in, conv_a_w, conv_b_w, conv_b_b, ln_b_g, ln_b_b, w_out,
          norm2_g, w_router, b_router, w_gate_up, b_gate_up, w_down, b_down):
    y, sa_new, sb_new = conv_mixer(rms_norm(x, norm1_g), sa, sb, w_in, conv_a_w, conv_b_w,
                                   conv_b_b, ln_b_g, ln_b_b, w_out)
    x = x + y
    b, t, d = x.shape
    m = moe(rms_norm(x, norm2_g).reshape(b * t, d), w_router, b_router,
            w_gate_up, b_gate_up, w_down, b_down)
    return x + m.reshape(b, t, d), sa_new, sb_new


def setup_inputs(seed: int = 0) -> dict:
    key = jax.random.key(seed)
    ks = jax.random.split(key, 24)
    f32 = jnp.float32
    nrm = lambda k, s, sc: jax.random.normal(k, s, f32) * sc
    return {
        "x_prompt": nrm(ks[0], (BATCH, SEQ, D_MODEL), 1.0),
        "x_sample": nrm(ks[1], (DEC_BATCH, DEC_SEQ, D_MODEL), 1.0),
        "state_conv_a": nrm(ks[2], (DEPTH, DEC_BATCH, K_A - 1, C_A), 1.0),
        "state_conv_b": nrm(ks[3], (DEPTH, DEC_BATCH, K_B - 1, C_B), 1.0),
        "meta_tokens": nrm(ks[4], (N_META, D_MODEL), 1.0),
        "norm1_g": 1.0 + nrm(ks[5], (DEPTH, D_MODEL), 0.02),
        "w_in": nrm(ks[6], (DEPTH, D_MODEL, 3 * C_A + 2 * C_B), D_MODEL ** -0.5),
        "conv_a_w": nrm(ks[7], (DEPTH, K_A, C_A), K_A ** -0.5),
        "conv_b_w": nrm(ks[8], (DEPTH, K_B, C_B), K_B ** -0.5),
        "conv_b_b": nrm(ks[9], (DEPTH, C_B), 0.02),
        "ln_b_g": 1.0 + nrm(ks[10], (DEPTH, C_B), 0.02),
        "ln_b_b": nrm(ks[11], (DEPTH, C_B), 0.02),
        "w_out": nrm(ks[12], (DEPTH, MIX_WIDTH, D_MODEL), MIX_WIDTH ** -0.5),
        "norm2_g": 1.0 + nrm(ks[13], (DEPTH, D_MODEL), 0.02),
        "w_router": nrm(ks[14], (DEPTH, D_MODEL, N_EXPERTS), D_MODEL ** -0.5),
        "b_router": nrm(ks[15], (DEPTH, N_EXPERTS), 0.01),
        "w_gate_up": nrm(ks[16], (DEPTH, N_EXPERTS, D_MODEL, 2 * D_FF), D_MODEL ** -0.5),
        "b_gate_up": nrm(ks[17], (DEPTH, N_EXPERTS, 2 * D_FF), 0.02),
        "w_down": nrm(ks[18], (DEPTH, N_EXPERTS, D_FF, D_MODEL), D_FF ** -0.5),
        "b_down": nrm(ks[19], (DEPTH, N_EXPERTS, D_MODEL), 0.02),
        "final_norm_g": 1.0 + nrm(ks[20], (D_MODEL,), 0.02),
    }


def reference(x_prompt, x_sample, state_conv_a, state_conv_b, meta_tokens, norm1_g, w_in,
              conv_a_w, conv_b_w, conv_b_b, ln_b_g, ln_b_b, w_out, norm2_g, w_router,
              b_router, w_gate_up, b_gate_up, w_down, b_down, final_norm_g):
    bp = x_prompt.shape[0]
    meta = jnp.broadcast_to(meta_tokens.astype(x_prompt.dtype)[None], (bp, N_META, D_MODEL))
    xp = jnp.concatenate([meta, x_prompt], axis=1)
    xs = x_sample
    pa_list, pb_list, sa_list, sb_list = [], [], [], []
    for l in range(DEPTH):
        lp = (norm1_g[l], w_in[l], conv_a_w[l], conv_b_w[l], conv_b_b[l], ln_b_g[l], ln_b_b[l],
              w_out[l], norm2_g[l], w_router[l], b_router[l], w_gate_up[l], b_gate_up[l],
              w_down[l], b_down[l])
        za = jnp.zeros((bp, K_A - 1, C_A), xp.dtype)
        zb = jnp.zeros((bp, K_B - 1, C_B), xp.dtype)
        xp, pa, pb = block(xp, za, zb, *lp)
        xs, sa, sb = block(xs, state_conv_a[l], state_conv_b[l], *lp)
        pa_list.append(pa)
        pb_list.append(pb)
        sa_list.append(sa)
        sb_list.append(sb)
    y_prompt = rms_norm(xp, final_norm_g)[:, N_META:]
    y_sample = rms_norm(xs, final_norm_g)
    new_conv_a_prompt = jnp.stack(pa_list, axis=0)
    new_conv_b_prompt = jnp.stack(pb_list, axis=0)
    new_conv_a_sample = jnp.stack(sa_list, axis=0)
    new_conv_b_sample = jnp.stack(sb_list, axis=0)
    return (y_prompt, y_sample, new_conv_a_prompt, new_conv_b_prompt, new_conv_a_sample, new_conv_b_sample)
```

```python
import functools

import jax
import jax.numpy as jnp
from jax import lax
from jax.experimental import pallas as pl
from jax.experimental.pallas import tpu as pltpu

F32 = jnp.float32
BF16 = jnp.bfloat16

D_MODEL = 2048
N_META = 16
C_A = 1024
C_B = 1024
K_A = 3
K_B = 31
N_EXPERTS = 32
TOP_K = 4
D_FF = 2048
SWIGLU_LIMIT = 7.0
SWIGLU_ALPHA = 1.702
RMS_EPS = 1e-5
LN_EPS = 1e-5

LANES = 128
N_CHUNK = C_B // LANES
HALO_B = 32
HALO_A = 8
T_MIX = 256
TM_POST = 512
TM_MOE = 256
BF_UP = 1024
BN_DOWN = 1024
TC_COMB = 128
VMEM_LIMIT = 56 * 1024 * 1024


def _sigmoid(x):
    return jax.nn.sigmoid(x)


def _rms_rows(x, g):
    ms = jnp.mean(x * x, axis=-1, keepdims=True)
    return (x * lax.rsqrt(ms + RMS_EPS)) * g


def _mixer_prompt_kernel(x_ref, meta_ref, g1_ref, win_ref, caw_ref, cbw_ref, cbb_ref, lng_ref, lnb_ref,
                         y_ref, pa_ref, pb_ref,
                         u_ref, gb_ref, cv_ref, mgb_ref, mcv_ref, bc_ref):
    b = pl.program_id(0)
    i = pl.program_id(1)
    n_i = pl.num_programs(1)
    t = T_MIX

    def in_proj(rows):
        h = _rms_rows(rows, g1_ref[...]).astype(BF16)
        return jnp.dot(h, win_ref[...], preferred_element_type=F32)

    @pl.when((b == 0) & (i == 0))
    def _():
        um = in_proj(meta_ref[...])
        cvm = um[:, C_A:2 * C_A] * um[:, 2 * C_A:3 * C_A]
        glum = um[:, 3 * C_A:3 * C_A + C_B] * _sigmoid(um[:, 3 * C_A + C_B:])
        for c in range(N_CHUNK):
            sl = slice(c * LANES, (c + 1) * LANES)
            mgb_ref[c, 0:HALO_B - N_META, :] = jnp.zeros((HALO_B - N_META, LANES), F32)
            mgb_ref[c, HALO_B - N_META:HALO_B, :] = glum[:, sl]
            mcv_ref[c] = cvm[N_META - HALO_A:, sl]

    @pl.when(i == 0)
    def _():
        gb_ref[:, 0:HALO_B, :] = mgb_ref[...]
        cv_ref[:, 0:HALO_A, :] = mcv_ref[...]

    u_ref[...] = in_proj(x_ref[...])

    for c in range(N_CHUNK):
        lo = c * LANES
        bg = u_ref[:, lo:lo + LANES]
        cg = u_ref[:, C_A + lo:C_A + lo + LANES]
        v = u_ref[:, 2 * C_A + lo:2 * C_A + lo + LANES]
        ga = u_ref[:, 3 * C_A + lo:3 * C_A + lo + LANES]
        gbv = u_ref[:, 3 * C_A + C_B + lo:3 * C_A + C_B + lo + LANES]
        cv_ref[c, HALO_A:, :] = cg * v
        gb_ref[c, HALO_B:, :] = ga * _sigmoid(gbv)
        acc = cv_ref[c, HALO_A - 2:HALO_A - 2 + t, :] * caw_ref[0:1, lo:lo + LANES]
        for k in range(1, K_A):
            acc = acc + cv_ref[c, HALO_A - 2 + k:HALO_A - 2 + k + t, :] * caw_ref[k:k + 1, lo:lo + LANES]
        y_ref[:, lo:lo + LANES] = (bg * acc).astype(BF16)
        off = HALO_B - (K_B - 1)
        accb = gb_ref[c, off:off + t, :] * cbw_ref[0:1, lo:lo + LANES]
        for k in range(1, K_B):
            accb = accb + gb_ref[c, off + k:off + k + t, :] * cbw_ref[k:k + 1, lo:lo + LANES]
        bc_ref[:, lo:lo + LANES] = accb + cbb_ref[:, lo:lo + LANES]

    rb = 64
    for r in range(t // rb):
        xb = bc_ref[r * rb:(r + 1) * rb, :]
        mu = jnp.mean(xb, axis=-1, keepdims=True)
        var = jnp.mean(jnp.square(xb - mu), axis=-1, keepdims=True)
        bn = (xb - mu) * lax.rsqrt(var + LN_EPS) * lng_ref[...] + lnb_ref[...]
        y_ref[r * rb:(r + 1) * rb, C_A:] = (bn * _sigmoid(bn)).astype(BF16)

    @pl.when(i == n_i - 1)
    def _():
        for c in range(N_CHUNK):
            sl = slice(c * LANES, (c + 1) * LANES)
            pa_ref[:, sl] = cv_ref[c, HALO_A + t - (K_A - 1):HALO_A + t, :]
            pb_ref[:, sl] = gb_ref[c, HALO_B + t - (K_B - 1):HALO_B + t, :]

    gb_ref[:, 0:HALO_B, :] = gb_ref[:, t:t + HALO_B, :]
    cv_ref[:, 0:HALO_A, :] = cv_ref[:, t:t + HALO_A, :]


def _mixer_prompt(x_prompt, meta, g1, win_bf, caw, cbw, cbb, lng, lnb):
    bp, seq, d = x_prompt.shape
    n_i = seq // T_MIX
    full = lambda shape: pl.BlockSpec(shape, lambda b, i: (0,) * len(shape))
    return pl.pallas_call(
        _mixer_prompt_kernel,
        grid=(bp, n_i),
        in_specs=[
            pl.BlockSpec((None, T_MIX, d), lambda b, i: (b, i, 0)),
            full((N_META, d)),
            full((1, d)),
            pl.BlockSpec(win_bf.shape, lambda b, i: (0, 0), pipeline_mode=pl.Buffered(1)),
            full((K_A, C_A)),
            full((K_B, C_B)),
            full((1, C_B)),
            full((1, C_B)),
            full((1, C_B)),
        ],
        out_specs=[
            pl.BlockSpec((T_MIX, d), lambda b, i: (b * n_i + i, 0)),
            pl.BlockSpec((None, K_A - 1, C_A), lambda b, i: (b, 0, 0)),
            pl.BlockSpec((None, K_B - 1, C_B), lambda b, i: (b, 0, 0)),
        ],
        out_shape=[
            jax.ShapeDtypeStruct((bp * seq, d), BF16),
            jax.ShapeDtypeStruct((bp, K_A - 1, C_A), F32),
            jax.ShapeDtypeStruct((bp, K_B - 1, C_B), F32),
        ],
        scratch_shapes=[
            pltpu.VMEM((T_MIX, win_bf.shape[1]), F32),
            pltpu.VMEM((N_CHUNK, T_MIX + HALO_B, LANES), F32),
            pltpu.VMEM((N_CHUNK, T_MIX + HALO_A, LANES), F32),
            pltpu.VMEM((N_CHUNK, HALO_B, LANES), F32),
            pltpu.VMEM((N_CHUNK, HALO_A, LANES), F32),
            pltpu.VMEM((T_MIX, C_B), F32),
        ],
        compiler_params=pltpu.CompilerParams(
            dimension_semantics=("arbitrary", "arbitrary"), vmem_limit_bytes=VMEM_LIMIT),
        name="mixer_prompt",
    )(x_prompt, meta, g1, win_bf, caw, cbw, cbb, lng, lnb)


def _mixer_sample_kernel(x_ref, g1_ref, wbg_ref, wcg_ref, wv_ref, wga_ref, wgb_ref, sa_ref, sb_ref,
                         caw_ref, cbw_ref, cbb_ref, lng_ref, lnb_ref,
                         y_ref, na_ref, glu_ref,
                         h_ref, ya_ref, bc_ref, *, n_seq, n_t):
    c = pl.program_id(0)

    @pl.when(c == 0)
    def _():
        h_ref[...] = _rms_rows(x_ref[...], g1_ref[...]).astype(BF16)

    @pl.when(c < N_CHUNK)
    def _():
        h = h_ref[...]
        proj = lambda w_ref: jnp.dot(h, w_ref[...], preferred_element_type=F32)
        bg, cg, v, ga, gbv = proj(wbg_ref), proj(wcg_ref), proj(wv_ref), proj(wga_ref), proj(wgb_ref)
        cv = cg * v
        glu = ga * _sigmoid(gbv)
        row = lambda a, tt: a[tt * n_seq:(tt + 1) * n_seq, :]
        xa = [sa_ref[j] for j in range(K_A - 1)] + [row(cv, tt) for tt in range(n_t)]
        xb = [sb_ref[j] for j in range(K_B - 1)] + [row(glu, tt) for tt in range(n_t)]
        for tt in range(n_t):
            acc = xa[tt] * caw_ref[0:1, :]
            for k in range(1, K_A):
                acc = acc + xa[tt + k] * caw_ref[k:k + 1, :]
            ya_ref[c, tt * n_seq:(tt + 1) * n_seq, :] = row(bg, tt) * acc
            accb = xb[tt] * cbw_ref[0:1, :]
            for k in range(1, K_B):
                accb = accb + xb[tt + k] * cbw_ref[k:k + 1, :]
            bc_ref[c, tt * n_seq:(tt + 1) * n_seq, :] = accb + cbb_ref[...]
            glu_ref[tt] = row(glu, tt)
        for j in range(K_A - 1):
            na_ref[j] = row(cv, n_t - (K_A - 1) + j)

    @pl.when(c == N_CHUNK)
    def _():
        xb = jnp.concatenate([bc_ref[cc] for cc in range(N_CHUNK)], axis=1)
        mu = jnp.mean(xb, axis=-1, keepdims=True)
        var = jnp.mean(jnp.square(xb - mu), axis=-1, keepdims=True)
        bn = (xb - mu) * lax.rsqrt(var + LN_EPS) * lng_ref[...] + lnb_ref[...]
        for cc in range(N_CHUNK):
            y_ref[:, cc * LANES:(cc + 1) * LANES] = ya_ref[cc].astype(BF16)
        y_ref[:, C_A:] = (bn * _sigmoid(bn)).astype(BF16)


def _mixer_sample(xs_t, g1, win_bf, sa_t, sb_t, caw, cbw, cbb, lng, lnb, n_seq, n_t):
    rows, d = xs_t.shape
    cc = lambda c: jnp.minimum(c, N_CHUNK - 1)
    wspec = lambda g: pl.BlockSpec((d, LANES), lambda c, g=g: (0, g * N_CHUNK + cc(c)))
    full = lambda shape: pl.BlockSpec(shape, lambda c: (0,) * len(shape))
    kern = functools.partial(_mixer_sample_kernel, n_seq=n_seq, n_t=n_t)
    return pl.pallas_call(
        kern,
        grid=(N_CHUNK + 1,),
        in_specs=[
            full((rows, d)),
            full((1, d)),
            wspec(0), wspec(1), wspec(2), wspec(3), wspec(4),
            pl.BlockSpec((K_A - 1, n_seq, LANES), lambda c: (0, 0, cc(c))),
            pl.BlockSpec((K_B - 1, n_seq, LANES), lambda c: (0, 0, cc(c))),
            pl.BlockSpec((K_A, LANES), lambda c: (0, cc(c))),
            pl.BlockSpec((K_B, LANES), lambda c: (0, cc(c))),
            pl.BlockSpec((1, LANES), lambda c: (0, cc(c))),
            full((1, C_B)),
            full((1, C_B)),
        ],
        out_specs=[
            full((rows, d)),
            pl.BlockSpec((K_A - 1, n_seq, LANES), lambda c: (0, 0, cc(c))),
            pl.BlockSpec((n_t, n_seq, LANES), lambda c: (0, 0, cc(c))),
        ],
        out_shape=[
            jax.ShapeDtypeStruct((rows, d), BF16),
            jax.ShapeDtypeStruct((K_A - 1, n_seq, C_A), F32),
            jax.ShapeDtypeStruct((n_t, n_seq, C_B), F32),
        ],
        scratch_shapes=[
            pltpu.VMEM((rows, d), BF16),
            pltpu.VMEM((N_CHUNK, rows, LANES), F32),
            pltpu.VMEM((N_CHUNK, rows, LANES), F32),
        ],
        compiler_params=pltpu.CompilerParams(
            dimension_semantics=("arbitrary",), vmem_limit_bytes=VMEM_LIMIT),
        name="mixer_sample",
    )(xs_t, g1, win_bf, win_bf, win_bf, win_bf, win_bf, sa_t, sb_t, caw, cbw, cbb, lng, lnb)


def _post_mixer_kernel(yp_ref, ys_ref, xp_ref, xs_ref, wout_ref, g2_ref, wr_ref, br_ref,
                       x1_ref, xn_ref, idx_ref, gate_ref, rank_ref, cnt_ref,
                       run_ref, *, n_prompt_tiles):
    i = pl.program_id(0)
    tm = yp_ref.shape[0]

    @pl.when(i == 0)
    def _():
        run_ref[...] = jnp.zeros(run_ref.shape, F32)

    is_prompt = i < n_prompt_tiles
    x = jnp.where(is_prompt, xp_ref[...], xs_ref[...])
    y = jnp.where(is_prompt, yp_ref[...], ys_ref[...])
    x1 = x + jnp.dot(y, wout_ref[...], preferred_element_type=F32)
    x1_ref[...] = x1
    xn = _rms_rows(x1, g2_ref[...])
    xn_ref[...] = xn

    logits = lax.dot_general(wr_ref[...], xn, (((1,), (1,)), ((), ())),
                             precision=lax.Precision.HIGHEST, preferred_element_type=F32) + br_ref[...]
    eidx = lax.broadcasted_iota(jnp.int32, logits.shape, 0)
    work = logits
    vals, sels, hots = [], [], []
    for _ in range(TOP_K):
        m = jnp.max(work, axis=0, keepdims=True)
        sel = jnp.min(jnp.where(work == m, eidx, N_EXPERTS), axis=0, keepdims=True)
        hot = eidx == sel
        vals.append(m)
        sels.append(sel)
        hots.append(hot)
        work = jnp.where(hot, -jnp.inf, work)
    exps = [jnp.exp(v - vals[0]) for v in vals]
    denom = exps[0] + exps[1] + exps[2] + exps[3]
    for k in range(TOP_K):
        idx_ref[k:k + 1, :] = sels[k]
        gate_ref[k:k + 1, :] = exps[k] / denom

    chosen = (hots[0] | hots[1] | hots[2] | hots[3])
    chosen_bf = chosen.astype(F32).astype(BF16)
    s_io = lax.broadcasted_iota(jnp.int32, (tm, tm), 0)
    t_io = lax.broadcasted_iota(jnp.int32, (tm, tm), 1)
    upper = (s_io < t_io).astype(F32).astype(BF16)
    before = jnp.dot(chosen_bf, upper, preferred_element_type=F32) + run_ref[:, 0:1]
    for k in range(TOP_K):
        r = jnp.sum(jnp.where(hots[k], before, 0.0), axis=0, keepdims=True)
        rank_ref[k:k + 1, :] = r.astype(jnp.int32)
    run_ref[...] = run_ref[...] + jnp.sum(chosen.astype(F32), axis=1, keepdims=True)
    cnt_ref[...] = run_ref[...].astype(jnp.int32)


def _post_mixer(yp, ys, xp2, xs_t, wout_bf, g2, wr_t, br_col):
    d = yp.shape[1]
    n = yp.shape[0] + ys.shape[0]
    tm = TM_POST
    n_tiles = n // tm
    n_pt = xp2.shape[0] // tm
    kern = functools.partial(_post_mixer_kernel, n_prompt_tiles=n_pt)
    full = lambda shape: pl.BlockSpec(shape, lambda i: (0,) * len(shape))
    return pl.pallas_call(
        kern,
        grid=(n_tiles,),
        in_specs=[
            pl.BlockSpec((tm, d), lambda i: (jnp.minimum(i, n_pt - 1), 0)),
            pl.BlockSpec((tm, d), lambda i: (jnp.maximum(i - n_pt, 0), 0)),
            pl.BlockSpec((tm, d), lambda i: (jnp.minimum(i, n_pt - 1), 0)),
            pl.BlockSpec((tm, d), lambda i: (jnp.maximum(i - n_pt, 0), 0)),
            pl.BlockSpec(wout_bf.shape, lambda i: (0, 0), pipeline_mode=pl.Buffered(1)),
            full((1, d)),
            full((N_EXPERTS, d)),
            full((N_EXPERTS, 1)),
        ],
        out_specs=[
            pl.BlockSpec((tm, d), lambda i: (i, 0)),
            pl.BlockSpec((tm, d), lambda i: (i, 0)),
            pl.BlockSpec((TOP_K, tm), lambda i: (0, i)),
            pl.BlockSpec((TOP_K, tm), lambda i: (0, i)),
            pl.BlockSpec((TOP_K, tm), lambda i: (0, i)),
            full((N_EXPERTS, LANES)),
        ],
        out_shape=[
            jax.ShapeDtypeStruct((n, d), F32),
            jax.ShapeDtypeStruct((n, d), F32),
            jax.ShapeDtypeStruct((TOP_K, n), jnp.int32),
            jax.ShapeDtypeStruct((TOP_K, n), F32),
            jax.ShapeDtypeStruct((TOP_K, n), jnp.int32),
            jax.ShapeDtypeStruct((N_EXPERTS, LANES), jnp.int32),
        ],
        scratch_shapes=[pltpu.VMEM((N_EXPERTS, LANES), F32)],
        compiler_params=pltpu.CompilerParams(
            dimension_semantics=("arbitrary",), vmem_limit_bytes=VMEM_LIMIT),
        name="post_mixer",
    )(yp, ys, xp2, xs_t, wout_bf, g2, wr_t, br_col)


def _row_gather_copy(src_hbm, idx, buf, slot, r, sem):
    return pltpu.make_async_copy(src_hbm.at[pl.ds(idx, 1)], buf.at[slot, pl.ds(r, 1)], sem.at[slot])


def _dispatch_kernel(nused_ref, src_ref, xn_hbm, out_ref, buf, sem):
    t = pl.program_id(0)
    n_t = pl.num_programs(0)
    tm = out_ref.shape[0]
    n_used = nused_ref[0]

    def issue(tile, slot):
        def body(r, carry):
            _row_gather_copy(xn_hbm, src_ref[tile * tm + r], buf, slot, r, sem).start()
            return carry
        lax.fori_loop(0, tm, body, 0)

    @pl.when(t == 0)
    def _():
        issue(0, 0)

    @pl.when((t + 1 < n_t) & (t + 1 < n_used))
    def _():
        issue(t + 1, (t + 1) % 2)

    @pl.when(t < n_used)
    def _():
        slot = t % 2
        pltpu.make_async_copy(xn_hbm.at[pl.ds(0, tm)], buf.at[slot], sem.at[slot]).wait()
        out_ref[...] = buf[slot].astype(BF16)

    @pl.when(t >= n_used)
    def _():
        out_ref[...] = jnp.zeros(out_ref.shape, BF16)


def _dispatch(n_used, src, xn, p_max):
    n, d = xn.shape
    tm = TM_MOE
    n_tiles = p_max // tm
    return pl.pallas_call(
        _dispatch_kernel,
        grid_spec=pltpu.PrefetchScalarGridSpec(
            num_scalar_prefetch=2,
            grid=(n_tiles,),
            in_specs=[pl.BlockSpec(memory_space=pl.ANY)],
            out_specs=pl.BlockSpec((tm, d), lambda t, nu, s: (t, 0)),
            scratch_shapes=[pltpu.VMEM((2, tm, d), F32), pltpu.SemaphoreType.DMA((2,))],
        ),
        out_shape=jax.ShapeDtypeStruct((p_max, d), BF16),
        compiler_params=pltpu.CompilerParams(
            dimension_semantics=("arbitrary",), vmem_limit_bytes=VMEM_LIMIT),
        name="dispatch",
    )(n_used, src, xn)


_FLAG_VALID = 1
_FLAG_NEW_WEIGHTS = 2
_CAST_ROWS = 256


def _cast_weight(w_ref, wbf_ref):
    def body(r, carry):
        r0 = pl.multiple_of(r * _CAST_ROWS, _CAST_ROWS)
        wbf_ref[pl.ds(r0, _CAST_ROWS), :] = w_ref[pl.ds(r0, _CAST_ROWS), :].astype(BF16)
        return carry
    lax.fori_loop(0, w_ref.shape[0] // _CAST_ROWS, body, 0)


def _moe_up_kernel(te_ref, tb_ref, fl_ref, x_ref, wg_ref, wu_ref, bg_ref, bu_ref, h_ref, wgb_ref, wub_ref):
    t = pl.program_id(1)
    flags = fl_ref[t]

    @pl.when((flags & _FLAG_NEW_WEIGHTS) != 0)
    def _():
        _cast_weight(wg_ref, wgb_ref)
        _cast_weight(wu_ref, wub_ref)

    @pl.when((flags & _FLAG_VALID) != 0)
    def _():
        x = x_ref[...]
        gate = jnp.dot(x, wgb_ref[...], preferred_element_type=F32) + bg_ref[...]
        up = jnp.dot(x, wub_ref[...], preferred_element_type=F32) + bu_ref[...]
        gate = jnp.minimum(gate, SWIGLU_LIMIT)
        up = jnp.clip(up, -SWIGLU_LIMIT, SWIGLU_LIMIT)
        act = gate * _sigmoid(SWIGLU_ALPHA * gate) * (up + 1.0)
        h_ref[...] = act.astype(BF16)

    @pl.when((flags & _FLAG_VALID) == 0)
    def _():
        h_ref[...] = jnp.zeros(h_ref.shape, BF16)


def _moe_up(te, tb, fl, xs, w_gate_up, b_gate_up3):
    p_max, d = xs.shape
    tm = TM_MOE
    n_tiles = p_max // tm
    n_j = D_FF // BF_UP
    return pl.pallas_call(
        _moe_up_kernel,
        grid_spec=pltpu.PrefetchScalarGridSpec(
            num_scalar_prefetch=3,
            grid=(n_j, n_tiles),
            in_specs=[
                pl.BlockSpec((tm, d), lambda j, t, te, tb, fl: (tb[t], 0)),
                pl.BlockSpec((None, d, BF_UP), lambda j, t, te, tb, fl: (te[t], 0, j)),
                pl.BlockSpec((None, d, BF_UP), lambda j, t, te, tb, fl: (te[t], 0, n_j + j)),
                pl.BlockSpec((None, 1, BF_UP), lambda j, t, te, tb, fl: (te[t], 0, j)),
                pl.BlockSpec((None, 1, BF_UP), lambda j, t, te, tb, fl: (te[t], 0, n_j + j)),
            ],
            out_specs=pl.BlockSpec((tm, BF_UP), lambda j, t, te, tb, fl: (t, j)),
            scratch_shapes=[pltpu.VMEM((d, BF_UP), BF16), pltpu.VMEM((d, BF_UP), BF16)],
        ),
        out_shape=jax.ShapeDtypeStruct((p_max, D_FF), BF16),
        compiler_params=pltpu.CompilerParams(
            dimension_semantics=("arbitrary", "arbitrary"), vmem_limit_bytes=VMEM_LIMIT),
        name="moe_up",
    )(te, tb, fl, xs, w_gate_up, w_gate_up, b_gate_up3, b_gate_up3)


def _moe_down_kernel(te_ref, tb_ref, fl_ref, h_ref, wd_ref, bd_ref, y_ref, wdb_ref):
    t = pl.program_id(1)
    flags = fl_ref[t]

    @pl.when((flags & _FLAG_NEW_WEIGHTS) != 0)
    def _():
        _cast_weight(wd_ref, wdb_ref)

    @pl.when((flags & _FLAG_VALID) != 0)
    def _():
        y_ref[...] = jnp.dot(h_ref[...], wdb_ref[...], preferred_element_type=F32) + bd_ref[...]

    @pl.when((flags & _FLAG_VALID) == 0)
    def _():
        y_ref[...] = jnp.zeros(y_ref.shape, F32)


def _moe_down(te, tb, fl, h, w_down, b_down3):
    p_max, f = h.shape
    tm = TM_MOE
    n_tiles = p_max // tm
    n_j = D_MODEL // BN_DOWN
    return pl.pallas_call(
        _moe_down_kernel,
        grid_spec=pltpu.PrefetchScalarGridSpec(
            num_scalar_prefetch=3,
            grid=(n_j, n_tiles),
            in_specs=[
                pl.BlockSpec((tm, f), lambda j, t, te, tb, fl: (tb[t], 0)),
                pl.BlockSpec((None, f, BN_DOWN), lambda j, t, te, tb, fl: (te[t], 0, j)),
                pl.BlockSpec((None, 1, BN_DOWN), lambda j, t, te, tb, fl: (te[t], 0, j)),
            ],
            out_specs=pl.BlockSpec((tm, BN_DOWN), lambda j, t, te, tb, fl: (t, j)),
            scratch_shapes=[pltpu.VMEM((f, BN_DOWN), BF16)],
        ),
        out_shape=jax.ShapeDtypeStruct((p_max, D_MODEL), F32),
        compiler_params=pltpu.CompilerParams(
            dimension_semantics=("arbitrary", "arbitrary"), vmem_limit_bytes=VMEM_LIMIT),
        name="moe_down",
    )(te, tb, fl, h, w_down, b_down3)


def _combine_kernel(pos_ref, x1_ref, gate_ref, gf_ref, y_hbm, op_ref, os_ref, buf, sem, *, n_prompt_tiles):
    i = pl.program_id(0)
    n_i = pl.num_programs(0)
    tc = x1_ref.shape[0]
    rows = TOP_K * tc

    def issue(tile, slot):
        def body(r, carry):
            _row_gather_copy(y_hbm, pos_ref[tile * rows + r], buf, slot, r, sem).start()
            return carry
        lax.fori_loop(0, rows, body, 0)

    @pl.when(i == 0)
    def _():
        issue(0, 0)

    @pl.when(i + 1 < n_i)
    def _():
        issue(i + 1, (i + 1) % 2)

    slot = i % 2
    pltpu.make_async_copy(y_hbm.at[pl.ds(0, rows)], buf.at[slot], sem.at[slot]).wait()
    acc = x1_ref[...]
    for k in range(TOP_K):
        acc = acc + gate_ref[:, k:k + 1] * buf[slot, k * tc:(k + 1) * tc, :]
    out = _rms_rows(acc, gf_ref[...])

    @pl.when(i < n_prompt_tiles)
    def _():
        op_ref[...] = out

    @pl.when(i >= n_prompt_tiles)
    def _():
        os_ref[...] = out


def _combine(pos_flat, x1, gates_nk, gf, y_sorted, n_prompt):
    n, d = x1.shape
    tc = TC_COMB
    n_tiles = n // tc
    n_pt = n_prompt // tc
    kern = functools.partial(_combine_kernel, n_prompt_tiles=n_pt)
    return pl.pallas_call(
        kern,
        grid_spec=pltpu.PrefetchScalarGridSpec(
            num_scalar_prefetch=1,
            grid=(n_tiles,),
            in_specs=[
                pl.BlockSpec((tc, d), lambda i, p: (i, 0)),
                pl.BlockSpec((tc, TOP_K), lambda i, p: (i, 0)),
                pl.BlockSpec((1, d), lambda i, p: (0, 0)),
                pl.BlockSpec(memory_space=pl.ANY),
            ],
            out_specs=[
                pl.BlockSpec((tc, d), lambda i, p: (jnp.minimum(i, n_pt - 1), 0)),
                pl.BlockSpec((tc, d), lambda i, p: (jnp.maximum(i - n_pt, 0), 0)),
            ],
            scratch_shapes=[pltpu.VMEM((2, TOP_K * tc, d), F32), pltpu.SemaphoreType.DMA((2,))],
        ),
        out_shape=[
            jax.ShapeDtypeStruct((n_prompt, d), F32),
            jax.ShapeDtypeStruct((n - n_prompt, d), F32),
        ],
        compiler_params=pltpu.CompilerParams(
            dimension_semantics=("arbitrary",), vmem_limit_bytes=VMEM_LIMIT),
        name="combine",
    )(pos_flat, x1, gates_nk, gf, y_sorted)


def kernel(x_prompt, x_sample, state_conv_a, state_conv_b, meta_tokens, norm1_g, w_in, conv_a_w, conv_b_w,
           conv_b_b, ln_b_g, ln_b_b, w_out, norm2_g, w_router, b_router, w_gate_up, b_gate_up, w_down,
           b_down, final_norm_g):
    bp, seq, d = x_prompt.shape
    n_seq, n_t, _ = x_sample.shape
    n_prompt = bp * seq
    n_sample = n_seq * n_t
    n = n_prompt + n_sample
    assert norm1_g.shape[0] == 1, "single layer"
    assert seq % T_MIX == 0 and n_prompt % TM_POST == 0 and n_sample == TM_POST
    assert n_prompt % n_sample == 0 and n % TC_COMB == 0 and n_prompt % TC_COMB == 0

    g1 = norm1_g[0][None]
    win_bf = w_in[0].astype(BF16)
    wout_bf = w_out[0].astype(BF16)
    caw, cbw = conv_a_w[0], conv_b_w[0]
    cbb, lng, lnb = conv_b_b[0][None], ln_b_g[0][None], ln_b_b[0][None]

    xs_t = jnp.transpose(x_sample, (1, 0, 2)).reshape(n_sample, d)
    sa_t = jnp.transpose(state_conv_a[0], (1, 0, 2))
    sb_t = jnp.transpose(state_conv_b[0], (1, 0, 2))

    ymix_p, pa, pb = _mixer_prompt(x_prompt, meta_tokens, g1, win_bf, caw, cbw, cbb, lng, lnb)
    ymix_s, na_t, glu_t = _mixer_sample(xs_t, g1, win_bf, sa_t, sb_t, caw, cbw, cbb, lng, lnb, n_seq, n_t)

    x1, xn, idx, gates, rank, cnt = _post_mixer(
        ymix_p, ymix_s, x_prompt.reshape(n_prompt, d), xs_t, wout_bf, norm2_g[0][None],
        jnp.transpose(w_router[0]), b_router[0][:, None])

    tm = TM_MOE
    n_assign = n * TOP_K
    n_tiles = (n_assign + N_EXPERTS * (tm - 1)) // tm
    p_max = n_tiles * tm
    counts = cnt[:, 0]
    tiles_per_e = (counts + tm - 1) // tm
    tile_end = jnp.cumsum(tiles_per_e)
    tile_start = tile_end - tiles_per_e
    n_used = tile_end[-1]
    pos = tile_start[idx] * tm + rank
    tid = jnp.arange(n_tiles, dtype=jnp.int32)
    tb = jnp.minimum(tid, n_used - 1).astype(jnp.int32)
    te = jnp.minimum(jnp.searchsorted(tile_end, tb, side="right"), N_EXPERTS - 1).astype(jnp.int32)
    valid = tid < n_used
    new_w = valid & ((tid == 0) | (te != jnp.roll(te, 1)))
    fl = (valid.astype(jnp.int32) * _FLAG_VALID + new_w.astype(jnp.int32) * _FLAG_NEW_WEIGHTS)
    tok = jnp.broadcast_to(jnp.arange(n, dtype=jnp.int32)[None], (TOP_K, n))
    src = jnp.zeros((p_max,), jnp.int32).at[pos.reshape(-1)].set(tok.reshape(-1))

    xs_sorted = _dispatch(n_used.reshape(1).astype(jnp.int32), src, xn, p_max)
    h = _moe_up(te, tb, fl, xs_sorted, w_gate_up[0], b_gate_up[0][:, None, :])
    y_sorted = _moe_down(te, tb, fl, h, w_down[0], b_down[0][:, None, :])

    tc = TC_COMB
    pos_flat = pos.reshape(TOP_K, n // tc, tc).transpose(1, 0, 2).reshape(-1).astype(jnp.int32)
    yp, ys_t = _combine(pos_flat, x1, jnp.transpose(gates), final_norm_g[None], y_sorted, n_prompt)

    y_prompt = yp.reshape(bp, seq, d)
    y_sample = jnp.transpose(ys_t.reshape(n_t, n_seq, d), (1, 0, 2))
    new_a_prompt = pa[None]
    new_b_prompt = pb[None]
    new_a_sample = jnp.transpose(na_t, (1, 0, 2))[None]
    glu_s = jnp.transpose(glu_t, (1, 0, 2))
    new_b_sample = jnp.concatenate([state_conv_b[0][:, n_t:], glu_s], axis=1)[None]
    return (y_prompt, y_sample, new_a_prompt, new_b_prompt, new_a_sample, new_b_sample)
```

```python
import functools

import jax
import jax.numpy as jnp
from jax import lax
from jax.experimental import pallas as pl
from jax.experimental.pallas import tpu as pltpu

F32 = jnp.float32
BF16 = jnp.bfloat16

D_MODEL = 2048
N_META = 16
C_A = 1024
C_B = 1024
K_A = 3
K_B = 31
N_EXPERTS = 32
TOP_K = 4
D_FF = 2048
SWIGLU_LIMIT = 7.0
SWIGLU_ALPHA = 1.702
RMS_EPS = 1e-5
LN_EPS = 1e-5

LANES = 128
N_CHUNK = C_B // LANES
HALO_B = 32
HALO_A = 8
T_MIX = 256
TM_POST = 512
TM_MOE = 256
BF_UP = 1024
BN_DOWN = 1024
TC_COMB = 128
TC_DISP = 512
ISSUE_UNROLL = 8
VMEM_LIMIT = 56 * 1024 * 1024


def _sigmoid(x):
    return jax.nn.sigmoid(x)


def _rms_rows(x, g):
    ms = jnp.mean(x * x, axis=-1, keepdims=True)
    return (x * lax.rsqrt(ms + RMS_EPS)) * g


def _pack_bf16_pairs(x):
    c = x.shape[1] // 2
    hi = lax.bitcast_convert_type(x[:, :c].astype(BF16).astype(F32), jnp.uint32)
    lo = lax.bitcast_convert_type(x[:, c:].astype(BF16).astype(F32), jnp.uint32)
    return hi | (lo >> 16)


def _unpack_bf16_pairs(p):
    hi = lax.bitcast_convert_type(p & jnp.uint32(0xFFFF0000), F32).astype(BF16)
    lo = lax.bitcast_convert_type(p << 16, F32).astype(BF16)
    return jnp.concatenate([hi, lo], axis=1)


def _mixer_prompt_kernel(x_ref, meta_ref, g1_ref, win_ref, caw_ref, cbw_ref, cbb_ref, lng_ref, lnb_ref,
                         y_ref, pa_ref, pb_ref,
                         u_ref, gb_ref, cv_ref, mgb_ref, mcv_ref, bc_ref):
    b = pl.program_id(0)
    i = pl.program_id(1)
    n_i = pl.num_programs(1)
    t = T_MIX

    def in_proj(rows):
        h = _rms_rows(rows, g1_ref[...]).astype(BF16)
        return jnp.dot(h, win_ref[...], preferred_element_type=F32)

    @pl.when((b == 0) & (i == 0))
    def _():
        um = in_proj(meta_ref[...])
        cvm = um[:, C_A:2 * C_A] * um[:, 2 * C_A:3 * C_A]
        glum = um[:, 3 * C_A:3 * C_A + C_B] * _sigmoid(um[:, 3 * C_A + C_B:])
        for c in range(N_CHUNK):
            sl = slice(c * LANES, (c + 1) * LANES)
            mgb_ref[c, 0:HALO_B - N_META, :] = jnp.zeros((HALO_B - N_META, LANES), F32)
            mgb_ref[c, HALO_B - N_META:HALO_B, :] = glum[:, sl]
            mcv_ref[c] = cvm[N_META - HALO_A:, sl]

    @pl.when(i == 0)
    def _():
        gb_ref[:, 0:HALO_B, :] = mgb_ref[...]
        cv_ref[:, 0:HALO_A, :] = mcv_ref[...]

    u_ref[...] = in_proj(x_ref[...])

    for c in range(N_CHUNK):
        lo = c * LANES
        bg = u_ref[:, lo:lo + LANES]
        cg = u_ref[:, C_A + lo:C_A + lo + LANES]
        v = u_ref[:, 2 * C_A + lo:2 * C_A + lo + LANES]
        ga = u_ref[:, 3 * C_A + lo:3 * C_A + lo + LANES]
        gbv = u_ref[:, 3 * C_A + C_B + lo:3 * C_A + C_B + lo + LANES]
        cv_ref[c, HALO_A:, :] = cg * v
        gb_ref[c, HALO_B:, :] = ga * _sigmoid(gbv)
        acc = cv_ref[c, HALO_A - 2:HALO_A - 2 + t, :] * caw_ref[0:1, lo:lo + LANES]
        for k in range(1, K_A):
            acc = acc + cv_ref[c, HALO_A - 2 + k:HALO_A - 2 + k + t, :] * caw_ref[k:k + 1, lo:lo + LANES]
        y_ref[:, lo:lo + LANES] = (bg * acc).astype(BF16)
        off = HALO_B - (K_B - 1)
        accb = gb_ref[c, off:off + t, :] * cbw_ref[0:1, lo:lo + LANES]
        for k in range(1, K_B):
            accb = accb + gb_ref[c, off + k:off + k + t, :] * cbw_ref[k:k + 1, lo:lo + LANES]
        bc_ref[:, lo:lo + LANES] = accb + cbb_ref[:, lo:lo + LANES]

    rb = 64
    for r in range(t // rb):
        xb = bc_ref[r * rb:(r + 1) * rb, :]
        mu = jnp.mean(xb, axis=-1, keepdims=True)
        var = jnp.mean(jnp.square(xb - mu), axis=-1, keepdims=True)
        bn = (xb - mu) * lax.rsqrt(var + LN_EPS) * lng_ref[...] + lnb_ref[...]
        y_ref[r * rb:(r + 1) * rb, C_A:] = (bn * _sigmoid(bn)).astype(BF16)

    @pl.when(i == n_i - 1)
    def _():
        for c in range(N_CHUNK):
            sl = slice(c * LANES, (c + 1) * LANES)
            pa_ref[:, sl] = cv_ref[c, HALO_A + t - (K_A - 1):HALO_A + t, :]
            pb_ref[:, sl] = gb_ref[c, HALO_B + t - (K_B - 1):HALO_B + t, :]

    gb_ref[:, 0:HALO_B, :] = gb_ref[:, t:t + HALO_B, :]
    cv_ref[:, 0:HALO_A, :] = cv_ref[:, t:t + HALO_A, :]


def _mixer_prompt(x_prompt, meta, g1, win_bf, caw, cbw, cbb, lng, lnb):
    bp, seq, d = x_prompt.shape
    n_i = seq // T_MIX
    full = lambda shape: pl.BlockSpec(shape, lambda b, i: (0,) * len(shape))
    return pl.pallas_call(
        _mixer_prompt_kernel,
        grid=(bp, n_i),
        in_specs=[
            pl.BlockSpec((None, T_MIX, d), lambda b, i: (b, i, 0)),
            full((N_META, d)),
            full((1, d)),
            pl.BlockSpec(win_bf.shape, lambda b, i: (0, 0), pipeline_mode=pl.Buffered(1)),
            full((K_A, C_A)),
            full((K_B, C_B)),
            full((1, C_B)),
            full((1, C_B)),
            full((1, C_B)),
        ],
        out_specs=[
            pl.BlockSpec((T_MIX, d), lambda b, i: (b * n_i + i, 0)),
            pl.BlockSpec((None, K_A - 1, C_A), lambda b, i: (b, 0, 0)),
            pl.BlockSpec((None, K_B - 1, C_B), lambda b, i: (b, 0, 0)),
        ],
        out_shape=[
            jax.ShapeDtypeStruct((bp * seq, d), BF16),
            jax.ShapeDtypeStruct((bp, K_A - 1, C_A), F32),
            jax.ShapeDtypeStruct((bp, K_B - 1, C_B), F32),
        ],
        scratch_shapes=[
            pltpu.VMEM((T_MIX, win_bf.shape[1]), F32),
            pltpu.VMEM((N_CHUNK, T_MIX + HALO_B, LANES), F32),
            pltpu.VMEM((N_CHUNK, T_MIX + HALO_A, LANES), F32),
            pltpu.VMEM((N_CHUNK, HALO_B, LANES), F32),
            pltpu.VMEM((N_CHUNK, HALO_A, LANES), F32),
            pltpu.VMEM((T_MIX, C_B), F32),
        ],
        compiler_params=pltpu.CompilerParams(
            dimension_semantics=("arbitrary", "arbitrary"), vmem_limit_bytes=VMEM_LIMIT),
        name="mixer_prompt",
    )(x_prompt, meta, g1, win_bf, caw, cbw, cbb, lng, lnb)


def _mixer_sample_kernel(x_ref, g1_ref, wbg_ref, wcg_ref, wv_ref, wga_ref, wgb_ref, sa_ref, sb_ref,
                         caw_ref, cbw_ref, cbb_ref, lng_ref, lnb_ref,
                         y_ref, na_ref, glu_ref,
                         h_ref, ya_ref, bc_ref, *, n_seq, n_t):
    c = pl.program_id(0)

    @pl.when(c == 0)
    def _():
        h_ref[...] = _rms_rows(x_ref[...], g1_ref[...]).astype(BF16)

    @pl.when(c < N_CHUNK)
    def _():
        h = h_ref[...]
        proj = lambda w_ref: jnp.dot(h, w_ref[...], preferred_element_type=F32)
        bg, cg, v, ga, gbv = proj(wbg_ref), proj(wcg_ref), proj(wv_ref), proj(wga_ref), proj(wgb_ref)
        cv = cg * v
        glu = ga * _sigmoid(gbv)
        row = lambda a, tt: a[tt * n_seq:(tt + 1) * n_seq, :]
        xa = [sa_ref[j] for j in range(K_A - 1)] + [row(cv, tt) for tt in range(n_t)]
        xb = [sb_ref[j] for j in range(K_B - 1)] + [row(glu, tt) for tt in range(n_t)]
        for tt in range(n_t):
            acc = xa[tt] * caw_ref[0:1, :]
            for k in range(1, K_A):
                acc = acc + xa[tt + k] * caw_ref[k:k + 1, :]
            ya_ref[c, tt * n_seq:(tt + 1) * n_seq, :] = row(bg, tt) * acc
            accb = xb[tt] * cbw_ref[0:1, :]
            for k in range(1, K_B):
                accb = accb + xb[tt + k] * cbw_ref[k:k + 1, :]
            bc_ref[c, tt * n_seq:(tt + 1) * n_seq, :] = accb + cbb_ref[...]
            glu_ref[tt] = row(glu, tt)
        for j in range(K_A - 1):
            na_ref[j] = row(cv, n_t - (K_A - 1) + j)

    @pl.when(c == N_CHUNK)
    def _():
        xb = jnp.concatenate([bc_ref[cc] for cc in range(N_CHUNK)], axis=1)
        mu = jnp.mean(xb, axis=-1, keepdims=True)
        var = jnp.mean(jnp.square(xb - mu), axis=-1, keepdims=True)
        bn = (xb - mu) * lax.rsqrt(var + LN_EPS) * lng_ref[...] + lnb_ref[...]
        for cc in range(N_CHUNK):
            y_ref[:, cc * LANES:(cc + 1) * LANES] = ya_ref[cc].astype(BF16)
        y_ref[:, C_A:] = (bn * _sigmoid(bn)).astype(BF16)


def _mixer_sample(xs_t, g1, win_bf, sa_t, sb_t, caw, cbw, cbb, lng, lnb, n_seq, n_t):
    rows, d = xs_t.shape
    cc = lambda c: jnp.minimum(c, N_CHUNK - 1)
    wspec = lambda g: pl.BlockSpec((d, LANES), lambda c, g=g: (0, g * N_CHUNK + cc(c)))
    full = lambda shape: pl.BlockSpec(shape, lambda c: (0,) * len(shape))
    kern = functools.partial(_mixer_sample_kernel, n_seq=n_seq, n_t=n_t)
    return pl.pallas_call(
        kern,
        grid=(N_CHUNK + 1,),
        in_specs=[
            full((rows, d)),
            full((1, d)),
            wspec(0), wspec(1), wspec(2), wspec(3), wspec(4),
            pl.BlockSpec((K_A - 1, n_seq, LANES), lambda c: (0, 0, cc(c))),
            pl.BlockSpec((K_B - 1, n_seq, LANES), lambda c: (0, 0, cc(c))),
            pl.BlockSpec((K_A, LANES), lambda c: (0, cc(c))),
            pl.BlockSpec((K_B, LANES), lambda c: (0, cc(c))),
            pl.BlockSpec((1, LANES), lambda c: (0, cc(c))),
            full((1, C_B)),
            full((1, C_B)),
        ],
        out_specs=[
            full((rows, d)),
            pl.BlockSpec((K_A - 1, n_seq, LANES), lambda c: (0, 0, cc(c))),
            pl.BlockSpec((n_t, n_seq, LANES), lambda c: (0, 0, cc(c))),
        ],
        out_shape=[
            jax.ShapeDtypeStruct((rows, d), BF16),
            jax.ShapeDtypeStruct((K_A - 1, n_seq, C_A), F32),
            jax.ShapeDtypeStruct((n_t, n_seq, C_B), F32),
        ],
        scratch_shapes=[
            pltpu.VMEM((rows, d), BF16),
            pltpu.VMEM((N_CHUNK, rows, LANES), F32),
            pltpu.VMEM((N_CHUNK, rows, LANES), F32),
        ],
        compiler_params=pltpu.CompilerParams(
            dimension_semantics=("arbitrary",), vmem_limit_bytes=VMEM_LIMIT),
        name="mixer_sample",
    )(xs_t, g1, win_bf, win_bf, win_bf, win_bf, win_bf, sa_t, sb_t, caw, cbw, cbb, lng, lnb)


def _post_mixer_kernel(yp_ref, ys_ref, xp_ref, xs_ref, wout_ref, g2_ref, wr_ref, br_ref,
                       x1_ref, xn_ref, idx_ref, gate_ref, rank_ref, cnt_ref,
                       run_ref, *, n_prompt_tiles):
    i = pl.program_id(0)
    tm = yp_ref.shape[0]

    @pl.when(i == 0)
    def _():
        run_ref[...] = jnp.zeros(run_ref.shape, F32)

    is_prompt = i < n_prompt_tiles
    x = jnp.where(is_prompt, xp_ref[...], xs_ref[...])
    y = jnp.where(is_prompt, yp_ref[...], ys_ref[...])
    x1 = x + jnp.dot(y, wout_ref[...], preferred_element_type=F32)
    x1_ref[...] = x1
    xn = _rms_rows(x1, g2_ref[...])
    xn_ref[...] = _pack_bf16_pairs(xn)

    logits = lax.dot_general(wr_ref[...], xn, (((1,), (1,)), ((), ())),
                             precision=lax.Precision.HIGHEST, preferred_element_type=F32) + br_ref[...]
    eidx = lax.broadcasted_iota(jnp.int32, logits.shape, 0)
    work = logits
    vals, sels, hots = [], [], []
    for _ in range(TOP_K):
        m = jnp.max(work, axis=0, keepdims=True)
        sel = jnp.min(jnp.where(work == m, eidx, N_EXPERTS), axis=0, keepdims=True)
        hot = eidx == sel
        vals.append(m)
        sels.append(sel)
        hots.append(hot)
        work = jnp.where(hot, -jnp.inf, work)
    exps = [jnp.exp(v - vals[0]) for v in vals]
    denom = exps[0] + exps[1] + exps[2] + exps[3]
    for k in range(TOP_K):
        idx_ref[k:k + 1, :] = sels[k]
        gate_ref[k:k + 1, :] = exps[k] / denom

    chosen = (hots[0] | hots[1] | hots[2] | hots[3])
    chosen_bf = chosen.astype(F32).astype(BF16)
    s_io = lax.broadcasted_iota(jnp.int32, (tm, tm), 0)
    t_io = lax.broadcasted_iota(jnp.int32, (tm, tm), 1)
    upper = (s_io < t_io).astype(F32).astype(BF16)
    before = jnp.dot(chosen_bf, upper, preferred_element_type=F32) + run_ref[:, 0:1]
    for k in range(TOP_K):
        r = jnp.sum(jnp.where(hots[k], before, 0.0), axis=0, keepdims=True)
        rank_ref[k:k + 1, :] = r.astype(jnp.int32)
    run_ref[...] = run_ref[...] + jnp.sum(chosen.astype(F32), axis=1, keepdims=True)
    cnt_ref[...] = run_ref[...].astype(jnp.int32)


def _post_mixer(yp, ys, xp2, xs_t, wout_bf, g2, wr_t, br_col):
    d = yp.shape[1]
    n = yp.shape[0] + ys.shape[0]
    tm = TM_POST
    n_tiles = n // tm
    n_pt = xp2.shape[0] // tm
    kern = functools.partial(_post_mixer_kernel, n_prompt_tiles=n_pt)
    full = lambda shape: pl.BlockSpec(shape, lambda i: (0,) * len(shape))
    return pl.pallas_call(
        kern,
        grid=(n_tiles,),
        in_specs=[
            pl.BlockSpec((tm, d), lambda i: (jnp.minimum(i, n_pt - 1), 0)),
            pl.BlockSpec((tm, d), lambda i: (jnp.maximum(i - n_pt, 0), 0)),
            pl.BlockSpec((tm, d), lambda i: (jnp.minimum(i, n_pt - 1), 0)),
            pl.BlockSpec((tm, d), lambda i: (jnp.maximum(i - n_pt, 0), 0)),
            pl.BlockSpec(wout_bf.shape, lambda i: (0, 0), pipeline_mode=pl.Buffered(1)),
            full((1, d)),
            full((N_EXPERTS, d)),
            full((N_EXPERTS, 1)),
        ],
        out_specs=[
            pl.BlockSpec((tm, d), lambda i: (i, 0)),
            pl.BlockSpec((tm, d // 2), lambda i: (i, 0)),
            pl.BlockSpec((TOP_K, tm), lambda i: (0, i)),
            pl.BlockSpec((TOP_K, tm), lambda i: (0, i)),
            pl.BlockSpec((TOP_K, tm), lambda i: (0, i)),
            full((N_EXPERTS, LANES)),
        ],
        out_shape=[
            jax.ShapeDtypeStruct((n, d), F32),
            jax.ShapeDtypeStruct((n, d // 2), jnp.uint32),
            jax.ShapeDtypeStruct((TOP_K, n), jnp.int32),
            jax.ShapeDtypeStruct((TOP_K, n), F32),
            jax.ShapeDtypeStruct((TOP_K, n), jnp.int32),
            jax.ShapeDtypeStruct((N_EXPERTS, LANES), jnp.int32),
        ],
        scratch_shapes=[pltpu.VMEM((N_EXPERTS, LANES), F32)],
        compiler_params=pltpu.CompilerParams(
            dimension_semantics=("arbitrary",), vmem_limit_bytes=VMEM_LIMIT),
        name="post_mixer",
    )(yp, ys, xp2, xs_t, wout_bf, g2, wr_t, br_col)


def _dispatch_kernel(pos_ref, zf_ref, x_ref, xs_hbm, zbuf, sem, zsem):
    i = pl.program_id(0)
    tc = x_ref.shape[0]
    tm = zbuf.shape[0]
    n_tiles = xs_hbm.shape[0] // tm

    def zero_copy(t):
        return pltpu.make_async_copy(zbuf, xs_hbm.at[pl.ds(pl.multiple_of(t * tm, tm), tm)], zsem)

    @pl.when(i == 0)
    def _():
        zbuf[...] = jnp.zeros(zbuf.shape, zbuf.dtype)

        def start(t, carry):
            @pl.when(zf_ref[t] != 0)
            def _():
                zero_copy(t).start()
            return carry
        lax.fori_loop(0, n_tiles, start, 0)

        def wait(t, carry):
            @pl.when(zf_ref[t] != 0)
            def _():
                zero_copy(t).wait()
            return carry
        lax.fori_loop(0, n_tiles, wait, 0)

    base = i * (TOP_K * tc)

    def body(g, carry):
        r0 = g * ISSUE_UNROLL
        for u in range(ISSUE_UNROLL):
            for k in range(TOP_K):
                p = pos_ref[base + k * tc + r0 + u]
                pltpu.make_async_copy(x_ref.at[pl.ds(r0 + u, 1)], xs_hbm.at[pl.ds(p, 1)], sem).start(
                    priority=(u * TOP_K + k) % 2)
        return carry
    lax.fori_loop(0, tc // ISSUE_UNROLL, body, 0)
    for k in range(TOP_K):
        pltpu.make_async_copy(x_ref, xs_hbm.at[pl.ds(0, tc)], sem).wait()


def _dispatch(pos_tiles, zero_flags, xn_packed, p_max):
    n, dp = xn_packed.shape
    tc = TC_DISP
    return pl.pallas_call(
        _dispatch_kernel,
        grid_spec=pltpu.PrefetchScalarGridSpec(
            num_scalar_prefetch=2,
            grid=(n // tc,),
            in_specs=[pl.BlockSpec((tc, dp), lambda i, p, z: (i, 0))],
            out_specs=pl.BlockSpec(memory_space=pl.ANY),
            scratch_shapes=[pltpu.VMEM((TM_MOE, dp), jnp.uint32), pltpu.SemaphoreType.DMA(()),
                            pltpu.SemaphoreType.DMA(())],
        ),
        out_shape=jax.ShapeDtypeStruct((p_max, dp), jnp.uint32),
        compiler_params=pltpu.CompilerParams(
            dimension_semantics=("arbitrary",), vmem_limit_bytes=VMEM_LIMIT),
        name="dispatch",
    )(pos_tiles, zero_flags, xn_packed)


_FLAG_VALID = 1
_FLAG_NEW_WEIGHTS = 2
_CAST_ROWS = 256


def _cast_weight(w_ref, wbf_ref):
    def body(r, carry):
        r0 = pl.multiple_of(r * _CAST_ROWS, _CAST_ROWS)
        wbf_ref[pl.ds(r0, _CAST_ROWS), :] = w_ref[pl.ds(r0, _CAST_ROWS), :].astype(BF16)
        return carry
    lax.fori_loop(0, w_ref.shape[0] // _CAST_ROWS, body, 0)


def _moe_up_kernel(te_ref, tb_ref, fl_ref, x_ref, wg_ref, wu_ref, bg_ref, bu_ref, h_ref, wgb_ref, wub_ref):
    t = pl.program_id(1)
    flags = fl_ref[t]

    @pl.when((flags & _FLAG_NEW_WEIGHTS) != 0)
    def _():
        _cast_weight(wg_ref, wgb_ref)
        _cast_weight(wu_ref, wub_ref)

    @pl.when((flags & _FLAG_VALID) != 0)
    def _():
        x = _unpack_bf16_pairs(x_ref[...])
        gate = jnp.dot(x, wgb_ref[...], preferred_element_type=F32) + bg_ref[...]
        up = jnp.dot(x, wub_ref[...], preferred_element_type=F32) + bu_ref[...]
        gate = jnp.minimum(gate, SWIGLU_LIMIT)
        up = jnp.clip(up, -SWIGLU_LIMIT, SWIGLU_LIMIT)
        act = gate * _sigmoid(SWIGLU_ALPHA * gate) * (up + 1.0)
        h_ref[...] = act.astype(BF16)

    @pl.when((flags & _FLAG_VALID) == 0)
    def _():
        h_ref[...] = jnp.zeros(h_ref.shape, BF16)


def _moe_up(te, tb, fl, xs, w_gate_up, b_gate_up3):
    p_max, dp = xs.shape
    d = 2 * dp
    tm = TM_MOE
    n_tiles = p_max // tm
    n_j = D_FF // BF_UP
    return pl.pallas_call(
        _moe_up_kernel,
        grid_spec=pltpu.PrefetchScalarGridSpec(
            num_scalar_prefetch=3,
            grid=(n_j, n_tiles),
            in_specs=[
                pl.BlockSpec((tm, dp), lambda j, t, te, tb, fl: (tb[t], 0)),
                pl.BlockSpec((None, d, BF_UP), lambda j, t, te, tb, fl: (te[t], 0, j)),
                pl.BlockSpec((None, d, BF_UP), lambda j, t, te, tb, fl: (te[t], 0, n_j + j)),
                pl.BlockSpec((None, 1, BF_UP), lambda j, t, te, tb, fl: (te[t], 0, j)),
                pl.BlockSpec((None, 1, BF_UP), lambda j, t, te, tb, fl: (te[t], 0, n_j + j)),
            ],
            out_specs=pl.BlockSpec((tm, BF_UP), lambda j, t, te, tb, fl: (t, j)),
            scratch_shapes=[pltpu.VMEM((d, BF_UP), BF16), pltpu.VMEM((d, BF_UP), BF16)],
        ),
        out_shape=jax.ShapeDtypeStruct((p_max, D_FF), BF16),
        compiler_params=pltpu.CompilerParams(
            dimension_semantics=("arbitrary", "arbitrary"), vmem_limit_bytes=VMEM_LIMIT),
        name="moe_up",
    )(te, tb, fl, xs, w_gate_up, w_gate_up, b_gate_up3, b_gate_up3)


def _moe_down_kernel(te_ref, tb_ref, fl_ref, h_ref, wd_ref, bd_ref, y_ref, wdb_ref):
    t = pl.program_id(1)
    flags = fl_ref[t]

    @pl.when((flags & _FLAG_NEW_WEIGHTS) != 0)
    def _():
        _cast_weight(wd_ref, wdb_ref)

    @pl.when((flags & _FLAG_VALID) != 0)
    def _():
        y_ref[...] = jnp.dot(h_ref[...], wdb_ref[...], preferred_element_type=F32) + bd_ref[...]

    @pl.when((flags & _FLAG_VALID) == 0)
    def _():
        y_ref[...] = jnp.zeros(y_ref.shape, F32)


def _moe_down(te, tb, fl, h, w_down, b_down3):
    p_max, f = h.shape
    tm = TM_MOE
    n_tiles = p_max // tm
    n_j = D_MODEL // BN_DOWN
    return pl.pallas_call(
        _moe_down_kernel,
        grid_spec=pltpu.PrefetchScalarGridSpec(
            num_scalar_prefetch=3,
            grid=(n_j, n_tiles),
            in_specs=[
                pl.BlockSpec((tm, f), lambda j, t, te, tb, fl: (tb[t], 0)),
                pl.BlockSpec((None, f, BN_DOWN), lambda j, t, te, tb, fl: (te[t], 0, j)),
                pl.BlockSpec((None, 1, BN_DOWN), lambda j, t, te, tb, fl: (te[t], 0, j)),
            ],
            out_specs=pl.BlockSpec((tm, BN_DOWN), lambda j, t, te, tb, fl: (t, j)),
            scratch_shapes=[pltpu.VMEM((f, BN_DOWN), BF16)],
        ),
        out_shape=jax.ShapeDtypeStruct((p_max, D_MODEL), F32),
        compiler_params=pltpu.CompilerParams(
            dimension_semantics=("arbitrary", "arbitrary"), vmem_limit_bytes=VMEM_LIMIT),
        name="moe_down",
    )(te, tb, fl, h, w_down, b_down3)


def _combine_kernel(pos_ref, x1_ref, gate_ref, gf_ref, y_hbm, op_ref, os_ref, buf, sem, *, n_prompt_tiles):
    i = pl.program_id(0)
    n_i = pl.num_programs(0)
    tc = x1_ref.shape[0]
    rows = TOP_K * tc

    def issue(tile, slot):
        base = tile * rows

        def body(g, carry):
            r0 = g * ISSUE_UNROLL
            for u in range(ISSUE_UNROLL):
                p = pos_ref[base + r0 + u]
                pltpu.make_async_copy(y_hbm.at[pl.ds(p, 1)], buf.at[slot, pl.ds(r0 + u, 1)],
                                      sem.at[slot]).start(priority=u % 2)
            return carry
        lax.fori_loop(0, rows // ISSUE_UNROLL, body, 0)

    @pl.when(i == 0)
    def _():
        issue(0, 0)

    @pl.when(i + 1 < n_i)
    def _():
        issue(i + 1, (i + 1) % 2)

    slot = i % 2
    pltpu.make_async_copy(y_hbm.at[pl.ds(0, rows)], buf.at[slot], sem.at[slot]).wait()
    acc = x1_ref[...]
    for k in range(TOP_K):
        acc = acc + gate_ref[:, k:k + 1] * buf[slot, k * tc:(k + 1) * tc, :]
    out = _rms_rows(acc, gf_ref[...])

    @pl.when(i < n_prompt_tiles)
    def _():
        op_ref[...] = out

    @pl.when(i >= n_prompt_tiles)
    def _():
        os_ref[...] = out


def _combine(pos_flat, x1, gates_nk, gf, y_sorted, n_prompt):
    n, d = x1.shape
    tc = TC_COMB
    n_tiles = n // tc
    n_pt = n_prompt // tc
    kern = functools.partial(_combine_kernel, n_prompt_tiles=n_pt)
    return pl.pallas_call(
        kern,
        grid_spec=pltpu.PrefetchScalarGridSpec(
            num_scalar_prefetch=1,
            grid=(n_tiles,),
            in_specs=[
                pl.BlockSpec((tc, d), lambda i, p: (i, 0)),
                pl.BlockSpec((tc, TOP_K), lambda i, p: (i, 0)),
                pl.BlockSpec((1, d), lambda i, p: (0, 0)),
                pl.BlockSpec(memory_space=pl.ANY),
            ],
            out_specs=[
                pl.BlockSpec((tc, d), lambda i, p: (jnp.minimum(i, n_pt - 1), 0)),
                pl.BlockSpec((tc, d), lambda i, p: (jnp.maximum(i - n_pt, 0), 0)),
            ],
            scratch_shapes=[pltpu.VMEM((2, TOP_K * tc, d), F32), pltpu.SemaphoreType.DMA((2,))],
        ),
        out_shape=[
            jax.ShapeDtypeStruct((n_prompt, d), F32),
            jax.ShapeDtypeStruct((n - n_prompt, d), F32),
        ],
        compiler_params=pltpu.CompilerParams(
            dimension_semantics=("arbitrary",), vmem_limit_bytes=VMEM_LIMIT),
        name="combine",
    )(pos_flat, x1, gates_nk, gf, y_sorted)


def kernel(x_prompt, x_sample, state_conv_a, state_conv_b, meta_tokens, norm1_g, w_in, conv_a_w, conv_b_w,
           conv_b_b, ln_b_g, ln_b_b, w_out, norm2_g, w_router, b_router, w_gate_up, b_gate_up, w_down,
           b_down, final_norm_g):
    bp, seq, d = x_prompt.shape
    n_seq, n_t, _ = x_sample.shape
    n_prompt = bp * seq
    n_sample = n_seq * n_t
    n = n_prompt + n_sample
    assert norm1_g.shape[0] == 1, "single layer"
    assert seq % T_MIX == 0 and n_prompt % TM_POST == 0 and n_sample == TM_POST
    assert n_prompt % n_sample == 0 and n % TC_COMB == 0 and n_prompt % TC_COMB == 0

    g1 = norm1_g[0][None]
    win_bf = w_in[0].astype(BF16)
    wout_bf = w_out[0].astype(BF16)
    caw, cbw = conv_a_w[0], conv_b_w[0]
    cbb, lng, lnb = conv_b_b[0][None], ln_b_g[0][None], ln_b_b[0][None]

    xs_t = jnp.transpose(x_sample, (1, 0, 2)).reshape(n_sample, d)
    sa_t = jnp.transpose(state_conv_a[0], (1, 0, 2))
    sb_t = jnp.transpose(state_conv_b[0], (1, 0, 2))

    ymix_p, pa, pb = _mixer_prompt(x_prompt, meta_tokens, g1, win_bf, caw, cbw, cbb, lng, lnb)
    ymix_s, na_t, glu_t = _mixer_sample(xs_t, g1, win_bf, sa_t, sb_t, caw, cbw, cbb, lng, lnb, n_seq, n_t)

    x1, xn, idx, gates, rank, cnt = _post_mixer(
        ymix_p, ymix_s, x_prompt.reshape(n_prompt, d), xs_t, wout_bf, norm2_g[0][None],
        jnp.transpose(w_router[0]), b_router[0][:, None])

    tm = TM_MOE
    n_assign = n * TOP_K
    n_tiles = (n_assign + N_EXPERTS * (tm - 1)) // tm
    p_max = n_tiles * tm
    counts = cnt[:, 0]
    tiles_per_e = (counts + tm - 1) // tm
    tile_end = jnp.cumsum(tiles_per_e)
    tile_start = tile_end - tiles_per_e
    n_used = tile_end[-1]
    e_ar = jnp.arange(N_EXPERTS, dtype=jnp.int32)
    start_of = jnp.sum(jnp.where(idx[None] == e_ar[:, None, None], tile_start[:, None, None], 0), axis=0)
    pos = (start_of * tm + rank).astype(jnp.int32)
    tid = jnp.arange(n_tiles, dtype=jnp.int32)
    tb = jnp.minimum(tid, n_used - 1).astype(jnp.int32)
    te = jnp.minimum(jnp.sum((tile_end[None, :] <= tb[:, None]).astype(jnp.int32), axis=1),
                     N_EXPERTS - 1).astype(jnp.int32)
    valid = tid < n_used
    new_w = valid & ((tid == 0) | (te != jnp.roll(te, 1)))
    fl = (valid.astype(jnp.int32) * _FLAG_VALID + new_w.astype(jnp.int32) * _FLAG_NEW_WEIGHTS)
    partial = (counts % tm) != 0
    zero_flags = ((tid >= n_used) | jnp.any((tid[:, None] == (tile_end - 1)[None, :]) & partial[None, :],
                                            axis=1)).astype(jnp.int32)

    by_tile = lambda tc: pos.reshape(TOP_K, n // tc, tc).transpose(1, 0, 2).reshape(-1)
    xs_sorted = _dispatch(by_tile(TC_DISP), zero_flags, xn, p_max)
    h = _moe_up(te, tb, fl, xs_sorted, w_gate_up[0], b_gate_up[0][:, None, :])
    y_sorted = _moe_down(te, tb, fl, h, w_down[0], b_down[0][:, None, :])

    yp, ys_t = _combine(by_tile(TC_COMB), x1, jnp.transpose(gates), final_norm_g[None], y_sorted, n_prompt)

    y_prompt = yp.reshape(bp, seq, d)
    y_sample = jnp.transpose(ys_t.reshape(n_t, n_seq, d), (1, 0, 2))
    new_a_prompt = pa[None]
    new_b_prompt = pb[None]
    new_a_sample = jnp.transpose(na_t, (1, 0, 2))[None]
    glu_s = jnp.transpose(glu_t, (1, 0, 2))
    new_b_sample = jnp.concatenate([state_conv_b[0][:, n_t:], glu_s], axis=1)[None]
    return (y_prompt, y_sample, new_a_prompt, new_b_prompt, new_a_sample, new_b_sample)
```

```python
import functools

import jax
import jax.numpy as jnp
from jax import lax
from jax.experimental import pallas as pl
from jax.experimental.pallas import tpu as pltpu

F32 = jnp.float32
BF16 = jnp.bfloat16

D_MODEL = 2048
N_META = 16
C_A = 1024
C_B = 1024
K_A = 3
K_B = 31
N_EXPERTS = 32
TOP_K = 4
D_FF = 2048
SWIGLU_LIMIT = 7.0
SWIGLU_ALPHA = 1.702
RMS_EPS = 1e-5
LN_EPS = 1e-5

LANES = 128
N_CHUNK = C_B // LANES
HALO_B = 32
HALO_A = 8
T_MIX = 256
TM_POST = 512
TM_MOE = 256
BF_UP = 1024
BN_DOWN = 1024
TC_COMB = 128
TC_DISP = 512
ISSUE_UNROLL = 8
VMEM_LIMIT = 56 * 1024 * 1024


def _sigmoid(x):
    return jax.nn.sigmoid(x)


def _rms_rows(x, g):
    ms = jnp.mean(x * x, axis=-1, keepdims=True)
    return (x * lax.rsqrt(ms + RMS_EPS)) * g


def _pack_bf16_pairs(x):
    c = x.shape[1] // 2
    hi = lax.bitcast_convert_type(x[:, :c].astype(BF16).astype(F32), jnp.uint32)
    lo = lax.bitcast_convert_type(x[:, c:].astype(BF16).astype(F32), jnp.uint32)
    return hi | (lo >> 16)


def _unpack_bf16_pairs(p):
    hi = lax.bitcast_convert_type(p & jnp.uint32(0xFFFF0000), F32).astype(BF16)
    lo = lax.bitcast_convert_type(p << 16, F32).astype(BF16)
    return jnp.concatenate([hi, lo], axis=1)


def _mixer_prompt_kernel(x_ref, meta_ref, g1_ref, win_ref, caw_ref, cbw_ref, cbb_ref, lng_ref, lnb_ref,
                         y_ref, pa_ref, pb_ref,
                         u_ref, gb_ref, cv_ref, mgb_ref, mcv_ref, bc_ref):
    b = pl.program_id(0)
    i = pl.program_id(1)
    n_i = pl.num_programs(1)
    t = T_MIX

    def in_proj(rows):
        h = _rms_rows(rows, g1_ref[...]).astype(BF16)
        return jnp.dot(h, win_ref[...], preferred_element_type=F32)

    @pl.when((b == 0) & (i == 0))
    def _():
        um = in_proj(meta_ref[...])
        cvm = um[:, C_A:2 * C_A] * um[:, 2 * C_A:3 * C_A]
        glum = um[:, 3 * C_A:3 * C_A + C_B] * _sigmoid(um[:, 3 * C_A + C_B:])
        for c in range(N_CHUNK):
            sl = slice(c * LANES, (c + 1) * LANES)
            mgb_ref[c, 0:HALO_B - N_META, :] = jnp.zeros((HALO_B - N_META, LANES), F32)
            mgb_ref[c, HALO_B - N_META:HALO_B, :] = glum[:, sl]
            mcv_ref[c] = cvm[N_META - HALO_A:, sl]

    @pl.when(i == 0)
    def _():
        gb_ref[:, 0:HALO_B, :] = mgb_ref[...]
        cv_ref[:, 0:HALO_A, :] = mcv_ref[...]

    u_ref[...] = in_proj(x_ref[...])

    for c in range(N_CHUNK):
        lo = c * LANES
        bg = u_ref[:, lo:lo + LANES]
        cg = u_ref[:, C_A + lo:C_A + lo + LANES]
        v = u_ref[:, 2 * C_A + lo:2 * C_A + lo + LANES]
        ga = u_ref[:, 3 * C_A + lo:3 * C_A + lo + LANES]
        gbv = u_ref[:, 3 * C_A + C_B + lo:3 * C_A + C_B + lo + LANES]
        cv_ref[c, HALO_A:, :] = cg * v
        gb_ref[c, HALO_B:, :] = ga * _sigmoid(gbv)
        acc = cv_ref[c, HALO_A - 2:HALO_A - 2 + t, :] * caw_ref[0:1, lo:lo + LANES]
        for k in range(1, K_A):
            acc = acc + cv_ref[c, HALO_A - 2 + k:HALO_A - 2 + k + t, :] * caw_ref[k:k + 1, lo:lo + LANES]
        y_ref[:, lo:lo + LANES] = (bg * acc).astype(BF16)
        off = HALO_B - (K_B - 1)
        accb = gb_ref[c, off:off + t, :] * cbw_ref[0:1, lo:lo + LANES]
        for k in range(1, K_B):
            accb = accb + gb_ref[c, off + k:off + k + t, :] * cbw_ref[k:k + 1, lo:lo + LANES]
        bc_ref[:, lo:lo + LANES] = accb + cbb_ref[:, lo:lo + LANES]

    rb = 64
    for r in range(t // rb):
        xb = bc_ref[r * rb:(r + 1) * rb, :]
        mu = jnp.mean(xb, axis=-1, keepdims=True)
        var = jnp.mean(jnp.square(xb - mu), axis=-1, keepdims=True)
        bn = (xb - mu) * lax.rsqrt(var + LN_EPS) * lng_ref[...] + lnb_ref[...]
        y_ref[r * rb:(r + 1) * rb, C_A:] = (bn * _sigmoid(bn)).astype(BF16)

    @pl.when(i == n_i - 1)
    def _():
        for c in range(N_CHUNK):
            sl = slice(c * LANES, (c + 1) * LANES)
            pa_ref[:, sl] = cv_ref[c, HALO_A + t - (K_A - 1):HALO_A + t, :]
            pb_ref[:, sl] = gb_ref[c, HALO_B + t - (K_B - 1):HALO_B + t, :]

    gb_ref[:, 0:HALO_B, :] = gb_ref[:, t:t + HALO_B, :]
    cv_ref[:, 0:HALO_A, :] = cv_ref[:, t:t + HALO_A, :]


def _mixer_prompt(x_prompt, meta, g1, win_bf, caw, cbw, cbb, lng, lnb):
    bp, seq, d = x_prompt.shape
    n_i = seq // T_MIX
    full = lambda shape: pl.BlockSpec(shape, lambda b, i: (0,) * len(shape))
    return pl.pallas_call(
        _mixer_prompt_kernel,
        grid=(bp, n_i),
        in_specs=[
            pl.BlockSpec((None, T_MIX, d), lambda b, i: (b, i, 0)),
            full((N_META, d)),
            full((1, d)),
            pl.BlockSpec(win_bf.shape, lambda b, i: (0, 0), pipeline_mode=pl.Buffered(1)),
            full((K_A, C_A)),
            full((K_B, C_B)),
            full((1, C_B)),
            full((1, C_B)),
            full((1, C_B)),
        ],
        out_specs=[
            pl.BlockSpec((T_MIX, d), lambda b, i: (b * n_i + i, 0)),
            pl.BlockSpec((None, K_A - 1, C_A), lambda b, i: (b, 0, 0)),
            pl.BlockSpec((None, K_B - 1, C_B), lambda b, i: (b, 0, 0)),
        ],
        out_shape=[
            jax.ShapeDtypeStruct((bp * seq, d), BF16),
            jax.ShapeDtypeStruct((bp, K_A - 1, C_A), F32),
            jax.ShapeDtypeStruct((bp, K_B - 1, C_B), F32),
        ],
        scratch_shapes=[
            pltpu.VMEM((T_MIX, win_bf.shape[1]), F32),
            pltpu.VMEM((N_CHUNK, T_MIX + HALO_B, LANES), F32),
            pltpu.VMEM((N_CHUNK, T_MIX + HALO_A, LANES), F32),
            pltpu.VMEM((N_CHUNK, HALO_B, LANES), F32),
            pltpu.VMEM((N_CHUNK, HALO_A, LANES), F32),
            pltpu.VMEM((T_MIX, C_B), F32),
        ],
        compiler_params=pltpu.CompilerParams(
            dimension_semantics=("arbitrary", "arbitrary"), vmem_limit_bytes=VMEM_LIMIT),
        name="mixer_prompt",
    )(x_prompt, meta, g1, win_bf, caw, cbw, cbb, lng, lnb)


def _mixer_sample_kernel(x_ref, g1_ref, wbg_ref, wcg_ref, wv_ref, wga_ref, wgb_ref, sa_ref, sb_ref,
                         caw_ref, cbw_ref, cbb_ref, lng_ref, lnb_ref,
                         y_ref, na_ref, glu_ref,
                         h_ref, ya_ref, bc_ref, *, n_seq, n_t):
    c = pl.program_id(0)

    @pl.when(c == 0)
    def _():
        h_ref[...] = _rms_rows(x_ref[...], g1_ref[...]).astype(BF16)

    @pl.when(c < N_CHUNK)
    def _():
        h = h_ref[...]
        proj = lambda w_ref: jnp.dot(h, w_ref[...], preferred_element_type=F32)
        bg, cg, v, ga, gbv = proj(wbg_ref), proj(wcg_ref), proj(wv_ref), proj(wga_ref), proj(wgb_ref)
        cv = cg * v
        glu = ga * _sigmoid(gbv)
        row = lambda a, tt: a[tt * n_seq:(tt + 1) * n_seq, :]
        xa = [sa_ref[j] for j in range(K_A - 1)] + [row(cv, tt) for tt in range(n_t)]
        xb = [sb_ref[j] for j in range(K_B - 1)] + [row(glu, tt) for tt in range(n_t)]
        for tt in range(n_t):
            acc = xa[tt] * caw_ref[0:1, :]
            for k in range(1, K_A):
                acc = acc + xa[tt + k] * caw_ref[k:k + 1, :]
            ya_ref[c, tt * n_seq:(tt + 1) * n_seq, :] = row(bg, tt) * acc
            accb = xb[tt] * cbw_ref[0:1, :]
            for k in range(1, K_B):
                accb = accb + xb[tt + k] * cbw_ref[k:k + 1, :]
            bc_ref[c, tt * n_seq:(tt + 1) * n_seq, :] = accb + cbb_ref[...]
            glu_ref[tt] = row(glu, tt)
        for j in range(K_A - 1):
            na_ref[j] = row(cv, n_t - (K_A - 1) + j)

    @pl.when(c == N_CHUNK)
    def _():
        xb = jnp.concatenate([bc_ref[cc] for cc in range(N_CHUNK)], axis=1)
        mu = jnp.mean(xb, axis=-1, keepdims=True)
        var = jnp.mean(jnp.square(xb - mu), axis=-1, keepdims=True)
        bn = (xb - mu) * lax.rsqrt(var + LN_EPS) * lng_ref[...] + lnb_ref[...]
        for cc in range(N_CHUNK):
            y_ref[:, cc * LANES:(cc + 1) * LANES] = ya_ref[cc].astype(BF16)
        y_ref[:, C_A:] = (bn * _sigmoid(bn)).astype(BF16)


def _mixer_sample(xs_t, g1, win_bf, sa_t, sb_t, caw, cbw, cbb, lng, lnb, n_seq, n_t):
    rows, d = xs_t.shape
    cc = lambda c: jnp.minimum(c, N_CHUNK - 1)
    wspec = lambda g: pl.BlockSpec((d, LANES), lambda c, g=g: (0, g * N_CHUNK + cc(c)))
    full = lambda shape: pl.BlockSpec(shape, lambda c: (0,) * len(shape))
    kern = functools.partial(_mixer_sample_kernel, n_seq=n_seq, n_t=n_t)
    return pl.pallas_call(
        kern,
        grid=(N_CHUNK + 1,),
        in_specs=[
            full((rows, d)),
            full((1, d)),
            wspec(0), wspec(1), wspec(2), wspec(3), wspec(4),
            pl.BlockSpec((K_A - 1, n_seq, LANES), lambda c: (0, 0, cc(c))),
            pl.BlockSpec((K_B - 1, n_seq, LANES), lambda c: (0, 0, cc(c))),
            pl.BlockSpec((K_A, LANES), lambda c: (0, cc(c))),
            pl.BlockSpec((K_B, LANES), lambda c: (0, cc(c))),
            pl.BlockSpec((1, LANES), lambda c: (0, cc(c))),
            full((1, C_B)),
            full((1, C_B)),
        ],
        out_specs=[
            full((rows, d)),
            pl.BlockSpec((K_A - 1, n_seq, LANES), lambda c: (0, 0, cc(c))),
            pl.BlockSpec((n_t, n_seq, LANES), lambda c: (0, 0, cc(c))),
        ],
        out_shape=[
            jax.ShapeDtypeStruct((rows, d), BF16),
            jax.ShapeDtypeStruct((K_A - 1, n_seq, C_A), F32),
            jax.ShapeDtypeStruct((n_t, n_seq, C_B), F32),
        ],
        scratch_shapes=[
            pltpu.VMEM((rows, d), BF16),
            pltpu.VMEM((N_CHUNK, rows, LANES), F32),
            pltpu.VMEM((N_CHUNK, rows, LANES), F32),
        ],
        compiler_params=pltpu.CompilerParams(
            dimension_semantics=("arbitrary",), vmem_limit_bytes=VMEM_LIMIT),
        name="mixer_sample",
    )(xs_t, g1, win_bf, win_bf, win_bf, win_bf, win_bf, sa_t, sb_t, caw, cbw, cbb, lng, lnb)


def _post_mixer_kernel(yp_ref, ys_ref, xp_ref, xs_ref, wout_ref, g2_ref, wr_ref, br_ref,
                       x1_ref, xn_ref, idx_ref, gate_ref, rank_ref, cnt_ref,
                       run_ref, *, n_prompt_tiles):
    i = pl.program_id(0)
    tm = yp_ref.shape[0]

    @pl.when(i == 0)
    def _():
        run_ref[...] = jnp.zeros(run_ref.shape, F32)

    is_prompt = i < n_prompt_tiles
    x = jnp.where(is_prompt, xp_ref[...], xs_ref[...])
    y = jnp.where(is_prompt, yp_ref[...], ys_ref[...])
    x1 = x + jnp.dot(y, wout_ref[...], preferred_element_type=F32)
    x1_ref[...] = x1
    xn = _rms_rows(x1, g2_ref[...])
    xn_ref[...] = _pack_bf16_pairs(xn)

    logits = lax.dot_general(wr_ref[...], xn, (((1,), (1,)), ((), ())),
                             precision=lax.Precision.HIGHEST, preferred_element_type=F32) + br_ref[...]
    eidx = lax.broadcasted_iota(jnp.int32, logits.shape, 0)
    work = logits
    vals, sels, hots = [], [], []
    for _ in range(TOP_K):
        m = jnp.max(work, axis=0, keepdims=True)
        sel = jnp.min(jnp.where(work == m, eidx, N_EXPERTS), axis=0, keepdims=True)
        hot = eidx == sel
        vals.append(m)
        sels.append(sel)
        hots.append(hot)
        work = jnp.where(hot, -jnp.inf, work)
    exps = [jnp.exp(v - vals[0]) for v in vals]
    denom = exps[0] + exps[1] + exps[2] + exps[3]
    for k in range(TOP_K):
        idx_ref[k:k + 1, :] = sels[k]
        gate_ref[k:k + 1, :] = exps[k] / denom

    chosen = (hots[0] | hots[1] | hots[2] | hots[3])
    chosen_bf = chosen.astype(F32).astype(BF16)
    s_io = lax.broadcasted_iota(jnp.int32, (tm, tm), 0)
    t_io = lax.broadcasted_iota(jnp.int32, (tm, tm), 1)
    upper = (s_io < t_io).astype(F32).astype(BF16)
    before = jnp.dot(chosen_bf, upper, preferred_element_type=F32) + run_ref[:, 0:1]
    for k in range(TOP_K):
        r = jnp.sum(jnp.where(hots[k], before, 0.0), axis=0, keepdims=True)
        rank_ref[k:k + 1, :] = r.astype(jnp.int32)
    run_ref[...] = run_ref[...] + jnp.sum(chosen.astype(F32), axis=1, keepdims=True)
    cnt_ref[...] = run_ref[...].astype(jnp.int32)


def _post_mixer(yp, ys, xp2, xs_t, wout_bf, g2, wr_t, br_col):
    d = yp.shape[1]
    n = yp.shape[0] + ys.shape[0]
    tm = TM_POST
    n_tiles = n // tm
    n_pt = xp2.shape[0] // tm
    kern = functools.partial(_post_mixer_kernel, n_prompt_tiles=n_pt)
    full = lambda shape: pl.BlockSpec(shape, lambda i: (0,) * len(shape))
    return pl.pallas_call(
        kern,
        grid=(n_tiles,),
        in_specs=[
            pl.BlockSpec((tm, d), lambda i: (jnp.minimum(i, n_pt - 1), 0)),
            pl.BlockSpec((tm, d), lambda i: (jnp.maximum(i - n_pt, 0), 0)),
            pl.BlockSpec((tm, d), lambda i: (jnp.minimum(i, n_pt - 1), 0)),
            pl.BlockSpec((tm, d), lambda i: (jnp.maximum(i - n_pt, 0), 0)),
            pl.BlockSpec(wout_bf.shape, lambda i: (0, 0), pipeline_mode=pl.Buffered(1)),
            full((1, d)),
            full((N_EXPERTS, d)),
            full((N_EXPERTS, 1)),
        ],
        out_specs=[
            pl.BlockSpec((tm, d), lambda i: (i, 0)),
            pl.BlockSpec((tm, d // 2), lambda i: (i, 0)),
            pl.BlockSpec((TOP_K, tm), lambda i: (0, i)),
            pl.BlockSpec((TOP_K, tm), lambda i: (0, i)),
            pl.BlockSpec((TOP_K, tm), lambda i: (0, i)),
            full((N_EXPERTS, LANES)),
        ],
        out_shape=[
            jax.ShapeDtypeStruct((n, d), F32),
            jax.ShapeDtypeStruct((n, d // 2), jnp.uint32),
            jax.ShapeDtypeStruct((TOP_K, n), jnp.int32),
            jax.ShapeDtypeStruct((TOP_K, n), F32),
            jax.ShapeDtypeStruct((TOP_K, n), jnp.int32),
            jax.ShapeDtypeStruct((N_EXPERTS, LANES), jnp.int32),
        ],
        scratch_shapes=[pltpu.VMEM((N_EXPERTS, LANES), F32)],
        compiler_params=pltpu.CompilerParams(
            dimension_semantics=("arbitrary",), vmem_limit_bytes=VMEM_LIMIT),
        name="post_mixer",
    )(yp, ys, xp2, xs_t, wout_bf, g2, wr_t, br_col)


def _dispatch_kernel(pos_ref, zf_ref, x_ref, xs_hbm, zbuf, sem, zsem):
    i = pl.program_id(0)
    tc = x_ref.shape[0]
    tm = zbuf.shape[0]
    n_tiles = xs_hbm.shape[0] // tm

    def zero_copy(t):
        return pltpu.make_async_copy(zbuf, xs_hbm.at[pl.ds(pl.multiple_of(t * tm, tm), tm)], zsem)

    @pl.when(i == 0)
    def _():
        zbuf[...] = jnp.zeros(zbuf.shape, zbuf.dtype)

        def start(t, carry):
            @pl.when(zf_ref[t] != 0)
            def _():
                zero_copy(t).start()
            return carry
        lax.fori_loop(0, n_tiles, start, 0)

        def wait(t, carry):
            @pl.when(zf_ref[t] != 0)
            def _():
                zero_copy(t).wait()
            return carry
        lax.fori_loop(0, n_tiles, wait, 0)

    base = i * (TOP_K * tc)

    def body(g, carry):
        r0 = g * ISSUE_UNROLL
        for u in range(ISSUE_UNROLL):
            for k in range(TOP_K):
                p = pos_ref[base + k * tc + r0 + u]
                pltpu.make_async_copy(x_ref.at[pl.ds(r0 + u, 1)], xs_hbm.at[pl.ds(p, 1)], sem).start(
                    priority=(u * TOP_K + k) % 2)
        return carry
    lax.fori_loop(0, tc // ISSUE_UNROLL, body, 0)
    for k in range(TOP_K):
        pltpu.make_async_copy(x_ref, xs_hbm.at[pl.ds(0, tc)], sem).wait()


def _dispatch(pos_tiles, zero_flags, xn_packed, p_max):
    n, dp = xn_packed.shape
    tc = TC_DISP
    return pl.pallas_call(
        _dispatch_kernel,
        grid_spec=pltpu.PrefetchScalarGridSpec(
            num_scalar_prefetch=2,
            grid=(n // tc,),
            in_specs=[pl.BlockSpec((tc, dp), lambda i, p, z: (i, 0))],
            out_specs=pl.BlockSpec(memory_space=pl.ANY),
            scratch_shapes=[pltpu.VMEM((TM_MOE, dp), jnp.uint32), pltpu.SemaphoreType.DMA(()),
                            pltpu.SemaphoreType.DMA(())],
        ),
        out_shape=jax.ShapeDtypeStruct((p_max, dp), jnp.uint32),
        compiler_params=pltpu.CompilerParams(
            dimension_semantics=("arbitrary",), vmem_limit_bytes=VMEM_LIMIT),
        name="dispatch",
    )(pos_tiles, zero_flags, xn_packed)


_FLAG_VALID = 1
_FLAG_NEW_WEIGHTS = 2


def _advance_weights(te_ref, ne_ref, fl_ref, slot_ref, copies):
    j = pl.program_id(0)
    t = pl.program_id(1)
    n_j = pl.num_programs(0)

    @pl.when((fl_ref[t] & _FLAG_NEW_WEIGHTS) != 0)
    def _():
        first = (j == 0) & (t == 0)

        @pl.when(first)
        def _():
            slot_ref[0] = 0
            for c in copies(te_ref[t], j, 0):
                c.start()

        @pl.when(jnp.logical_not(first))
        def _():
            slot_ref[0] = 1 - slot_ref[0]

        s = slot_ref[0]
        for c in copies(te_ref[t], j, s):
            c.wait()
        ne = ne_ref[t]

        @pl.when(ne >= 0)
        def _():
            for c in copies(ne, j, 1 - s):
                c.start()

        @pl.when((ne < 0) & (j + 1 < n_j))
        def _():
            for c in copies(te_ref[0], j + 1, 1 - s):
                c.start()


def _bf16_dot(x_bf, w_f32):
    return lax.dot_general(x_bf, w_f32, (((1,), (0,)), ((), ())), preferred_element_type=F32)


def _moe_up_kernel(te_ref, tb_ref, fl_ref, ne_ref, x_ref, bg_ref, bu_ref, w_hbm, h_ref, wbuf, sem, slot_ref):
    t = pl.program_id(1)
    flags = fl_ref[t]

    def copies(e, j, slot):
        cg = pl.multiple_of(j * BF_UP, BF_UP)
        cu = pl.multiple_of(D_FF + j * BF_UP, BF_UP)
        return (pltpu.make_async_copy(w_hbm.at[e, :, pl.ds(cg, BF_UP)], wbuf.at[slot, 0], sem.at[slot]),
                pltpu.make_async_copy(w_hbm.at[e, :, pl.ds(cu, BF_UP)], wbuf.at[slot, 1], sem.at[slot]))

    _advance_weights(te_ref, ne_ref, fl_ref, slot_ref, copies)

    @pl.when((flags & _FLAG_VALID) != 0)
    def _():
        s = slot_ref[0]
        x = _unpack_bf16_pairs(x_ref[...])
        gate = _bf16_dot(x, wbuf[s, 0]) + bg_ref[...]
        up = _bf16_dot(x, wbuf[s, 1]) + bu_ref[...]
        gate = jnp.minimum(gate, SWIGLU_LIMIT)
        up = jnp.clip(up, -SWIGLU_LIMIT, SWIGLU_LIMIT)
        act = gate * _sigmoid(SWIGLU_ALPHA * gate) * (up + 1.0)
        h_ref[...] = act.astype(BF16)

    @pl.when((flags & _FLAG_VALID) == 0)
    def _():
        h_ref[...] = jnp.zeros(h_ref.shape, BF16)


def _moe_up(te, tb, fl, ne, xs, w_gate_up, b_gate_up3):
    p_max, dp = xs.shape
    d = 2 * dp
    tm = TM_MOE
    n_tiles = p_max // tm
    n_j = D_FF // BF_UP
    return pl.pallas_call(
        _moe_up_kernel,
        grid_spec=pltpu.PrefetchScalarGridSpec(
            num_scalar_prefetch=4,
            grid=(n_j, n_tiles),
            in_specs=[
                pl.BlockSpec((tm, dp), lambda j, t, te, tb, fl, ne: (tb[t], 0)),
                pl.BlockSpec((None, 1, BF_UP), lambda j, t, te, tb, fl, ne: (te[t], 0, j)),
                pl.BlockSpec((None, 1, BF_UP), lambda j, t, te, tb, fl, ne: (te[t], 0, n_j + j)),
                pl.BlockSpec(memory_space=pl.ANY),
            ],
            out_specs=pl.BlockSpec((tm, BF_UP), lambda j, t, te, tb, fl, ne: (t, j)),
            scratch_shapes=[pltpu.VMEM((2, 2, d, BF_UP), F32), pltpu.SemaphoreType.DMA((2,)),
                            pltpu.SMEM((1,), jnp.int32)],
        ),
        out_shape=jax.ShapeDtypeStruct((p_max, D_FF), BF16),
        compiler_params=pltpu.CompilerParams(
            dimension_semantics=("arbitrary", "arbitrary"), vmem_limit_bytes=VMEM_LIMIT),
        name="moe_up",
    )(te, tb, fl, ne, xs, b_gate_up3, b_gate_up3, w_gate_up)


def _moe_down_kernel(te_ref, tb_ref, fl_ref, ne_ref, h_ref, bd_ref, w_hbm, y_ref, wbuf, sem, slot_ref):
    t = pl.program_id(1)
    flags = fl_ref[t]

    def copies(e, j, slot):
        c0 = pl.multiple_of(j * BN_DOWN, BN_DOWN)
        return (pltpu.make_async_copy(w_hbm.at[e, :, pl.ds(c0, BN_DOWN)], wbuf.at[slot], sem.at[slot]),)

    _advance_weights(te_ref, ne_ref, fl_ref, slot_ref, copies)

    @pl.when((flags & _FLAG_VALID) != 0)
    def _():
        y_ref[...] = _bf16_dot(h_ref[...], wbuf[slot_ref[0]]) + bd_ref[...]

    @pl.when((flags & _FLAG_VALID) == 0)
    def _():
        y_ref[...] = jnp.zeros(y_ref.shape, F32)


def _moe_down(te, tb, fl, ne, h, w_down, b_down3):
    p_max, f = h.shape
    tm = TM_MOE
    n_tiles = p_max // tm
    n_j = D_MODEL // BN_DOWN
    return pl.pallas_call(
        _moe_down_kernel,
        grid_spec=pltpu.PrefetchScalarGridSpec(
            num_scalar_prefetch=4,
            grid=(n_j, n_tiles),
            in_specs=[
                pl.BlockSpec((tm, f), lambda j, t, te, tb, fl, ne: (tb[t], 0)),
                pl.BlockSpec((None, 1, BN_DOWN), lambda j, t, te, tb, fl, ne: (te[t], 0, j)),
                pl.BlockSpec(memory_space=pl.ANY),
            ],
            out_specs=pl.BlockSpec((tm, BN_DOWN), lambda j, t, te, tb, fl, ne: (t, j)),
            scratch_shapes=[pltpu.VMEM((2, f, BN_DOWN), F32), pltpu.SemaphoreType.DMA((2,)),
                            pltpu.SMEM((1,), jnp.int32)],
        ),
        out_shape=jax.ShapeDtypeStruct((p_max, D_MODEL), F32),
        compiler_params=pltpu.CompilerParams(
            dimension_semantics=("arbitrary", "arbitrary"), vmem_limit_bytes=VMEM_LIMIT),
        name="moe_down",
    )(te, tb, fl, ne, h, b_down3, w_down)


def _combine_kernel(pos_ref, x1_ref, gate_ref, gf_ref, y_hbm, op_ref, os_ref, buf, sem, *, n_prompt_tiles):
    i = pl.program_id(0)
    n_i = pl.num_programs(0)
    tc = x1_ref.shape[0]
    rows = TOP_K * tc

    def issue(tile, slot):
        base = tile * rows

        def body(g, carry):
            r0 = g * ISSUE_UNROLL
            for u in range(ISSUE_UNROLL):
                p = pos_ref[base + r0 + u]
                pltpu.make_async_copy(y_hbm.at[pl.ds(p, 1)], buf.at[slot, pl.ds(r0 + u, 1)],
                                      sem.at[slot]).start(priority=u % 2)
            return carry
        lax.fori_loop(0, rows // ISSUE_UNROLL, body, 0)

    @pl.when(i == 0)
    def _():
        issue(0, 0)

    @pl.when(i + 1 < n_i)
    def _():
        issue(i + 1, (i + 1) % 2)

    slot = i % 2
    pltpu.make_async_copy(y_hbm.at[pl.ds(0, rows)], buf.at[slot], sem.at[slot]).wait()
    acc = x1_ref[...]
    for k in range(TOP_K):
        acc = acc + gate_ref[:, k:k + 1] * buf[slot, k * tc:(k + 1) * tc, :]
    out = _rms_rows(acc, gf_ref[...])

    @pl.when(i < n_prompt_tiles)
    def _():
        op_ref[...] = out

    @pl.when(i >= n_prompt_tiles)
    def _():
        os_ref[...] = out


def _combine(pos_flat, x1, gates_nk, gf, y_sorted, n_prompt):
    n, d = x1.shape
    tc = TC_COMB
    n_tiles = n // tc
    n_pt = n_prompt // tc
    kern = functools.partial(_combine_kernel, n_prompt_tiles=n_pt)
    return pl.pallas_call(
        kern,
        grid_spec=pltpu.PrefetchScalarGridSpec(
            num_scalar_prefetch=1,
            grid=(n_tiles,),
            in_specs=[
                pl.BlockSpec((tc, d), lambda i, p: (i, 0)),
                pl.BlockSpec((tc, TOP_K), lambda i, p: (i, 0)),
                pl.BlockSpec((1, d), lambda i, p: (0, 0)),
                pl.BlockSpec(memory_space=pl.ANY),
            ],
            out_specs=[
                pl.BlockSpec((tc, d), lambda i, p: (jnp.minimum(i, n_pt - 1), 0)),
                pl.BlockSpec((tc, d), lambda i, p: (jnp.maximum(i - n_pt, 0), 0)),
            ],
            scratch_shapes=[pltpu.VMEM((2, TOP_K * tc, d), F32), pltpu.SemaphoreType.DMA((2,))],
        ),
        out_shape=[
            jax.ShapeDtypeStruct((n_prompt, d), F32),
            jax.ShapeDtypeStruct((n - n_prompt, d), F32),
        ],
        compiler_params=pltpu.CompilerParams(
            dimension_semantics=("arbitrary",), vmem_limit_bytes=VMEM_LIMIT),
        name="combine",
    )(pos_flat, x1, gates_nk, gf, y_sorted)


def kernel(x_prompt, x_sample, state_conv_a, state_conv_b, meta_tokens, norm1_g, w_in, conv_a_w, conv_b_w,
           conv_b_b, ln_b_g, ln_b_b, w_out, norm2_g, w_router, b_router, w_gate_up, b_gate_up, w_down,
           b_down, final_norm_g):
    bp, seq, d = x_prompt.shape
    n_seq, n_t, _ = x_sample.shape
    n_prompt = bp * seq
    n_sample = n_seq * n_t
    n = n_prompt + n_sample
    assert norm1_g.shape[0] == 1, "single layer"
    assert seq % T_MIX == 0 and n_prompt % TM_POST == 0 and n_sample == TM_POST
    assert n_prompt % n_sample == 0 and n % TC_COMB == 0 and n_prompt % TC_COMB == 0

    g1 = norm1_g[0][None]
    win_bf = w_in[0].astype(BF16)
    wout_bf = w_out[0].astype(BF16)
    caw, cbw = conv_a_w[0], conv_b_w[0]
    cbb, lng, lnb = conv_b_b[0][None], ln_b_g[0][None], ln_b_b[0][None]

    xs_t = jnp.transpose(x_sample, (1, 0, 2)).reshape(n_sample, d)
    sa_t = jnp.transpose(state_conv_a[0], (1, 0, 2))
    sb_t = jnp.transpose(state_conv_b[0], (1, 0, 2))

    ymix_p, pa, pb = _mixer_prompt(x_prompt, meta_tokens, g1, win_bf, caw, cbw, cbb, lng, lnb)
    ymix_s, na_t, glu_t = _mixer_sample(xs_t, g1, win_bf, sa_t, sb_t, caw, cbw, cbb, lng, lnb, n_seq, n_t)

    x1, xn, idx, gates, rank, cnt = _post_mixer(
        ymix_p, ymix_s, x_prompt.reshape(n_prompt, d), xs_t, wout_bf, norm2_g[0][None],
        jnp.transpose(w_router[0]), b_router[0][:, None])

    tm = TM_MOE
    n_assign = n * TOP_K
    n_tiles = (n_assign + N_EXPERTS * (tm - 1)) // tm
    p_max = n_tiles * tm
    counts = cnt[:, 0]
    tiles_per_e = (counts + tm - 1) // tm
    tile_end = jnp.cumsum(tiles_per_e)
    tile_start = tile_end - tiles_per_e
    n_used = tile_end[-1]
    e_ar = jnp.arange(N_EXPERTS, dtype=jnp.int32)
    start_of = jnp.sum(jnp.where(idx[None] == e_ar[:, None, None], tile_start[:, None, None], 0), axis=0)
    pos = (start_of * tm + rank).astype(jnp.int32)
    tid = jnp.arange(n_tiles, dtype=jnp.int32)
    tb = jnp.maximum(jnp.minimum(tid, n_used - 1), 0).astype(jnp.int32)
    expert_of = lambda tile: jnp.minimum(
        jnp.sum((tile_end[None, :] <= tile[:, None]).astype(jnp.int32), axis=1), N_EXPERTS - 1).astype(jnp.int32)
    te = expert_of(tb)
    valid = tid < n_used
    new_w = valid & ((tid == 0) | (te != jnp.roll(te, 1)))
    fl = (valid.astype(jnp.int32) * _FLAG_VALID + new_w.astype(jnp.int32) * _FLAG_NEW_WEIGHTS)
    next_tile = jnp.sum(jnp.where(te[:, None] == e_ar[None, :], tile_end[None, :], 0), axis=1)
    ne = jnp.where(next_tile < n_used, expert_of(next_tile), -1).astype(jnp.int32)
    partial = (counts % tm) != 0
    zero_flags = ((tid >= n_used) | jnp.any((tid[:, None] == (tile_end - 1)[None, :]) & partial[None, :],
                                            axis=1)).astype(jnp.int32)

    by_tile = lambda tc: pos.reshape(TOP_K, n // tc, tc).transpose(1, 0, 2).reshape(-1)
    xs_sorted = _dispatch(by_tile(TC_DISP), zero_flags, xn, p_max)
    h = _moe_up(te, tb, fl, ne, xs_sorted, w_gate_up[0], b_gate_up[0][:, None, :])
    y_sorted = _moe_down(te, tb, fl, ne, h, w_down[0], b_down[0][:, None, :])

    yp, ys_t = _combine(by_tile(TC_COMB), x1, jnp.transpose(gates), final_norm_g[None], y_sorted, n_prompt)

    y_prompt = yp.reshape(bp, seq, d)
    y_sample = jnp.transpose(ys_t.reshape(n_t, n_seq, d), (1, 0, 2))
    new_a_prompt = pa[None]
    new_b_prompt = pb[None]
    new_a_sample = jnp.transpose(na_t, (1, 0, 2))[None]
    glu_s = jnp.transpose(glu_t, (1, 0, 2))
    new_b_sample = jnp.concatenate([state_conv_b[0][:, n_t:], glu_s], axis=1)[None]
    return (y_prompt, y_sample, new_a_prompt, new_b_prompt, new_a_sample, new_b_sample)
```

```python
import functools

import jax
import jax.numpy as jnp
from jax import lax
from jax.experimental import pallas as pl
from jax.experimental.pallas import tpu as pltpu

F32 = jnp.float32
BF16 = jnp.bfloat16

D_MODEL = 2048
N_META = 16
C_A = 1024
C_B = 1024
K_A = 3
K_B = 31
N_EXPERTS = 32
TOP_K = 4
D_FF = 2048
SWIGLU_LIMIT = 7.0
SWIGLU_ALPHA = 1.702
RMS_EPS = 1e-5
LN_EPS = 1e-5

LANES = 128
N_CHUNK = C_B // LANES
HALO_B = 32
HALO_A = 8
T_MIX = 256
TM_POST = 512
TM_MOE = 256
BF_UP = 1024
BN_DOWN = 2048
ROW_SUB = D_MODEL // 2 // LANES
TC_COMB = 128
TC_DISP = 512
ISSUE_UNROLL = 8
VMEM_LIMIT = 56 * 1024 * 1024


def _sigmoid(x):
    return jax.nn.sigmoid(x)


def _rms_rows(x, g):
    ms = jnp.mean(x * x, axis=-1, keepdims=True)
    return (x * lax.rsqrt(ms + RMS_EPS)) * g


def _pack_bf16_pairs(x):
    c = x.shape[1] // 2
    hi = lax.bitcast_convert_type(x[:, :c].astype(BF16).astype(F32), jnp.uint32)
    lo = lax.bitcast_convert_type(x[:, c:].astype(BF16).astype(F32), jnp.uint32)
    return hi | (lo >> 16)


def _unpack_hi_lo(p):
    hi = lax.bitcast_convert_type(p & jnp.uint32(0xFFFF0000), F32)
    lo = lax.bitcast_convert_type(p << 16, F32)
    return hi, lo


def _store_row_tiles(ref, packed):
    r = packed.shape[0]
    for s in range(ROW_SUB):
        ref[pl.ds(s, r, stride=ROW_SUB), :] = packed[:, s * LANES:(s + 1) * LANES]


def _load_row_tiles(ref, first_row, r):
    return [ref[pl.ds(first_row * ROW_SUB + s, r, stride=ROW_SUB), :] for s in range(ROW_SUB)]


def _mixer_prompt_kernel(x_ref, meta_ref, g1_ref, win_ref, caw_ref, cbw_ref, cbb_ref, lng_ref, lnb_ref,
                         y_ref, pa_ref, pb_ref,
                         u_ref, gb_ref, cv_ref, mgb_ref, mcv_ref, bc_ref):
    b = pl.program_id(0)
    i = pl.program_id(1)
    n_i = pl.num_programs(1)
    t = T_MIX

    def in_proj(rows):
        h = _rms_rows(rows, g1_ref[...]).astype(BF16)
        return jnp.dot(h, win_ref[...], preferred_element_type=F32)

    @pl.when((b == 0) & (i == 0))
    def _():
        um = in_proj(meta_ref[...])
        cvm = um[:, C_A:2 * C_A] * um[:, 2 * C_A:3 * C_A]
        glum = um[:, 3 * C_A:3 * C_A + C_B] * _sigmoid(um[:, 3 * C_A + C_B:])
        for c in range(N_CHUNK):
            sl = slice(c * LANES, (c + 1) * LANES)
            mgb_ref[c, 0:HALO_B - N_META, :] = jnp.zeros((HALO_B - N_META, LANES), F32)
            mgb_ref[c, HALO_B - N_META:HALO_B, :] = glum[:, sl]
            mcv_ref[c] = cvm[N_META - HALO_A:, sl]

    @pl.when(i == 0)
    def _():
        gb_ref[:, 0:HALO_B, :] = mgb_ref[...]
        cv_ref[:, 0:HALO_A, :] = mcv_ref[...]

    u_ref[...] = in_proj(x_ref[...])

    for c in range(N_CHUNK):
        lo = c * LANES
        bg = u_ref[:, lo:lo + LANES]
        cg = u_ref[:, C_A + lo:C_A + lo + LANES]
        v = u_ref[:, 2 * C_A + lo:2 * C_A + lo + LANES]
        ga = u_ref[:, 3 * C_A + lo:3 * C_A + lo + LANES]
        gbv = u_ref[:, 3 * C_A + C_B + lo:3 * C_A + C_B + lo + LANES]
        cv_ref[c, HALO_A:, :] = cg * v
        gb_ref[c, HALO_B:, :] = ga * _sigmoid(gbv)
        acc = cv_ref[c, HALO_A - 2:HALO_A - 2 + t, :] * caw_ref[0:1, lo:lo + LANES]
        for k in range(1, K_A):
            acc = acc + cv_ref[c, HALO_A - 2 + k:HALO_A - 2 + k + t, :] * caw_ref[k:k + 1, lo:lo + LANES]
        y_ref[:, lo:lo + LANES] = (bg * acc).astype(BF16)
        off = HALO_B - (K_B - 1)
        accb = gb_ref[c, off:off + t, :] * cbw_ref[0:1, lo:lo + LANES]
        for k in range(1, K_B):
            accb = accb + gb_ref[c, off + k:off + k + t, :] * cbw_ref[k:k + 1, lo:lo + LANES]
        bc_ref[:, lo:lo + LANES] = accb + cbb_ref[:, lo:lo + LANES]

    rb = 64
    for r in range(t // rb):
        xb = bc_ref[r * rb:(r + 1) * rb, :]
        mu = jnp.mean(xb, axis=-1, keepdims=True)
        var = jnp.mean(jnp.square(xb - mu), axis=-1, keepdims=True)
        bn = (xb - mu) * lax.rsqrt(var + LN_EPS) * lng_ref[...] + lnb_ref[...]
        y_ref[r * rb:(r + 1) * rb, C_A:] = (bn * _sigmoid(bn)).astype(BF16)

    @pl.when(i == n_i - 1)
    def _():
        for c in range(N_CHUNK):
            sl = slice(c * LANES, (c + 1) * LANES)
            pa_ref[:, sl] = cv_ref[c, HALO_A + t - (K_A - 1):HALO_A + t, :]
            pb_ref[:, sl] = gb_ref[c, HALO_B + t - (K_B - 1):HALO_B + t, :]

    gb_ref[:, 0:HALO_B, :] = gb_ref[:, t:t + HALO_B, :]
    cv_ref[:, 0:HALO_A, :] = cv_ref[:, t:t + HALO_A, :]


def _mixer_prompt(x_prompt, meta, g1, win_bf, caw, cbw, cbb, lng, lnb):
    bp, seq, d = x_prompt.shape
    n_i = seq // T_MIX
    full = lambda shape: pl.BlockSpec(shape, lambda b, i: (0,) * len(shape))
    return pl.pallas_call(
        _mixer_prompt_kernel,
        grid=(bp, n_i),
        in_specs=[
            pl.BlockSpec((None, T_MIX, d), lambda b, i: (b, i, 0)),
            full((N_META, d)),
            full((1, d)),
            pl.BlockSpec(win_bf.shape, lambda b, i: (0, 0), pipeline_mode=pl.Buffered(1)),
            full((K_A, C_A)),
            full((K_B, C_B)),
            full((1, C_B)),
            full((1, C_B)),
            full((1, C_B)),
        ],
        out_specs=[
            pl.BlockSpec((T_MIX, d), lambda b, i: (b * n_i + i, 0)),
            pl.BlockSpec((None, K_A - 1, C_A), lambda b, i: (b, 0, 0)),
            pl.BlockSpec((None, K_B - 1, C_B), lambda b, i: (b, 0, 0)),
        ],
        out_shape=[
            jax.ShapeDtypeStruct((bp * seq, d), BF16),
            jax.ShapeDtypeStruct((bp, K_A - 1, C_A), F32),
            jax.ShapeDtypeStruct((bp, K_B - 1, C_B), F32),
        ],
        scratch_shapes=[
            pltpu.VMEM((T_MIX, win_bf.shape[1]), F32),
            pltpu.VMEM((N_CHUNK, T_MIX + HALO_B, LANES), F32),
            pltpu.VMEM((N_CHUNK, T_MIX + HALO_A, LANES), F32),
            pltpu.VMEM((N_CHUNK, HALO_B, LANES), F32),
            pltpu.VMEM((N_CHUNK, HALO_A, LANES), F32),
            pltpu.VMEM((T_MIX, C_B), F32),
        ],
        compiler_params=pltpu.CompilerParams(
            dimension_semantics=("arbitrary", "arbitrary"), vmem_limit_bytes=VMEM_LIMIT),
        name="mixer_prompt",
    )(x_prompt, meta, g1, win_bf, caw, cbw, cbb, lng, lnb)


def _mixer_sample_kernel(x_ref, g1_ref, wbg_ref, wcg_ref, wv_ref, wga_ref, wgb_ref, sa_ref, sb_ref,
                         caw_ref, cbw_ref, cbb_ref, lng_ref, lnb_ref,
                         y_ref, na_ref, glu_ref,
                         h_ref, ya_ref, bc_ref, *, n_seq, n_t):
    c = pl.program_id(0)

    @pl.when(c == 0)
    def _():
        h_ref[...] = _rms_rows(x_ref[...], g1_ref[...]).astype(BF16)

    @pl.when(c < N_CHUNK)
    def _():
        h = h_ref[...]
        proj = lambda w_ref: jnp.dot(h, w_ref[...], preferred_element_type=F32)
        bg, cg, v, ga, gbv = proj(wbg_ref), proj(wcg_ref), proj(wv_ref), proj(wga_ref), proj(wgb_ref)
        cv = cg * v
        glu = ga * _sigmoid(gbv)
        row = lambda a, tt: a[tt * n_seq:(tt + 1) * n_seq, :]
        xa = [sa_ref[j] for j in range(K_A - 1)] + [row(cv, tt) for tt in range(n_t)]
        xb = [sb_ref[j] for j in range(K_B - 1)] + [row(glu, tt) for tt in range(n_t)]
        for tt in range(n_t):
            acc = xa[tt] * caw_ref[0:1, :]
            for k in range(1, K_A):
                acc = acc + xa[tt + k] * caw_ref[k:k + 1, :]
            ya_ref[c, tt * n_seq:(tt + 1) * n_seq, :] = row(bg, tt) * acc
            accb = xb[tt] * cbw_ref[0:1, :]
            for k in range(1, K_B):
                accb = accb + xb[tt + k] * cbw_ref[k:k + 1, :]
            bc_ref[c, tt * n_seq:(tt + 1) * n_seq, :] = accb + cbb_ref[...]
            glu_ref[tt] = row(glu, tt)
        for j in range(K_A - 1):
            na_ref[j] = row(cv, n_t - (K_A - 1) + j)

    @pl.when(c == N_CHUNK)
    def _():
        xb = jnp.concatenate([bc_ref[cc] for cc in range(N_CHUNK)], axis=1)
        mu = jnp.mean(xb, axis=-1, keepdims=True)
        var = jnp.mean(jnp.square(xb - mu), axis=-1, keepdims=True)
        bn = (xb - mu) * lax.rsqrt(var + LN_EPS) * lng_ref[...] + lnb_ref[...]
        for cc in range(N_CHUNK):
            y_ref[:, cc * LANES:(cc + 1) * LANES] = ya_ref[cc].astype(BF16)
        y_ref[:, C_A:] = (bn * _sigmoid(bn)).astype(BF16)


def _mixer_sample(xs_t, g1, win_bf, sa_t, sb_t, caw, cbw, cbb, lng, lnb, n_seq, n_t):
    rows, d = xs_t.shape
    cc = lambda c: jnp.minimum(c, N_CHUNK - 1)
    wspec = lambda g: pl.BlockSpec((d, LANES), lambda c, g=g: (0, g * N_CHUNK + cc(c)))
    full = lambda shape: pl.BlockSpec(shape, lambda c: (0,) * len(shape))
    kern = functools.partial(_mixer_sample_kernel, n_seq=n_seq, n_t=n_t)
    return pl.pallas_call(
        kern,
        grid=(N_CHUNK + 1,),
        in_specs=[
            full((rows, d)),
            full((1, d)),
            wspec(0), wspec(1), wspec(2), wspec(3), wspec(4),
            pl.BlockSpec((K_A - 1, n_seq, LANES), lambda c: (0, 0, cc(c))),
            pl.BlockSpec((K_B - 1, n_seq, LANES), lambda c: (0, 0, cc(c))),
            pl.BlockSpec((K_A, LANES), lambda c: (0, cc(c))),
            pl.BlockSpec((K_B, LANES), lambda c: (0, cc(c))),
            pl.BlockSpec((1, LANES), lambda c: (0, cc(c))),
            full((1, C_B)),
            full((1, C_B)),
        ],
        out_specs=[
            full((rows, d)),
            pl.BlockSpec((K_A - 1, n_seq, LANES), lambda c: (0, 0, cc(c))),
            pl.BlockSpec((n_t, n_seq, LANES), lambda c: (0, 0, cc(c))),
        ],
        out_shape=[
            jax.ShapeDtypeStruct((rows, d), BF16),
            jax.ShapeDtypeStruct((K_A - 1, n_seq, C_A), F32),
            jax.ShapeDtypeStruct((n_t, n_seq, C_B), F32),
        ],
        scratch_shapes=[
            pltpu.VMEM((rows, d), BF16),
            pltpu.VMEM((N_CHUNK, rows, LANES), F32),
            pltpu.VMEM((N_CHUNK, rows, LANES), F32),
        ],
        compiler_params=pltpu.CompilerParams(
            dimension_semantics=("arbitrary",), vmem_limit_bytes=VMEM_LIMIT),
        name="mixer_sample",
    )(xs_t, g1, win_bf, win_bf, win_bf, win_bf, win_bf, sa_t, sb_t, caw, cbw, cbb, lng, lnb)


def _post_mixer_kernel(yp_ref, ys_ref, xp_ref, xs_ref, wout_ref, g2_ref, wr_ref, br_ref,
                       x1_ref, xn_ref, idx_ref, gate_ref, rank_ref, cnt_ref,
                       run_ref, *, n_prompt_tiles):
    i = pl.program_id(0)
    tm = yp_ref.shape[0]

    @pl.when(i == 0)
    def _():
        run_ref[...] = jnp.zeros(run_ref.shape, F32)

    is_prompt = i < n_prompt_tiles
    x = jnp.where(is_prompt, xp_ref[...], xs_ref[...])
    y = jnp.where(is_prompt, yp_ref[...], ys_ref[...])
    x1 = x + jnp.dot(y, wout_ref[...], preferred_element_type=F32)
    x1_ref[...] = x1
    xn = _rms_rows(x1, g2_ref[...])
    _store_row_tiles(xn_ref, _pack_bf16_pairs(xn))

    logits = lax.dot_general(wr_ref[...], xn, (((1,), (1,)), ((), ())),
                             precision=lax.Precision.HIGHEST, preferred_element_type=F32) + br_ref[...]
    eidx = lax.broadcasted_iota(jnp.int32, logits.shape, 0)
    work = logits
    vals, sels, hots = [], [], []
    for _ in range(TOP_K):
        m = jnp.max(work, axis=0, keepdims=True)
        sel = jnp.min(jnp.where(work == m, eidx, N_EXPERTS), axis=0, keepdims=True)
        hot = eidx == sel
        vals.append(m)
        sels.append(sel)
        hots.append(hot)
        work = jnp.where(hot, -jnp.inf, work)
    exps = [jnp.exp(v - vals[0]) for v in vals]
    denom = exps[0] + exps[1] + exps[2] + exps[3]
    for k in range(TOP_K):
        idx_ref[k:k + 1, :] = sels[k]
        gate_ref[k:k + 1, :] = exps[k] / denom

    chosen = (hots[0] | hots[1] | hots[2] | hots[3])
    chosen_bf = chosen.astype(F32).astype(BF16)
    s_io = lax.broadcasted_iota(jnp.int32, (tm, tm), 0)
    t_io = lax.broadcasted_iota(jnp.int32, (tm, tm), 1)
    upper = (s_io < t_io).astype(F32).astype(BF16)
    before = jnp.dot(chosen_bf, upper, preferred_element_type=F32) + run_ref[:, 0:1]
    for k in range(TOP_K):
        r = jnp.sum(jnp.where(hots[k], before, 0.0), axis=0, keepdims=True)
        rank_ref[k:k + 1, :] = r.astype(jnp.int32)
    run_ref[...] = run_ref[...] + jnp.sum(chosen.astype(F32), axis=1, keepdims=True)
    cnt_ref[...] = run_ref[...].astype(jnp.int32)


def _post_mixer(yp, ys, xp2, xs_t, wout_bf, g2, wr_t, br_col):
    d = yp.shape[1]
    n = yp.shape[0] + ys.shape[0]
    tm = TM_POST
    n_tiles = n // tm
    n_pt = xp2.shape[0] // tm
    kern = functools.partial(_post_mixer_kernel, n_prompt_tiles=n_pt)
    full = lambda shape: pl.BlockSpec(shape, lambda i: (0,) * len(shape))
    return pl.pallas_call(
        kern,
        grid=(n_tiles,),
        in_specs=[
            pl.BlockSpec((tm, d), lambda i: (jnp.minimum(i, n_pt - 1), 0)),
            pl.BlockSpec((tm, d), lambda i: (jnp.maximum(i - n_pt, 0), 0)),
            pl.BlockSpec((tm, d), lambda i: (jnp.minimum(i, n_pt - 1), 0)),
            pl.BlockSpec((tm, d), lambda i: (jnp.maximum(i - n_pt, 0), 0)),
            pl.BlockSpec(wout_bf.shape, lambda i: (0, 0), pipeline_mode=pl.Buffered(1)),
            full((1, d)),
            full((N_EXPERTS, d)),
            full((N_EXPERTS, 1)),
        ],
        out_specs=[
            pl.BlockSpec((tm, d), lambda i: (i, 0)),
            pl.BlockSpec((tm * ROW_SUB, LANES), lambda i: (i, 0)),
            pl.BlockSpec((TOP_K, tm), lambda i: (0, i)),
            pl.BlockSpec((TOP_K, tm), lambda i: (0, i)),
            pl.BlockSpec((TOP_K, tm), lambda i: (0, i)),
            full((N_EXPERTS, LANES)),
        ],
        out_shape=[
            jax.ShapeDtypeStruct((n, d), F32),
            jax.ShapeDtypeStruct((n * ROW_SUB, LANES), jnp.uint32),
            jax.ShapeDtypeStruct((TOP_K, n), jnp.int32),
            jax.ShapeDtypeStruct((TOP_K, n), F32),
            jax.ShapeDtypeStruct((TOP_K, n), jnp.int32),
            jax.ShapeDtypeStruct((N_EXPERTS, LANES), jnp.int32),
        ],
        scratch_shapes=[pltpu.VMEM((N_EXPERTS, LANES), F32)],
        compiler_params=pltpu.CompilerParams(
            dimension_semantics=("arbitrary",), vmem_limit_bytes=VMEM_LIMIT),
        name="post_mixer",
    )(yp, ys, xp2, xs_t, wout_bf, g2, wr_t, br_col)


def _row_tile(ref, row):
    return ref.at[pl.ds(pl.multiple_of(row * ROW_SUB, ROW_SUB), ROW_SUB)]


def _dispatch_kernel(pos_ref, zf_ref, x_ref, xs_hbm, zbuf, sem, zsem):
    i = pl.program_id(0)
    tc = x_ref.shape[0] // ROW_SUB
    zrows = zbuf.shape[0]
    n_tiles = xs_hbm.shape[0] // zrows

    def zero_copy(t):
        return pltpu.make_async_copy(zbuf, xs_hbm.at[pl.ds(pl.multiple_of(t * zrows, zrows), zrows)], zsem)

    @pl.when(i == 0)
    def _():
        zbuf[...] = jnp.zeros(zbuf.shape, zbuf.dtype)

        def start(t, carry):
            @pl.when(zf_ref[t] != 0)
            def _():
                zero_copy(t).start()
            return carry
        lax.fori_loop(0, n_tiles, start, 0)

        def wait(t, carry):
            @pl.when(zf_ref[t] != 0)
            def _():
                zero_copy(t).wait()
            return carry
        lax.fori_loop(0, n_tiles, wait, 0)

    base = i * (TOP_K * tc)

    def body(g, carry):
        r0 = g * ISSUE_UNROLL
        for u in range(ISSUE_UNROLL):
            for k in range(TOP_K):
                p = pos_ref[base + k * tc + r0 + u]
                pltpu.make_async_copy(_row_tile(x_ref, r0 + u), _row_tile(xs_hbm, p), sem).start(
                    priority=(u * TOP_K + k) % 2)
        return carry
    lax.fori_loop(0, tc // ISSUE_UNROLL, body, 0)
    for k in range(TOP_K):
        pltpu.make_async_copy(x_ref, xs_hbm.at[pl.ds(0, tc * ROW_SUB)], sem).wait()


def _dispatch(pos_tiles, zero_flags, xn_tiles, p_max):
    n = xn_tiles.shape[0] // ROW_SUB
    tc = TC_DISP
    return pl.pallas_call(
        _dispatch_kernel,
        grid_spec=pltpu.PrefetchScalarGridSpec(
            num_scalar_prefetch=2,
            grid=(n // tc,),
            in_specs=[pl.BlockSpec((tc * ROW_SUB, LANES), lambda i, p, z: (i, 0))],
            out_specs=pl.BlockSpec(memory_space=pl.ANY),
            scratch_shapes=[pltpu.VMEM((TM_MOE * ROW_SUB, LANES), jnp.uint32), pltpu.SemaphoreType.DMA(()),
                            pltpu.SemaphoreType.DMA(())],
        ),
        out_shape=jax.ShapeDtypeStruct((p_max * ROW_SUB, LANES), jnp.uint32),
        compiler_params=pltpu.CompilerParams(
            dimension_semantics=("arbitrary",), vmem_limit_bytes=VMEM_LIMIT),
        name="dispatch",
    )(pos_tiles, zero_flags, xn_tiles)


_FLAG_VALID = 1
_FLAG_NEW_WEIGHTS = 2


def _advance_weights(te_ref, ne_ref, fl_ref, slot_ref, copies):
    j = pl.program_id(0)
    t = pl.program_id(1)
    n_j = pl.num_programs(0)

    @pl.when((fl_ref[t] & _FLAG_NEW_WEIGHTS) != 0)
    def _():
        first = (j == 0) & (t == 0)

        @pl.when(first)
        def _():
            slot_ref[0] = 0
            for c in copies(te_ref[t], j, 0):
                c.start()

        @pl.when(jnp.logical_not(first))
        def _():
            slot_ref[0] = 1 - slot_ref[0]

        s = slot_ref[0]
        for c in copies(te_ref[t], j, s):
            c.wait()
        ne = ne_ref[t]

        @pl.when(ne >= 0)
        def _():
            for c in copies(ne, j, 1 - s):
                c.start()

        @pl.when((ne < 0) & (j + 1 < n_j))
        def _():
            for c in copies(te_ref[0], j + 1, 1 - s):
                c.start()


def _bf16_dot(x_bf, w_f32):
    return lax.dot_general(x_bf, w_f32, (((1,), (0,)), ((), ())), preferred_element_type=F32)


def _moe_up_kernel(te_ref, tb_ref, fl_ref, ne_ref, x_ref, bg_ref, bu_ref, w_hbm, h_ref, wbuf, sem, slot_ref):
    t = pl.program_id(1)
    flags = fl_ref[t]

    def copies(e, j, slot):
        cg = pl.multiple_of(j * BF_UP, BF_UP)
        cu = pl.multiple_of(D_FF + j * BF_UP, BF_UP)
        return (pltpu.make_async_copy(w_hbm.at[e, :, pl.ds(cg, BF_UP)], wbuf.at[slot, 0], sem.at[slot]),
                pltpu.make_async_copy(w_hbm.at[e, :, pl.ds(cu, BF_UP)], wbuf.at[slot, 1], sem.at[slot]))

    _advance_weights(te_ref, ne_ref, fl_ref, slot_ref, copies)

    @pl.when((flags & _FLAG_VALID) != 0)
    def _():
        s = slot_ref[0]
        halves = [_unpack_hi_lo(c) for c in _load_row_tiles(x_ref, 0, h_ref.shape[0])]
        x = jnp.concatenate([hl[0].astype(BF16) for hl in halves] + [hl[1].astype(BF16) for hl in halves], axis=1)
        gate = _bf16_dot(x, wbuf[s, 0]) + bg_ref[...]
        up = _bf16_dot(x, wbuf[s, 1]) + bu_ref[...]
        gate = jnp.minimum(gate, SWIGLU_LIMIT)
        up = jnp.clip(up, -SWIGLU_LIMIT, SWIGLU_LIMIT)
        act = gate * _sigmoid(SWIGLU_ALPHA * gate) * (up + 1.0)
        h_ref[...] = act.astype(BF16)

    @pl.when((flags & _FLAG_VALID) == 0)
    def _():
        h_ref[...] = jnp.zeros(h_ref.shape, BF16)


def _moe_up(te, tb, fl, ne, xs, w_gate_up, b_gate_up3):
    p_max = xs.shape[0] // ROW_SUB
    d = D_MODEL
    tm = TM_MOE
    n_tiles = p_max // tm
    n_j = D_FF // BF_UP
    return pl.pallas_call(
        _moe_up_kernel,
        grid_spec=pltpu.PrefetchScalarGridSpec(
            num_scalar_prefetch=4,
            grid=(n_j, n_tiles),
            in_specs=[
                pl.BlockSpec((tm * ROW_SUB, LANES), lambda j, t, te, tb, fl, ne: (tb[t], 0)),
                pl.BlockSpec((None, 1, BF_UP), lambda j, t, te, tb, fl, ne: (te[t], 0, j)),
                pl.BlockSpec((None, 1, BF_UP), lambda j, t, te, tb, fl, ne: (te[t], 0, n_j + j)),
                pl.BlockSpec(memory_space=pl.ANY),
            ],
            out_specs=pl.BlockSpec((tm, BF_UP), lambda j, t, te, tb, fl, ne: (t, j)),
            scratch_shapes=[pltpu.VMEM((2, 2, d, BF_UP), F32), pltpu.SemaphoreType.DMA((2,)),
                            pltpu.SMEM((1,), jnp.int32)],
        ),
        out_shape=jax.ShapeDtypeStruct((p_max, D_FF), BF16),
        compiler_params=pltpu.CompilerParams(
            dimension_semantics=("arbitrary", "arbitrary"), vmem_limit_bytes=VMEM_LIMIT),
        name="moe_up",
    )(te, tb, fl, ne, xs, b_gate_up3, b_gate_up3, w_gate_up)


def _moe_down_kernel(te_ref, tb_ref, fl_ref, ne_ref, h_ref, bd_ref, w_hbm, y_ref, wbuf, sem, slot_ref):
    t = pl.program_id(1)
    flags = fl_ref[t]

    def copies(e, j, slot):
        c0 = pl.multiple_of(j * BN_DOWN, BN_DOWN)
        return (pltpu.make_async_copy(w_hbm.at[e, :, pl.ds(c0, BN_DOWN)], wbuf.at[slot], sem.at[slot]),)

    _advance_weights(te_ref, ne_ref, fl_ref, slot_ref, copies)

    @pl.when((flags & _FLAG_VALID) != 0)
    def _():
        y = _bf16_dot(h_ref[...], wbuf[slot_ref[0]]) + bd_ref[...]
        _store_row_tiles(y_ref, _pack_bf16_pairs(y))

    @pl.when((flags & _FLAG_VALID) == 0)
    def _():
        y_ref[...] = jnp.zeros(y_ref.shape, y_ref.dtype)


def _moe_down(te, tb, fl, ne, h, w_down, b_down3):
    p_max, f = h.shape
    tm = TM_MOE
    n_tiles = p_max // tm
    n_j = D_MODEL // BN_DOWN
    return pl.pallas_call(
        _moe_down_kernel,
        grid_spec=pltpu.PrefetchScalarGridSpec(
            num_scalar_prefetch=4,
            grid=(n_j, n_tiles),
            in_specs=[
                pl.BlockSpec((tm, f), lambda j, t, te, tb, fl, ne: (tb[t], 0)),
                pl.BlockSpec((None, 1, BN_DOWN), lambda j, t, te, tb, fl, ne: (te[t], 0, j)),
                pl.BlockSpec(memory_space=pl.ANY),
            ],
            out_specs=pl.BlockSpec((tm * ROW_SUB, LANES), lambda j, t, te, tb, fl, ne: (t, 0)),
            scratch_shapes=[pltpu.VMEM((2, f, BN_DOWN), F32), pltpu.SemaphoreType.DMA((2,)),
                            pltpu.SMEM((1,), jnp.int32)],
        ),
        out_shape=jax.ShapeDtypeStruct((p_max * ROW_SUB, LANES), jnp.uint32),
        compiler_params=pltpu.CompilerParams(
            dimension_semantics=("arbitrary", "arbitrary"), vmem_limit_bytes=VMEM_LIMIT),
        name="moe_down",
    )(te, tb, fl, ne, h, b_down3, w_down)


def _combine_kernel(pos_ref, x1_ref, gate_ref, gf_ref, y_hbm, op_ref, os_ref, buf, sem, *, n_prompt_tiles):
    i = pl.program_id(0)
    n_i = pl.num_programs(0)
    tc = x1_ref.shape[0]
    rows = TOP_K * tc

    def issue(tile, slot):
        base = tile * rows

        def body(g, carry):
            r0 = g * ISSUE_UNROLL
            for u in range(ISSUE_UNROLL):
                p = pos_ref[base + r0 + u]
                pltpu.make_async_copy(_row_tile(y_hbm, p), _row_tile(buf.at[slot], r0 + u),
                                      sem.at[slot]).start(priority=u % 2)
            return carry
        lax.fori_loop(0, rows // ISSUE_UNROLL, body, 0)

    @pl.when(i == 0)
    def _():
        issue(0, 0)

    @pl.when(i + 1 < n_i)
    def _():
        issue(i + 1, (i + 1) % 2)

    slot = i % 2
    pltpu.make_async_copy(y_hbm.at[pl.ds(0, rows * ROW_SUB)], buf.at[slot], sem.at[slot]).wait()
    his = [None] * ROW_SUB
    los = [None] * ROW_SUB
    for k in range(TOP_K):
        g = gate_ref[:, k:k + 1]
        for s, chunk in enumerate(_load_row_tiles(buf.at[slot], k * tc, tc)):
            hi, lo = _unpack_hi_lo(chunk)
            his[s] = g * hi if k == 0 else his[s] + g * hi
            los[s] = g * lo if k == 0 else los[s] + g * lo
    acc = x1_ref[...] + jnp.concatenate(his + los, axis=1)
    out = _rms_rows(acc, gf_ref[...])

    @pl.when(i < n_prompt_tiles)
    def _():
        op_ref[...] = out

    @pl.when(i >= n_prompt_tiles)
    def _():
        os_ref[...] = out


def _combine(pos_flat, x1, gates_nk, gf, y_sorted, n_prompt):
    n, d = x1.shape
    tc = TC_COMB
    n_tiles = n // tc
    n_pt = n_prompt // tc
    kern = functools.partial(_combine_kernel, n_prompt_tiles=n_pt)
    return pl.pallas_call(
        kern,
        grid_spec=pltpu.PrefetchScalarGridSpec(
            num_scalar_prefetch=1,
            grid=(n_tiles,),
            in_specs=[
                pl.BlockSpec((tc, d), lambda i, p: (i, 0)),
                pl.BlockSpec((tc, TOP_K), lambda i, p: (i, 0)),
                pl.BlockSpec((1, d), lambda i, p: (0, 0)),
                pl.BlockSpec(memory_space=pl.ANY),
            ],
            out_specs=[
                pl.BlockSpec((tc, d), lambda i, p: (jnp.minimum(i, n_pt - 1), 0)),
                pl.BlockSpec((tc, d), lambda i, p: (jnp.maximum(i - n_pt, 0), 0)),
            ],
            scratch_shapes=[pltpu.VMEM((2, TOP_K * tc * ROW_SUB, LANES), jnp.uint32),
                            pltpu.SemaphoreType.DMA((2,))],
        ),
        out_shape=[
            jax.ShapeDtypeStruct((n_prompt, d), F32),
            jax.ShapeDtypeStruct((n - n_prompt, d), F32),
        ],
        compiler_params=pltpu.CompilerParams(
            dimension_semantics=("arbitrary",), vmem_limit_bytes=VMEM_LIMIT),
        name="combine",
    )(pos_flat, x1, gates_nk, gf, y_sorted)


def kernel(x_prompt, x_sample, state_conv_a, state_conv_b, meta_tokens, norm1_g, w_in, conv_a_w, conv_b_w,
           conv_b_b, ln_b_g, ln_b_b, w_out, norm2_g, w_router, b_router, w_gate_up, b_gate_up, w_down,
           b_down, final_norm_g):
    bp, seq, d = x_prompt.shape
    n_seq, n_t, _ = x_sample.shape
    n_prompt = bp * seq
    n_sample = n_seq * n_t
    n = n_prompt + n_sample
    assert norm1_g.shape[0] == 1, "single layer"
    assert seq % T_MIX == 0 and n_prompt % TM_POST == 0 and n_sample == TM_POST
    assert n_prompt % n_sample == 0 and n % TC_COMB == 0 and n_prompt % TC_COMB == 0

    g1 = norm1_g[0][None]
    win_bf = w_in[0].astype(BF16)
    wout_bf = w_out[0].astype(BF16)
    caw, cbw = conv_a_w[0], conv_b_w[0]
    cbb, lng, lnb = conv_b_b[0][None], ln_b_g[0][None], ln_b_b[0][None]

    xs_t = jnp.transpose(x_sample, (1, 0, 2)).reshape(n_sample, d)
    sa_t = jnp.transpose(state_conv_a[0], (1, 0, 2))
    sb_t = jnp.transpose(state_conv_b[0], (1, 0, 2))

    ymix_p, pa, pb = _mixer_prompt(x_prompt, meta_tokens, g1, win_bf, caw, cbw, cbb, lng, lnb)
    ymix_s, na_t, glu_t = _mixer_sample(xs_t, g1, win_bf, sa_t, sb_t, caw, cbw, cbb, lng, lnb, n_seq, n_t)

    x1, xn, idx, gates, rank, cnt = _post_mixer(
        ymix_p, ymix_s, x_prompt.reshape(n_prompt, d), xs_t, wout_bf, norm2_g[0][None],
        jnp.transpose(w_router[0]), b_router[0][:, None])

    tm = TM_MOE
    n_assign = n * TOP_K
    n_tiles = (n_assign + N_EXPERTS * (tm - 1)) // tm
    p_max = n_tiles * tm
    counts = cnt[:, 0]
    tiles_per_e = (counts + tm - 1) // tm
    tile_end = jnp.cumsum(tiles_per_e)
    tile_start = tile_end - tiles_per_e
    n_used = tile_end[-1]
    e_ar = jnp.arange(N_EXPERTS, dtype=jnp.int32)
    start_of = jnp.sum(jnp.where(idx[None] == e_ar[:, None, None], tile_start[:, None, None], 0), axis=0)
    pos = (start_of * tm + rank).astype(jnp.int32)
    tid = jnp.arange(n_tiles, dtype=jnp.int32)
    tb = jnp.maximum(jnp.minimum(tid, n_used - 1), 0).astype(jnp.int32)
    expert_of = lambda tile: jnp.minimum(
        jnp.sum((tile_end[None, :] <= tile[:, None]).astype(jnp.int32), axis=1), N_EXPERTS - 1).astype(jnp.int32)
    te = expert_of(tb)
    valid = tid < n_used
    new_w = valid & ((tid == 0) | (te != jnp.roll(te, 1)))
    fl = (valid.astype(jnp.int32) * _FLAG_VALID + new_w.astype(jnp.int32) * _FLAG_NEW_WEIGHTS)
    next_tile = jnp.sum(jnp.where(te[:, None] == e_ar[None, :], tile_end[None, :], 0), axis=1)
    ne = jnp.where(next_tile < n_used, expert_of(next_tile), -1).astype(jnp.int32)
    partial = (counts % tm) != 0
    zero_flags = ((tid >= n_used) | jnp.any((tid[:, None] == (tile_end - 1)[None, :]) & partial[None, :],
                                            axis=1)).astype(jnp.int32)

    by_tile = lambda tc: pos.reshape(TOP_K, n // tc, tc).transpose(1, 0, 2).reshape(-1)
    xs_sorted = _dispatch(by_tile(TC_DISP), zero_flags, xn, p_max)
    h = _moe_up(te, tb, fl, ne, xs_sorted, w_gate_up[0], b_gate_up[0][:, None, :])
    y_sorted = _moe_down(te, tb, fl, ne, h, w_down[0], b_down[0][:, None, :])

    yp, ys_t = _combine(by_tile(TC_COMB), x1, jnp.transpose(gates), final_norm_g[None], y_sorted, n_prompt)

    y_prompt = yp.reshape(bp, seq, d)
    y_sample = jnp.transpose(ys_t.reshape(n_t, n_seq, d), (1, 0, 2))
    new_a_prompt = pa[None]
    new_b_prompt = pb[None]
    new_a_sample = jnp.transpose(na_t, (1, 0, 2))[None]
    glu_s = jnp.transpose(glu_t, (1, 0, 2))
    new_b_sample = jnp.concatenate([state_conv_b[0][:, n_t:], glu_s], axis=1)[None]
    return (y_prompt, y_sample, new_a_prompt, new_b_prompt, new_a_sample, new_b_sample)
```

```python
import functools

import jax
import jax.numpy as jnp
from jax import lax
from jax.experimental import pallas as pl
from jax.experimental.pallas import tpu as pltpu

F32 = jnp.float32
BF16 = jnp.bfloat16

D_MODEL = 2048
N_META = 16
C_A = 1024
C_B = 1024
K_A = 3
K_B = 31
N_EXPERTS = 32
TOP_K = 4
D_FF = 2048
SWIGLU_LIMIT = 7.0
SWIGLU_ALPHA = 1.702
RMS_EPS = 1e-5
LN_EPS = 1e-5

LANES = 128
N_CHUNK = C_B // LANES
HALO_B = 32
HALO_A = 8
T_MIX = 256
TM_POST = 512
TM_MOE = 256
BF_UP = 1024
BN_DOWN = 2048
ROW_SUB = D_MODEL // 2 // LANES
TC_COMB = 128
TC_DISP = 512
ISSUE_UNROLL = 8
VMEM_LIMIT = 56 * 1024 * 1024


def _sigmoid(x):
    return jax.nn.sigmoid(x)


def _rms_rows(x, g):
    ms = jnp.mean(x * x, axis=-1, keepdims=True)
    return (x * lax.rsqrt(ms + RMS_EPS)) * g


def _pack_bf16_pairs(x):
    c = x.shape[1] // 2
    hi = lax.bitcast_convert_type(x[:, :c].astype(BF16).astype(F32), jnp.uint32)
    lo = lax.bitcast_convert_type(x[:, c:].astype(BF16).astype(F32), jnp.uint32)
    return hi | (lo >> 16)


def _unpack_hi_lo(p):
    hi = lax.bitcast_convert_type(p & jnp.uint32(0xFFFF0000), F32)
    lo = lax.bitcast_convert_type(p << 16, F32)
    return hi, lo


def _store_row_tiles(ref, packed):
    r = packed.shape[0]
    for s in range(ROW_SUB):
        ref[pl.ds(s, r, stride=ROW_SUB), :] = packed[:, s * LANES:(s + 1) * LANES]


def _load_row_tiles(ref, first_row, r):
    return [ref[pl.ds(first_row * ROW_SUB + s, r, stride=ROW_SUB), :] for s in range(ROW_SUB)]


def _mixer_prompt_kernel(x0_ref, xa_ref, xb_ref, meta_ref, g1_ref, win_ref, caw_ref, cbw_ref, cbb_ref,
                         lng_ref, lnb_ref,
                         y_ref, pa_ref, pb_ref,
                         ua_ref, ub_ref, gb_ref, cv_ref, mgb_ref, mcv_ref, bc_ref, *, pairs_per_seq):
    s = pl.program_id(0)
    t = T_MIX

    def in_proj(rows):
        h = _rms_rows(rows, g1_ref[...]).astype(BF16)
        return jnp.dot(h, win_ref[...], preferred_element_type=F32)

    @pl.when(s == 0)
    def _():
        um = in_proj(meta_ref[...])
        cvm = um[:, C_A:2 * C_A] * um[:, 2 * C_A:3 * C_A]
        glum = um[:, 3 * C_A:3 * C_A + C_B] * _sigmoid(um[:, 3 * C_A + C_B:])
        for c in range(N_CHUNK):
            sl = slice(c * LANES, (c + 1) * LANES)
            mgb_ref[c, 0:HALO_B - N_META, :] = jnp.zeros((HALO_B - N_META, LANES), F32)
            mgb_ref[c, HALO_B - N_META:HALO_B, :] = glum[:, sl]
            mcv_ref[c] = cvm[N_META - HALO_A:, sl]
        ua_ref[...] = in_proj(x0_ref[...])

    @pl.when(s % pairs_per_seq == 0)
    def _():
        gb_ref[:, 0:HALO_B, :] = mgb_ref[...]
        cv_ref[:, 0:HALO_A, :] = mcv_ref[...]

    def conv_stage(u_ref, row0):
        for c in range(N_CHUNK):
            lo = c * LANES
            bg = u_ref[:, lo:lo + LANES]
            cg = u_ref[:, C_A + lo:C_A + lo + LANES]
            v = u_ref[:, 2 * C_A + lo:2 * C_A + lo + LANES]
            ga = u_ref[:, 3 * C_A + lo:3 * C_A + lo + LANES]
            gbv = u_ref[:, 3 * C_A + C_B + lo:3 * C_A + C_B + lo + LANES]
            cv_ref[c, HALO_A:, :] = cg * v
            gb_ref[c, HALO_B:, :] = ga * _sigmoid(gbv)
            acc = cv_ref[c, HALO_A - 2:HALO_A - 2 + t, :] * caw_ref[0:1, lo:lo + LANES]
            for k in range(1, K_A):
                acc = acc + cv_ref[c, HALO_A - 2 + k:HALO_A - 2 + k + t, :] * caw_ref[k:k + 1, lo:lo + LANES]
            y_ref[row0:row0 + t, lo:lo + LANES] = (bg * acc).astype(BF16)
            off = HALO_B - (K_B - 1)
            accb = gb_ref[c, off:off + t, :] * cbw_ref[0:1, lo:lo + LANES]
            for k in range(1, K_B):
                accb = accb + gb_ref[c, off + k:off + k + t, :] * cbw_ref[k:k + 1, lo:lo + LANES]
            bc_ref[:, lo:lo + LANES] = accb + cbb_ref[:, lo:lo + LANES]
        rb = 64
        for r in range(t // rb):
            xb = bc_ref[r * rb:(r + 1) * rb, :]
            mu = jnp.mean(xb, axis=-1, keepdims=True)
            var = jnp.mean(jnp.square(xb - mu), axis=-1, keepdims=True)
            bn = (xb - mu) * lax.rsqrt(var + LN_EPS) * lng_ref[...] + lnb_ref[...]
            y_ref[row0 + r * rb:row0 + (r + 1) * rb, C_A:] = (bn * _sigmoid(bn)).astype(BF16)
        gb_ref[:, 0:HALO_B, :] = gb_ref[:, t:t + HALO_B, :]
        cv_ref[:, 0:HALO_A, :] = cv_ref[:, t:t + HALO_A, :]

    ub_ref[...] = in_proj(xa_ref[...])
    conv_stage(ua_ref, 0)
    ua_ref[...] = in_proj(xb_ref[...])
    conv_stage(ub_ref, t)

    @pl.when(s % pairs_per_seq == pairs_per_seq - 1)
    def _():
        for c in range(N_CHUNK):
            sl = slice(c * LANES, (c + 1) * LANES)
            pa_ref[:, sl] = cv_ref[c, HALO_A - (K_A - 1):HALO_A, :]
            pb_ref[:, sl] = gb_ref[c, HALO_B - (K_B - 1):HALO_B, :]


def _mixer_prompt(x_prompt, meta, g1, win_bf, caw, cbw, cbb, lng, lnb):
    bp, seq, d = x_prompt.shape
    n_i = seq // T_MIX
    n_tiles = bp * n_i
    pairs_per_seq = n_i // 2
    full = lambda shape: pl.BlockSpec(shape, lambda s: (0,) * len(shape))

    def x_tile(offset):
        def index(s):
            tile = jnp.minimum(2 * s + offset, n_tiles - 1)
            return (tile // n_i, tile % n_i, 0)
        return pl.BlockSpec((None, T_MIX, d), index)

    return pl.pallas_call(
        functools.partial(_mixer_prompt_kernel, pairs_per_seq=pairs_per_seq),
        grid=(n_tiles // 2,),
        in_specs=[
            pl.BlockSpec((None, T_MIX, d), lambda s: (0, 0, 0)),
            x_tile(1),
            x_tile(2),
            full((N_META, d)),
            full((1, d)),
            pl.BlockSpec(win_bf.shape, lambda s: (0, 0), pipeline_mode=pl.Buffered(1)),
            full((K_A, C_A)),
            full((K_B, C_B)),
            full((1, C_B)),
            full((1, C_B)),
            full((1, C_B)),
        ],
        out_specs=[
            pl.BlockSpec((2 * T_MIX, d), lambda s: (s, 0)),
            pl.BlockSpec((None, K_A - 1, C_A), lambda s: (s // pairs_per_seq, 0, 0)),
            pl.BlockSpec((None, K_B - 1, C_B), lambda s: (s // pairs_per_seq, 0, 0)),
        ],
        out_shape=[
            jax.ShapeDtypeStruct((bp * seq, d), BF16),
            jax.ShapeDtypeStruct((bp, K_A - 1, C_A), F32),
            jax.ShapeDtypeStruct((bp, K_B - 1, C_B), F32),
        ],
        scratch_shapes=[
            pltpu.VMEM((T_MIX, win_bf.shape[1]), F32),
            pltpu.VMEM((T_MIX, win_bf.shape[1]), F32),
            pltpu.VMEM((N_CHUNK, T_MIX + HALO_B, LANES), F32),
            pltpu.VMEM((N_CHUNK, T_MIX + HALO_A, LANES), F32),
            pltpu.VMEM((N_CHUNK, HALO_B, LANES), F32),
            pltpu.VMEM((N_CHUNK, HALO_A, LANES), F32),
            pltpu.VMEM((T_MIX, C_B), F32),
        ],
        compiler_params=pltpu.CompilerParams(
            dimension_semantics=("arbitrary",), vmem_limit_bytes=VMEM_LIMIT),
        name="mixer_prompt",
    )(x_prompt, x_prompt, x_prompt, meta, g1, win_bf, caw, cbw, cbb, lng, lnb)


def _mixer_sample_kernel(x_ref, g1_ref, wbg_ref, wcg_ref, wv_ref, wga_ref, wgb_ref, sa_ref, sb_ref,
                         caw_ref, cbw_ref, cbb_ref, lng_ref, lnb_ref,
                         y_ref, na_ref, glu_ref,
                         h_ref, ya_ref, bc_ref, *, n_seq, n_t):
    c = pl.program_id(0)

    @pl.when(c == 0)
    def _():
        h_ref[...] = _rms_rows(x_ref[...], g1_ref[...]).astype(BF16)

    @pl.when(c < N_CHUNK)
    def _():
        h = h_ref[...]
        proj = lambda w_ref: jnp.dot(h, w_ref[...], preferred_element_type=F32)
        bg, cg, v, ga, gbv = proj(wbg_ref), proj(wcg_ref), proj(wv_ref), proj(wga_ref), proj(wgb_ref)
        cv = cg * v
        glu = ga * _sigmoid(gbv)
        row = lambda a, tt: a[tt * n_seq:(tt + 1) * n_seq, :]
        xa = [sa_ref[j] for j in range(K_A - 1)] + [row(cv, tt) for tt in range(n_t)]
        xb = [sb_ref[j] for j in range(K_B - 1)] + [row(glu, tt) for tt in range(n_t)]
        for tt in range(n_t):
            acc = xa[tt] * caw_ref[0:1, :]
            for k in range(1, K_A):
                acc = acc + xa[tt + k] * caw_ref[k:k + 1, :]
            ya_ref[c, tt * n_seq:(tt + 1) * n_seq, :] = row(bg, tt) * acc
            accb = xb[tt] * cbw_ref[0:1, :]
            for k in range(1, K_B):
                accb = accb + xb[tt + k] * cbw_ref[k:k + 1, :]
            bc_ref[c, tt * n_seq:(tt + 1) * n_seq, :] = accb + cbb_ref[...]
            glu_ref[tt] = row(glu, tt)
        for j in range(K_A - 1):
            na_ref[j] = row(cv, n_t - (K_A - 1) + j)

    @pl.when(c == N_CHUNK)
    def _():
        xb = jnp.concatenate([bc_ref[cc] for cc in range(N_CHUNK)], axis=1)
        mu = jnp.mean(xb, axis=-1, keepdims=True)
        var = jnp.mean(jnp.square(xb - mu), axis=-1, keepdims=True)
        bn = (xb - mu) * lax.rsqrt(var + LN_EPS) * lng_ref[...] + lnb_ref[...]
        for cc in range(N_CHUNK):
            y_ref[:, cc * LANES:(cc + 1) * LANES] = ya_ref[cc].astype(BF16)
        y_ref[:, C_A:] = (bn * _sigmoid(bn)).astype(BF16)


def _mixer_sample(xs_t, g1, win_bf, sa_t, sb_t, caw, cbw, cbb, lng, lnb, n_seq, n_t):
    rows, d = xs_t.shape
    cc = lambda c: jnp.minimum(c, N_CHUNK - 1)
    wspec = lambda g: pl.BlockSpec((d, LANES), lambda c, g=g: (0, g * N_CHUNK + cc(c)))
    full = lambda shape: pl.BlockSpec(shape, lambda c: (0,) * len(shape))
    kern = functools.partial(_mixer_sample_kernel, n_seq=n_seq, n_t=n_t)
    return pl.pallas_call(
        kern,
        grid=(N_CHUNK + 1,),
        in_specs=[
            full((rows, d)),
            full((1, d)),
            wspec(0), wspec(1), wspec(2), wspec(3), wspec(4),
            pl.BlockSpec((K_A - 1, n_seq, LANES), lambda c: (0, 0, cc(c))),
            pl.BlockSpec((K_B - 1, n_seq, LANES), lambda c: (0, 0, cc(c))),
            pl.BlockSpec((K_A, LANES), lambda c: (0, cc(c))),
            pl.BlockSpec((K_B, LANES), lambda c: (0, cc(c))),
            pl.BlockSpec((1, LANES), lambda c: (0, cc(c))),
            full((1, C_B)),
            full((1, C_B)),
        ],
        out_specs=[
            full((rows, d)),
            pl.BlockSpec((K_A - 1, n_seq, LANES), lambda c: (0, 0, cc(c))),
            pl.BlockSpec((n_t, n_seq, LANES), lambda c: (0, 0, cc(c))),
        ],
        out_shape=[
            jax.ShapeDtypeStruct((rows, d), BF16),
            jax.ShapeDtypeStruct((K_A - 1, n_seq, C_A), F32),
            jax.ShapeDtypeStruct((n_t, n_seq, C_B), F32),
        ],
        scratch_shapes=[
            pltpu.VMEM((rows, d), BF16),
            pltpu.VMEM((N_CHUNK, rows, LANES), F32),
            pltpu.VMEM((N_CHUNK, rows, LANES), F32),
        ],
        compiler_params=pltpu.CompilerParams(
            dimension_semantics=("arbitrary",), vmem_limit_bytes=VMEM_LIMIT),
        name="mixer_sample",
    )(xs_t, g1, win_bf, win_bf, win_bf, win_bf, win_bf, sa_t, sb_t, caw, cbw, cbb, lng, lnb)


def _post_mixer_kernel(yp_ref, ys_ref, xp_ref, xs_ref, wout_ref, g2_ref, wr_ref, br_ref,
                       x1_ref, xn_ref, idx_ref, gate_ref, rank_ref, cnt_ref,
                       run_ref, *, n_prompt_tiles):
    i = pl.program_id(0)
    tm = yp_ref.shape[0]

    @pl.when(i == 0)
    def _():
        run_ref[...] = jnp.zeros(run_ref.shape, F32)

    is_prompt = i < n_prompt_tiles
    x = jnp.where(is_prompt, xp_ref[...], xs_ref[...])
    y = jnp.where(is_prompt, yp_ref[...], ys_ref[...])
    x1 = x + jnp.dot(y, wout_ref[...], preferred_element_type=F32)
    x1_ref[...] = x1
    xn = _rms_rows(x1, g2_ref[...])
    _store_row_tiles(xn_ref, _pack_bf16_pairs(xn))

    logits = lax.dot_general(wr_ref[...], xn, (((1,), (1,)), ((), ())),
                             precision=lax.Precision.HIGHEST, preferred_element_type=F32) + br_ref[...]
    eidx = lax.broadcasted_iota(jnp.int32, logits.shape, 0)
    work = logits
    vals, sels, hots = [], [], []
    for _ in range(TOP_K):
        m = jnp.max(work, axis=0, keepdims=True)
        sel = jnp.min(jnp.where(work == m, eidx, N_EXPERTS), axis=0, keepdims=True)
        hot = eidx == sel
        vals.append(m)
        sels.append(sel)
        hots.append(hot)
        work = jnp.where(hot, -jnp.inf, work)
    exps = [jnp.exp(v - vals[0]) for v in vals]
    denom = exps[0] + exps[1] + exps[2] + exps[3]
    for k in range(TOP_K):
        idx_ref[k:k + 1, :] = sels[k]
        gate_ref[k:k + 1, :] = exps[k] / denom

    chosen = (hots[0] | hots[1] | hots[2] | hots[3])
    chosen_bf = chosen.astype(F32).astype(BF16)
    s_io = lax.broadcasted_iota(jnp.int32, (tm, tm), 0)
    t_io = lax.broadcasted_iota(jnp.int32, (tm, tm), 1)
    upper = (s_io < t_io).astype(F32).astype(BF16)
    before = jnp.dot(chosen_bf, upper, preferred_element_type=F32) + run_ref[:, 0:1]
    for k in range(TOP_K):
        r = jnp.sum(jnp.where(hots[k], before, 0.0), axis=0, keepdims=True)
        rank_ref[k:k + 1, :] = r.astype(jnp.int32)
    run_ref[...] = run_ref[...] + jnp.sum(chosen.astype(F32), axis=1, keepdims=True)
    cnt_ref[...] = run_ref[...].astype(jnp.int32)


def _post_mixer(yp, ys, xp2, xs_t, wout_bf, g2, wr_t, br_col):
    d = yp.shape[1]
    n = yp.shape[0] + ys.shape[0]
    tm = TM_POST
    n_tiles = n // tm
    n_pt = xp2.shape[0] // tm
    kern = functools.partial(_post_mixer_kernel, n_prompt_tiles=n_pt)
    full = lambda shape: pl.BlockSpec(shape, lambda i: (0,) * len(shape))
    return pl.pallas_call(
        kern,
        grid=(n_tiles,),
        in_specs=[
            pl.BlockSpec((tm, d), lambda i: (jnp.minimum(i, n_pt - 1), 0)),
            pl.BlockSpec((tm, d), lambda i: (jnp.maximum(i - n_pt, 0), 0)),
            pl.BlockSpec((tm, d), lambda i: (jnp.minimum(i, n_pt - 1), 0)),
            pl.BlockSpec((tm, d), lambda i: (jnp.maximum(i - n_pt, 0), 0)),
            pl.BlockSpec(wout_bf.shape, lambda i: (0, 0), pipeline_mode=pl.Buffered(1)),
            full((1, d)),
            full((N_EXPERTS, d)),
            full((N_EXPERTS, 1)),
        ],
        out_specs=[
            pl.BlockSpec((tm, d), lambda i: (i, 0)),
            pl.BlockSpec((tm * ROW_SUB, LANES), lambda i: (i, 0)),
            pl.BlockSpec((TOP_K, tm), lambda i: (0, i)),
            pl.BlockSpec((TOP_K, tm), lambda i: (0, i)),
            pl.BlockSpec((TOP_K, tm), lambda i: (0, i)),
            full((N_EXPERTS, LANES)),
        ],
        out_shape=[
            jax.ShapeDtypeStruct((n, d), F32),
            jax.ShapeDtypeStruct((n * ROW_SUB, LANES), jnp.uint32),
            jax.ShapeDtypeStruct((TOP_K, n), jnp.int32),
            jax.ShapeDtypeStruct((TOP_K, n), F32),
            jax.ShapeDtypeStruct((TOP_K, n), jnp.int32),
            jax.ShapeDtypeStruct((N_EXPERTS, LANES), jnp.int32),
        ],
        scratch_shapes=[pltpu.VMEM((N_EXPERTS, LANES), F32)],
        compiler_params=pltpu.CompilerParams(
            dimension_semantics=("arbitrary",), vmem_limit_bytes=VMEM_LIMIT),
        name="post_mixer",
    )(yp, ys, xp2, xs_t, wout_bf, g2, wr_t, br_col)


def _row_tile(ref, row):
    return ref.at[pl.ds(pl.multiple_of(row * ROW_SUB, ROW_SUB), ROW_SUB)]


def _dispatch_kernel(pos_ref, zf_ref, x_ref, xs_hbm, zbuf, sem, zsem):
    i = pl.program_id(0)
    tc = x_ref.shape[0] // ROW_SUB
    zrows = zbuf.shape[0]
    n_tiles = xs_hbm.shape[0] // zrows

    def zero_copy(t):
        return pltpu.make_async_copy(zbuf, xs_hbm.at[pl.ds(pl.multiple_of(t * zrows, zrows), zrows)], zsem)

    @pl.when(i == 0)
    def _():
        zbuf[...] = jnp.zeros(zbuf.shape, zbuf.dtype)

        def start(t, carry):
            @pl.when(zf_ref[t] != 0)
            def _():
                zero_copy(t).start()
            return carry
        lax.fori_loop(0, n_tiles, start, 0)

        def wait(t, carry):
            @pl.when(zf_ref[t] != 0)
            def _():
                zero_copy(t).wait()
            return carry
        lax.fori_loop(0, n_tiles, wait, 0)

    base = i * (TOP_K * tc)

    def body(g, carry):
        r0 = g * ISSUE_UNROLL
        for u in range(ISSUE_UNROLL):
            for k in range(TOP_K):
                p = pos_ref[base + k * tc + r0 + u]
                pltpu.make_async_copy(_row_tile(x_ref, r0 + u), _row_tile(xs_hbm, p), sem).start(
                    priority=(u * TOP_K + k) % 2)
        return carry
    lax.fori_loop(0, tc // ISSUE_UNROLL, body, 0)
    for k in range(TOP_K):
        pltpu.make_async_copy(x_ref, xs_hbm.at[pl.ds(0, tc * ROW_SUB)], sem).wait()


def _dispatch(pos_tiles, zero_flags, xn_tiles, p_max):
    n = xn_tiles.shape[0] // ROW_SUB
    tc = TC_DISP
    return pl.pallas_call(
        _dispatch_kernel,
        grid_spec=pltpu.PrefetchScalarGridSpec(
            num_scalar_prefetch=2,
            grid=(n // tc,),
            in_specs=[pl.BlockSpec((tc * ROW_SUB, LANES), lambda i, p, z: (i, 0))],
            out_specs=pl.BlockSpec(memory_space=pl.ANY),
            scratch_shapes=[pltpu.VMEM((TM_MOE * ROW_SUB, LANES), jnp.uint32), pltpu.SemaphoreType.DMA(()),
                            pltpu.SemaphoreType.DMA(())],
        ),
        out_shape=jax.ShapeDtypeStruct((p_max * ROW_SUB, LANES), jnp.uint32),
        compiler_params=pltpu.CompilerParams(
            dimension_semantics=("arbitrary",), vmem_limit_bytes=VMEM_LIMIT),
        name="dispatch",
    )(pos_tiles, zero_flags, xn_tiles)


_FLAG_VALID = 1
_FLAG_NEW_WEIGHTS = 2
WEIGHT_DMA_PRIORITY = 1


def _advance_weights(te_ref, ne_ref, fl_ref, slot_ref, copies):
    j = pl.program_id(0)
    t = pl.program_id(1)
    n_j = pl.num_programs(0)

    @pl.when((fl_ref[t] & _FLAG_NEW_WEIGHTS) != 0)
    def _():
        first = (j == 0) & (t == 0)

        @pl.when(first)
        def _():
            slot_ref[0] = 0
            for c in copies(te_ref[t], j, 0):
                c.start(priority=WEIGHT_DMA_PRIORITY)

        @pl.when(jnp.logical_not(first))
        def _():
            slot_ref[0] = 1 - slot_ref[0]

        s = slot_ref[0]
        for c in copies(te_ref[t], j, s):
            c.wait()
        ne = ne_ref[t]

        @pl.when(ne >= 0)
        def _():
            for c in copies(ne, j, 1 - s):
                c.start(priority=WEIGHT_DMA_PRIORITY)

        @pl.when((ne < 0) & (j + 1 < n_j))
        def _():
            for c in copies(te_ref[0], j + 1, 1 - s):
                c.start(priority=WEIGHT_DMA_PRIORITY)


def _bf16_dot(x_bf, w_f32):
    return lax.dot_general(x_bf, w_f32, (((1,), (0,)), ((), ())), preferred_element_type=F32)


def _moe_up_kernel(te_ref, tb_ref, fl_ref, ne_ref, x_ref, bg_ref, bu_ref, w_hbm, h_ref, wbuf, sem, slot_ref):
    t = pl.program_id(1)
    flags = fl_ref[t]

    def copies(e, j, slot):
        cg = pl.multiple_of(j * BF_UP, BF_UP)
        cu = pl.multiple_of(D_FF + j * BF_UP, BF_UP)
        return (pltpu.make_async_copy(w_hbm.at[e, :, pl.ds(cg, BF_UP)], wbuf.at[slot, 0], sem.at[slot]),
                pltpu.make_async_copy(w_hbm.at[e, :, pl.ds(cu, BF_UP)], wbuf.at[slot, 1], sem.at[slot]))

    _advance_weights(te_ref, ne_ref, fl_ref, slot_ref, copies)

    @pl.when((flags & _FLAG_VALID) != 0)
    def _():
        s = slot_ref[0]
        halves = [_unpack_hi_lo(c) for c in _load_row_tiles(x_ref, 0, h_ref.shape[0])]
        x = jnp.concatenate([hl[0].astype(BF16) for hl in halves] + [hl[1].astype(BF16) for hl in halves], axis=1)
        gate = _bf16_dot(x, wbuf[s, 0]) + bg_ref[...]
        up = _bf16_dot(x, wbuf[s, 1]) + bu_ref[...]
        gate = jnp.minimum(gate, SWIGLU_LIMIT)
        up = jnp.clip(up, -SWIGLU_LIMIT, SWIGLU_LIMIT)
        act = gate * _sigmoid(SWIGLU_ALPHA * gate) * (up + 1.0)
        h_ref[...] = act.astype(BF16)

    @pl.when((flags & _FLAG_VALID) == 0)
    def _():
        h_ref[...] = jnp.zeros(h_ref.shape, BF16)


def _moe_up(te, tb, fl, ne, xs, w_gate_up, b_gate_up3):
    p_max = xs.shape[0] // ROW_SUB
    d = D_MODEL
    tm = TM_MOE
    n_tiles = p_max // tm
    n_j = D_FF // BF_UP
    return pl.pallas_call(
        _moe_up_kernel,
        grid_spec=pltpu.PrefetchScalarGridSpec(
            num_scalar_prefetch=4,
            grid=(n_j, n_tiles),
            in_specs=[
                pl.BlockSpec((tm * ROW_SUB, LANES), lambda j, t, te, tb, fl, ne: (tb[t], 0)),
                pl.BlockSpec((None, 1, BF_UP), lambda j, t, te, tb, fl, ne: (te[t], 0, j)),
                pl.BlockSpec((None, 1, BF_UP), lambda j, t, te, tb, fl, ne: (te[t], 0, n_j + j)),
                pl.BlockSpec(memory_space=pl.ANY),
            ],
            out_specs=pl.BlockSpec((tm, BF_UP), lambda j, t, te, tb, fl, ne: (t, j)),
            scratch_shapes=[pltpu.VMEM((2, 2, d, BF_UP), F32), pltpu.SemaphoreType.DMA((2,)),
                            pltpu.SMEM((1,), jnp.int32)],
        ),
        out_shape=jax.ShapeDtypeStruct((p_max, D_FF), BF16),
        compiler_params=pltpu.CompilerParams(
            dimension_semantics=("arbitrary", "arbitrary"), vmem_limit_bytes=VMEM_LIMIT),
        name="moe_up",
    )(te, tb, fl, ne, xs, b_gate_up3, b_gate_up3, w_gate_up)


def _moe_down_kernel(te_ref, tb_ref, fl_ref, ne_ref, h_ref, bd_ref, w_hbm, y_ref, wbuf, sem, slot_ref):
    t = pl.program_id(1)
    flags = fl_ref[t]

    def copies(e, j, slot):
        c0 = pl.multiple_of(j * BN_DOWN, BN_DOWN)
        return (pltpu.make_async_copy(w_hbm.at[e, :, pl.ds(c0, BN_DOWN)], wbuf.at[slot], sem.at[slot]),)

    _advance_weights(te_ref, ne_ref, fl_ref, slot_ref, copies)

    @pl.when((flags & _FLAG_VALID) != 0)
    def _():
        y = _bf16_dot(h_ref[...], wbuf[slot_ref[0]]) + bd_ref[...]
        _store_row_tiles(y_ref, _pack_bf16_pairs(y))

    @pl.when((flags & _FLAG_VALID) == 0)
    def _():
        y_ref[...] = jnp.zeros(y_ref.shape, y_ref.dtype)


def _moe_down(te, tb, fl, ne, h, w_down, b_down3):
    p_max, f = h.shape
    tm = TM_MOE
    n_tiles = p_max // tm
    n_j = D_MODEL // BN_DOWN
    return pl.pallas_call(
        _moe_down_kernel,
        grid_spec=pltpu.PrefetchScalarGridSpec(
            num_scalar_prefetch=4,
            grid=(n_j, n_tiles),
            in_specs=[
                pl.BlockSpec((tm, f), lambda j, t, te, tb, fl, ne: (tb[t], 0)),
                pl.BlockSpec((None, 1, BN_DOWN), lambda j, t, te, tb, fl, ne: (te[t], 0, j)),
                pl.BlockSpec(memory_space=pl.ANY),
            ],
            out_specs=pl.BlockSpec((tm * ROW_SUB, LANES), lambda j, t, te, tb, fl, ne: (t, 0)),
            scratch_shapes=[pltpu.VMEM((2, f, BN_DOWN), F32), pltpu.SemaphoreType.DMA((2,)),
                            pltpu.SMEM((1,), jnp.int32)],
        ),
        out_shape=jax.ShapeDtypeStruct((p_max * ROW_SUB, LANES), jnp.uint32),
        compiler_params=pltpu.CompilerParams(
            dimension_semantics=("arbitrary", "arbitrary"), vmem_limit_bytes=VMEM_LIMIT),
        name="moe_down",
    )(te, tb, fl, ne, h, b_down3, w_down)


def _combine_kernel(pos_ref, x1_ref, gate_ref, gf_ref, y_hbm, op_ref, os_ref, buf, sem, *, n_prompt_tiles):
    i = pl.program_id(0)
    n_i = pl.num_programs(0)
    tc = x1_ref.shape[0]
    rows = TOP_K * tc

    def issue(tile, slot):
        base = tile * rows

        def body(g, carry):
            r0 = g * ISSUE_UNROLL
            for u in range(ISSUE_UNROLL):
                p = pos_ref[base + r0 + u]
                pltpu.make_async_copy(_row_tile(y_hbm, p), _row_tile(buf.at[slot], r0 + u),
                                      sem.at[slot]).start(priority=u % 2)
            return carry
        lax.fori_loop(0, rows // ISSUE_UNROLL, body, 0)

    @pl.when(i == 0)
    def _():
        issue(0, 0)

    @pl.when(i + 1 < n_i)
    def _():
        issue(i + 1, (i + 1) % 2)

    slot = i % 2
    pltpu.make_async_copy(y_hbm.at[pl.ds(0, rows * ROW_SUB)], buf.at[slot], sem.at[slot]).wait()
    his = [None] * ROW_SUB
    los = [None] * ROW_SUB
    for k in range(TOP_K):
        g = gate_ref[:, k:k + 1]
        for s, chunk in enumerate(_load_row_tiles(buf.at[slot], k * tc, tc)):
            hi, lo = _unpack_hi_lo(chunk)
            his[s] = g * hi if k == 0 else his[s] + g * hi
            los[s] = g * lo if k == 0 else los[s] + g * lo
    acc = x1_ref[...] + jnp.concatenate(his + los, axis=1)
    out = _rms_rows(acc, gf_ref[...])

    @pl.when(i < n_prompt_tiles)
    def _():
        op_ref[...] = out

    @pl.when(i >= n_prompt_tiles)
    def _():
        os_ref[...] = out


def _combine(pos_flat, x1, gates_nk, gf, y_sorted, n_prompt):
    n, d = x1.shape
    tc = TC_COMB
    n_tiles = n // tc
    n_pt = n_prompt // tc
    kern = functools.partial(_combine_kernel, n_prompt_tiles=n_pt)
    return pl.pallas_call(
        kern,
        grid_spec=pltpu.PrefetchScalarGridSpec(
            num_scalar_prefetch=1,
            grid=(n_tiles,),
            in_specs=[
                pl.BlockSpec((tc, d), lambda i, p: (i, 0)),
                pl.BlockSpec((tc, TOP_K), lambda i, p: (i, 0)),
                pl.BlockSpec((1, d), lambda i, p: (0, 0)),
                pl.BlockSpec(memory_space=pl.ANY),
            ],
            out_specs=[
                pl.BlockSpec((tc, d), lambda i, p: (jnp.minimum(i, n_pt - 1), 0)),
                pl.BlockSpec((tc, d), lambda i, p: (jnp.maximum(i - n_pt, 0), 0)),
            ],
            scratch_shapes=[pltpu.VMEM((2, TOP_K * tc * ROW_SUB, LANES), jnp.uint32),
                            pltpu.SemaphoreType.DMA((2,))],
        ),
        out_shape=[
            jax.ShapeDtypeStruct((n_prompt, d), F32),
            jax.ShapeDtypeStruct((n - n_prompt, d), F32),
        ],
        compiler_params=pltpu.CompilerParams(
            dimension_semantics=("arbitrary",), vmem_limit_bytes=VMEM_LIMIT),
        name="combine",
    )(pos_flat, x1, gates_nk, gf, y_sorted)


def kernel(x_prompt, x_sample, state_conv_a, state_conv_b, meta_tokens, norm1_g, w_in, conv_a_w, conv_b_w,
           conv_b_b, ln_b_g, ln_b_b, w_out, norm2_g, w_router, b_router, w_gate_up, b_gate_up, w_down,
           b_down, final_norm_g):
    bp, seq, d = x_prompt.shape
    n_seq, n_t, _ = x_sample.shape
    n_prompt = bp * seq
    n_sample = n_seq * n_t
    n = n_prompt + n_sample
    assert norm1_g.shape[0] == 1, "single layer"
    assert seq % T_MIX == 0 and n_prompt % TM_POST == 0 and n_sample == TM_POST
    assert n_prompt % n_sample == 0 and n % TC_COMB == 0 and n_prompt % TC_COMB == 0

    g1 = norm1_g[0][None]
    win_bf = w_in[0].astype(BF16)
    wout_bf = w_out[0].astype(BF16)
    caw, cbw = conv_a_w[0], conv_b_w[0]
    cbb, lng, lnb = conv_b_b[0][None], ln_b_g[0][None], ln_b_b[0][None]

    xs_t = jnp.transpose(x_sample, (1, 0, 2)).reshape(n_sample, d)
    sa_t = jnp.transpose(state_conv_a[0], (1, 0, 2))
    sb_t = jnp.transpose(state_conv_b[0], (1, 0, 2))

    ymix_p, pa, pb = _mixer_prompt(x_prompt, meta_tokens, g1, win_bf, caw, cbw, cbb, lng, lnb)
    ymix_s, na_t, glu_t = _mixer_sample(xs_t, g1, win_bf, sa_t, sb_t, caw, cbw, cbb, lng, lnb, n_seq, n_t)

    x1, xn, idx, gates, rank, cnt = _post_mixer(
        ymix_p, ymix_s, x_prompt.reshape(n_prompt, d), xs_t, wout_bf, norm2_g[0][None],
        jnp.transpose(w_router[0]), b_router[0][:, None])

    tm = TM_MOE
    n_assign = n * TOP_K
    n_tiles = (n_assign + N_EXPERTS * (tm - 1)) // tm
    p_max = n_tiles * tm
    counts = cnt[:, 0]
    tiles_per_e = (counts + tm - 1) // tm
    tile_end = jnp.cumsum(tiles_per_e)
    tile_start = tile_end - tiles_per_e
    n_used = tile_end[-1]
    e_ar = jnp.arange(N_EXPERTS, dtype=jnp.int32)
    start_of = jnp.sum(jnp.where(idx[None] == e_ar[:, None, None], tile_start[:, None, None], 0), axis=0)
    pos = (start_of * tm + rank).astype(jnp.int32)
    tid = jnp.arange(n_tiles, dtype=jnp.int32)
    tb = jnp.maximum(jnp.minimum(tid, n_used - 1), 0).astype(jnp.int32)
    expert_of = lambda tile: jnp.minimum(
        jnp.sum((tile_end[None, :] <= tile[:, None]).astype(jnp.int32), axis=1), N_EXPERTS - 1).astype(jnp.int32)
    te = expert_of(tb)
    valid = tid < n_used
    new_w = valid & ((tid == 0) | (te != jnp.roll(te, 1)))
    fl = (valid.astype(jnp.int32) * _FLAG_VALID + new_w.astype(jnp.int32) * _FLAG_NEW_WEIGHTS)
    next_tile = jnp.sum(jnp.where(te[:, None] == e_ar[None, :], tile_end[None, :], 0), axis=1)
    ne = jnp.where(next_tile < n_used, expert_of(next_tile), -1).astype(jnp.int32)
    partial = (counts % tm) != 0
    zero_flags = ((tid >= n_used) | jnp.any((tid[:, None] == (tile_end - 1)[None, :]) & partial[None, :],
                                            axis=1)).astype(jnp.int32)

    by_tile = lambda tc: pos.reshape(TOP_K, n // tc, tc).transpose(1, 0, 2).reshape(-1)
    xs_sorted = _dispatch(by_tile(TC_DISP), zero_flags, xn, p_max)
    h = _moe_up(te, tb, fl, ne, xs_sorted, w_gate_up[0], b_gate_up[0][:, None, :])
    y_sorted = _moe_down(te, tb, fl, ne, h, w_down[0], b_down[0][:, None, :])

    yp, ys_t = _combine(by_tile(TC_COMB), x1, jnp.transpose(gates), final_norm_g[None], y_sorted, n_prompt)

    y_prompt = yp.reshape(bp, seq, d)
    y_sample = jnp.transpose(ys_t.reshape(n_t, n_seq, d), (1, 0, 2))
    new_a_prompt = pa[None]
    new_b_prompt = pb[None]
    new_a_sample = jnp.transpose(na_t, (1, 0, 2))[None]
    glu_s = jnp.transpose(glu_t, (1, 0, 2))
    new_b_sample = jnp.concatenate([state_conv_b[0][:, n_t:], glu_s], axis=1)[None]
    return (y_prompt, y_sample, new_a_prompt, new_b_prompt, new_a_sample, new_b_sample)
```

```python
import functools

import jax
import jax.numpy as jnp
from jax import lax
from jax.experimental import pallas as pl
from jax.experimental.pallas import tpu as pltpu

F32 = jnp.float32
BF16 = jnp.bfloat16

D_MODEL = 2048
N_META = 16
C_A = 1024
C_B = 1024
K_A = 3
K_B = 31
N_EXPERTS = 32
TOP_K = 4
D_FF = 2048
SWIGLU_LIMIT = 7.0
SWIGLU_ALPHA = 1.702
RMS_EPS = 1e-5
LN_EPS = 1e-5

LANES = 128
N_CHUNK = C_B // LANES
HALO_B = 32
HALO_A = 8
T_MIX = 256
TM_POST = 512
TM_MOE = 512
SUB_MOE = 256
BF_UP = 1024
BN_DOWN = 2048
ROW_SUB = D_MODEL // 2 // LANES
TC_COMB = 128
TC_DISP = 512
ISSUE_UNROLL = 8
VMEM_LIMIT = 56 * 1024 * 1024


def _sigmoid(x):
    return jax.nn.sigmoid(x)


def _rms_rows(x, g):
    ms = jnp.mean(x * x, axis=-1, keepdims=True)
    return (x * lax.rsqrt(ms + RMS_EPS)) * g


def _pack_bf16_pairs(x):
    c = x.shape[1] // 2
    hi = lax.bitcast_convert_type(x[:, :c].astype(BF16).astype(F32), jnp.uint32)
    lo = lax.bitcast_convert_type(x[:, c:].astype(BF16).astype(F32), jnp.uint32)
    return hi | (lo >> 16)


def _unpack_hi_lo(p):
    hi = lax.bitcast_convert_type(p & jnp.uint32(0xFFFF0000), F32)
    lo = lax.bitcast_convert_type(p << 16, F32)
    return hi, lo


def _store_row_tiles(ref, first_row, packed):
    r = packed.shape[0]
    for s in range(ROW_SUB):
        ref[pl.ds(first_row * ROW_SUB + s, r, stride=ROW_SUB), :] = packed[:, s * LANES:(s + 1) * LANES]


def _load_row_tiles(ref, first_row, r):
    return [ref[pl.ds(first_row * ROW_SUB + s, r, stride=ROW_SUB), :] for s in range(ROW_SUB)]


def _mixer_prompt_kernel(x_ref, meta_ref, g1_ref, win_ref, caw_ref, cbw_ref, cbb_ref, lng_ref, lnb_ref,
                         y_ref, pa_ref, pb_ref,
                         u_ref, gb_ref, cv_ref, mgb_ref, mcv_ref, bc_ref):
    b = pl.program_id(0)
    i = pl.program_id(1)
    n_i = pl.num_programs(1)
    t = T_MIX

    def in_proj(rows):
        h = _rms_rows(rows, g1_ref[...]).astype(BF16)
        return jnp.dot(h, win_ref[...], preferred_element_type=F32)

    @pl.when((b == 0) & (i == 0))
    def _():
        um = in_proj(meta_ref[...])
        cvm = um[:, C_A:2 * C_A] * um[:, 2 * C_A:3 * C_A]
        glum = um[:, 3 * C_A:3 * C_A + C_B] * _sigmoid(um[:, 3 * C_A + C_B:])
        for c in range(N_CHUNK):
            sl = slice(c * LANES, (c + 1) * LANES)
            mgb_ref[c, 0:HALO_B - N_META, :] = jnp.zeros((HALO_B - N_META, LANES), F32)
            mgb_ref[c, HALO_B - N_META:HALO_B, :] = glum[:, sl]
            mcv_ref[c] = cvm[N_META - HALO_A:, sl]

    @pl.when(i == 0)
    def _():
        gb_ref[:, 0:HALO_B, :] = mgb_ref[...]
        cv_ref[:, 0:HALO_A, :] = mcv_ref[...]

    u_ref[...] = in_proj(x_ref[...])

    for c in range(N_CHUNK):
        lo = c * LANES
        bg = u_ref[:, lo:lo + LANES]
        cg = u_ref[:, C_A + lo:C_A + lo + LANES]
        v = u_ref[:, 2 * C_A + lo:2 * C_A + lo + LANES]
        ga = u_ref[:, 3 * C_A + lo:3 * C_A + lo + LANES]
        gbv = u_ref[:, 3 * C_A + C_B + lo:3 * C_A + C_B + lo + LANES]
        cv_ref[c, HALO_A:, :] = cg * v
        gb_ref[c, HALO_B:, :] = ga * _sigmoid(gbv)
        acc = cv_ref[c, HALO_A - 2:HALO_A - 2 + t, :] * caw_ref[0:1, lo:lo + LANES]
        for k in range(1, K_A):
            acc = acc + cv_ref[c, HALO_A - 2 + k:HALO_A - 2 + k + t, :] * caw_ref[k:k + 1, lo:lo + LANES]
        y_ref[:, lo:lo + LANES] = (bg * acc).astype(BF16)
        off = HALO_B - (K_B - 1)
        accb = gb_ref[c, off:off + t, :] * cbw_ref[0:1, lo:lo + LANES]
        for k in range(1, K_B):
            accb = accb + gb_ref[c, off + k:off + k + t, :] * cbw_ref[k:k + 1, lo:lo + LANES]
        bc_ref[:, lo:lo + LANES] = accb + cbb_ref[:, lo:lo + LANES]

    rb = 64
    for r in range(t // rb):
        xb = bc_ref[r * rb:(r + 1) * rb, :]
        mu = jnp.mean(xb, axis=-1, keepdims=True)
        var = jnp.mean(jnp.square(xb - mu), axis=-1, keepdims=True)
        bn = (xb - mu) * lax.rsqrt(var + LN_EPS) * lng_ref[...] + lnb_ref[...]
        y_ref[r * rb:(r + 1) * rb, C_A:] = (bn * _sigmoid(bn)).astype(BF16)

    @pl.when(i == n_i - 1)
    def _():
        for c in range(N_CHUNK):
            sl = slice(c * LANES, (c + 1) * LANES)
            pa_ref[:, sl] = cv_ref[c, HALO_A + t - (K_A - 1):HALO_A + t, :]
            pb_ref[:, sl] = gb_ref[c, HALO_B + t - (K_B - 1):HALO_B + t, :]

    gb_ref[:, 0:HALO_B, :] = gb_ref[:, t:t + HALO_B, :]
    cv_ref[:, 0:HALO_A, :] = cv_ref[:, t:t + HALO_A, :]


def _mixer_prompt(x_prompt, meta, g1, win_bf, caw, cbw, cbb, lng, lnb):
    bp, seq, d = x_prompt.shape
    n_i = seq // T_MIX
    full = lambda shape: pl.BlockSpec(shape, lambda b, i: (0,) * len(shape))
    return pl.pallas_call(
        _mixer_prompt_kernel,
        grid=(bp, n_i),
        in_specs=[
            pl.BlockSpec((None, T_MIX, d), lambda b, i: (b, i, 0)),
            full((N_META, d)),
            full((1, d)),
            pl.BlockSpec(win_bf.shape, lambda b, i: (0, 0), pipeline_mode=pl.Buffered(1)),
            full((K_A, C_A)),
            full((K_B, C_B)),
            full((1, C_B)),
            full((1, C_B)),
            full((1, C_B)),
        ],
        out_specs=[
            pl.BlockSpec((T_MIX, d), lambda b, i: (b * n_i + i, 0)),
            pl.BlockSpec((None, K_A - 1, C_A), lambda b, i: (b, 0, 0)),
            pl.BlockSpec((None, K_B - 1, C_B), lambda b, i: (b, 0, 0)),
        ],
        out_shape=[
            jax.ShapeDtypeStruct((bp * seq, d), BF16),
            jax.ShapeDtypeStruct((bp, K_A - 1, C_A), F32),
            jax.ShapeDtypeStruct((bp, K_B - 1, C_B), F32),
        ],
        scratch_shapes=[
            pltpu.VMEM((T_MIX, win_bf.shape[1]), F32),
            pltpu.VMEM((N_CHUNK, T_MIX + HALO_B, LANES), F32),
            pltpu.VMEM((N_CHUNK, T_MIX + HALO_A, LANES), F32),
            pltpu.VMEM((N_CHUNK, HALO_B, LANES), F32),
            pltpu.VMEM((N_CHUNK, HALO_A, LANES), F32),
            pltpu.VMEM((T_MIX, C_B), F32),
        ],
        compiler_params=pltpu.CompilerParams(
            dimension_semantics=("arbitrary", "arbitrary"), vmem_limit_bytes=VMEM_LIMIT),
        name="mixer_prompt",
    )(x_prompt, meta, g1, win_bf, caw, cbw, cbb, lng, lnb)


def _mixer_sample_kernel(x_ref, g1_ref, wbg_ref, wcg_ref, wv_ref, wga_ref, wgb_ref, sa_ref, sb_ref,
                         caw_ref, cbw_ref, cbb_ref, lng_ref, lnb_ref,
                         y_ref, na_ref, glu_ref,
                         h_ref, ya_ref, bc_ref, *, n_seq, n_t):
    c = pl.program_id(0)

    @pl.when(c == 0)
    def _():
        h_ref[...] = _rms_rows(x_ref[...], g1_ref[...]).astype(BF16)

    @pl.when(c < N_CHUNK)
    def _():
        h = h_ref[...]
        proj = lambda w_ref: jnp.dot(h, w_ref[...], preferred_element_type=F32)
        bg, cg, v, ga, gbv = proj(wbg_ref), proj(wcg_ref), proj(wv_ref), proj(wga_ref), proj(wgb_ref)
        cv = cg * v
        glu = ga * _sigmoid(gbv)
        row = lambda a, tt: a[tt * n_seq:(tt + 1) * n_seq, :]
        xa = [sa_ref[j] for j in range(K_A - 1)] + [row(cv, tt) for tt in range(n_t)]
        xb = [sb_ref[j] for j in range(K_B - 1)] + [row(glu, tt) for tt in range(n_t)]
        for tt in range(n_t):
            acc = xa[tt] * caw_ref[0:1, :]
            for k in range(1, K_A):
                acc = acc + xa[tt + k] * caw_ref[k:k + 1, :]
            ya_ref[c, tt * n_seq:(tt + 1) * n_seq, :] = row(bg, tt) * acc
            accb = xb[tt] * cbw_ref[0:1, :]
            for k in range(1, K_B):
                accb = accb + xb[tt + k] * cbw_ref[k:k + 1, :]
            bc_ref[c, tt * n_seq:(tt + 1) * n_seq, :] = accb + cbb_ref[...]
            glu_ref[tt] = row(glu, tt)
        for j in range(K_A - 1):
            na_ref[j] = row(cv, n_t - (K_A - 1) + j)

    @pl.when(c == N_CHUNK)
    def _():
        xb = jnp.concatenate([bc_ref[cc] for cc in range(N_CHUNK)], axis=1)
        mu = jnp.mean(xb, axis=-1, keepdims=True)
        var = jnp.mean(jnp.square(xb - mu), axis=-1, keepdims=True)
        bn = (xb - mu) * lax.rsqrt(var + LN_EPS) * lng_ref[...] + lnb_ref[...]
        for cc in range(N_CHUNK):
            y_ref[:, cc * LANES:(cc + 1) * LANES] = ya_ref[cc].astype(BF16)
        y_ref[:, C_A:] = (bn * _sigmoid(bn)).astype(BF16)


def _mixer_sample(xs_t, g1, win_bf, sa_t, sb_t, caw, cbw, cbb, lng, lnb, n_seq, n_t):
    rows, d = xs_t.shape
    cc = lambda c: jnp.minimum(c, N_CHUNK - 1)
    wspec = lambda g: pl.BlockSpec((d, LANES), lambda c, g=g: (0, g * N_CHUNK + cc(c)))
    full = lambda shape: pl.BlockSpec(shape, lambda c: (0,) * len(shape))
    kern = functools.partial(_mixer_sample_kernel, n_seq=n_seq, n_t=n_t)
    return pl.pallas_call(
        kern,
        grid=(N_CHUNK + 1,),
        in_specs=[
            full((rows, d)),
            full((1, d)),
            wspec(0), wspec(1), wspec(2), wspec(3), wspec(4),
            pl.BlockSpec((K_A - 1, n_seq, LANES), lambda c: (0, 0, cc(c))),
            pl.BlockSpec((K_B - 1, n_seq, LANES), lambda c: (0, 0, cc(c))),
            pl.BlockSpec((K_A, LANES), lambda c: (0, cc(c))),
            pl.BlockSpec((K_B, LANES), lambda c: (0, cc(c))),
            pl.BlockSpec((1, LANES), lambda c: (0, cc(c))),
            full((1, C_B)),
            full((1, C_B)),
        ],
        out_specs=[
            full((rows, d)),
            pl.BlockSpec((K_A - 1, n_seq, LANES), lambda c: (0, 0, cc(c))),
            pl.BlockSpec((n_t, n_seq, LANES), lambda c: (0, 0, cc(c))),
        ],
        out_shape=[
            jax.ShapeDtypeStruct((rows, d), BF16),
            jax.ShapeDtypeStruct((K_A - 1, n_seq, C_A), F32),
            jax.ShapeDtypeStruct((n_t, n_seq, C_B), F32),
        ],
        scratch_shapes=[
            pltpu.VMEM((rows, d), BF16),
            pltpu.VMEM((N_CHUNK, rows, LANES), F32),
            pltpu.VMEM((N_CHUNK, rows, LANES), F32),
        ],
        compiler_params=pltpu.CompilerParams(
            dimension_semantics=("arbitrary",), vmem_limit_bytes=VMEM_LIMIT),
        name="mixer_sample",
    )(xs_t, g1, win_bf, win_bf, win_bf, win_bf, win_bf, sa_t, sb_t, caw, cbw, cbb, lng, lnb)


def _post_mixer_kernel(yp_ref, ys_ref, xp_ref, xs_ref, wout_ref, g2_ref, wr_ref, br_ref,
                       x1_ref, xn_ref, idx_ref, gate_ref, rank_ref, cnt_ref,
                       run_ref, *, n_prompt_tiles):
    i = pl.program_id(0)
    tm = yp_ref.shape[0]

    @pl.when(i == 0)
    def _():
        run_ref[...] = jnp.zeros(run_ref.shape, F32)

    is_prompt = i < n_prompt_tiles
    x = jnp.where(is_prompt, xp_ref[...], xs_ref[...])
    y = jnp.where(is_prompt, yp_ref[...], ys_ref[...])
    x1 = x + jnp.dot(y, wout_ref[...], preferred_element_type=F32)
    x1_ref[...] = x1
    xn = _rms_rows(x1, g2_ref[...])
    _store_row_tiles(xn_ref, 0, _pack_bf16_pairs(xn))

    logits = lax.dot_general(wr_ref[...], xn, (((1,), (1,)), ((), ())),
                             precision=lax.Precision.HIGHEST, preferred_element_type=F32) + br_ref[...]
    eidx = lax.broadcasted_iota(jnp.int32, logits.shape, 0)
    work = logits
    vals, sels, hots = [], [], []
    for _ in range(TOP_K):
        m = jnp.max(work, axis=0, keepdims=True)
        sel = jnp.min(jnp.where(work == m, eidx, N_EXPERTS), axis=0, keepdims=True)
        hot = eidx == sel
        vals.append(m)
        sels.append(sel)
        hots.append(hot)
        work = jnp.where(hot, -jnp.inf, work)
    exps = [jnp.exp(v - vals[0]) for v in vals]
    denom = exps[0] + exps[1] + exps[2] + exps[3]
    for k in range(TOP_K):
        idx_ref[k:k + 1, :] = sels[k]
        gate_ref[k:k + 1, :] = exps[k] / denom

    chosen = (hots[0] | hots[1] | hots[2] | hots[3])
    chosen_bf = chosen.astype(F32).astype(BF16)
    s_io = lax.broadcasted_iota(jnp.int32, (tm, tm), 0)
    t_io = lax.broadcasted_iota(jnp.int32, (tm, tm), 1)
    upper = (s_io < t_io).astype(F32).astype(BF16)
    before = jnp.dot(chosen_bf, upper, preferred_element_type=F32) + run_ref[:, 0:1]
    for k in range(TOP_K):
        r = jnp.sum(jnp.where(hots[k], before, 0.0), axis=0, keepdims=True)
        rank_ref[k:k + 1, :] = r.astype(jnp.int32)
    run_ref[...] = run_ref[...] + jnp.sum(chosen.astype(F32), axis=1, keepdims=True)
    cnt_ref[...] = run_ref[...].astype(jnp.int32)


def _post_mixer(yp, ys, xp2, xs_t, wout_bf, g2, wr_t, br_col):
    d = yp.shape[1]
    n = yp.shape[0] + ys.shape[0]
    tm = TM_POST
    n_tiles = n // tm
    n_pt = xp2.shape[0] // tm
    kern = functools.partial(_post_mixer_kernel, n_prompt_tiles=n_pt)
    full = lambda shape: pl.BlockSpec(shape, lambda i: (0,) * len(shape))
    return pl.pallas_call(
        kern,
        grid=(n_tiles,),
        in_specs=[
            pl.BlockSpec((tm, d), lambda i: (jnp.minimum(i, n_pt - 1), 0)),
            pl.BlockSpec((tm, d), lambda i: (jnp.maximum(i - n_pt, 0), 0)),
            pl.BlockSpec((tm, d), lambda i: (jnp.minimum(i, n_pt - 1), 0)),
            pl.BlockSpec((tm, d), lambda i: (jnp.maximum(i - n_pt, 0), 0)),
            pl.BlockSpec(wout_bf.shape, lambda i: (0, 0), pipeline_mode=pl.Buffered(1)),
            full((1, d)),
            full((N_EXPERTS, d)),
            full((N_EXPERTS, 1)),
        ],
        out_specs=[
            pl.BlockSpec((tm, d), lambda i: (i, 0)),
            pl.BlockSpec((tm * ROW_SUB, LANES), lambda i: (i, 0)),
            pl.BlockSpec((TOP_K, tm), lambda i: (0, i)),
            pl.BlockSpec((TOP_K, tm), lambda i: (0, i)),
            pl.BlockSpec((TOP_K, tm), lambda i: (0, i)),
            full((N_EXPERTS, LANES)),
        ],
        out_shape=[
            jax.ShapeDtypeStruct((n, d), F32),
            jax.ShapeDtypeStruct((n * ROW_SUB, LANES), jnp.uint32),
            jax.ShapeDtypeStruct((TOP_K, n), jnp.int32),
            jax.ShapeDtypeStruct((TOP_K, n), F32),
            jax.ShapeDtypeStruct((TOP_K, n), jnp.int32),
            jax.ShapeDtypeStruct((N_EXPERTS, LANES), jnp.int32),
        ],
        scratch_shapes=[pltpu.VMEM((N_EXPERTS, LANES), F32)],
        compiler_params=pltpu.CompilerParams(
            dimension_semantics=("arbitrary",), vmem_limit_bytes=VMEM_LIMIT),
        name="post_mixer",
    )(yp, ys, xp2, xs_t, wout_bf, g2, wr_t, br_col)


def _row_tile(ref, row):
    return ref.at[pl.ds(pl.multiple_of(row * ROW_SUB, ROW_SUB), ROW_SUB)]


def _dispatch_kernel(pos_ref, zf_ref, x_ref, xs_hbm, zbuf, sem, zsem):
    i = pl.program_id(0)
    tc = x_ref.shape[0] // ROW_SUB
    zrows = zbuf.shape[0]
    n_tiles = xs_hbm.shape[0] // zrows

    def zero_copy(t):
        return pltpu.make_async_copy(zbuf, xs_hbm.at[pl.ds(pl.multiple_of(t * zrows, zrows), zrows)], zsem)

    @pl.when(i == 0)
    def _():
        zbuf[...] = jnp.zeros(zbuf.shape, zbuf.dtype)

        def start(t, carry):
            @pl.when(zf_ref[t] != 0)
            def _():
                zero_copy(t).start()
            return carry
        lax.fori_loop(0, n_tiles, start, 0)

        def wait(t, carry):
            @pl.when(zf_ref[t] != 0)
            def _():
                zero_copy(t).wait()
            return carry
        lax.fori_loop(0, n_tiles, wait, 0)

    base = i * (TOP_K * tc)

    def body(g, carry):
        r0 = g * ISSUE_UNROLL
        for u in range(ISSUE_UNROLL):
            for k in range(TOP_K):
                p = pos_ref[base + k * tc + r0 + u]
                pltpu.make_async_copy(_row_tile(x_ref, r0 + u), _row_tile(xs_hbm, p), sem).start(
                    priority=(u * TOP_K + k) % 2)
        return carry
    lax.fori_loop(0, tc // ISSUE_UNROLL, body, 0)
    for k in range(TOP_K):
        pltpu.make_async_copy(x_ref, xs_hbm.at[pl.ds(0, tc * ROW_SUB)], sem).wait()


def _dispatch(pos_tiles, zero_flags, xn_tiles, p_max):
    n = xn_tiles.shape[0] // ROW_SUB
    tc = TC_DISP
    return pl.pallas_call(
        _dispatch_kernel,
        grid_spec=pltpu.PrefetchScalarGridSpec(
            num_scalar_prefetch=2,
            grid=(n // tc,),
            in_specs=[pl.BlockSpec((tc * ROW_SUB, LANES), lambda i, p, z: (i, 0))],
            out_specs=pl.BlockSpec(memory_space=pl.ANY),
            scratch_shapes=[pltpu.VMEM((TM_MOE * ROW_SUB, LANES), jnp.uint32), pltpu.SemaphoreType.DMA(()),
                            pltpu.SemaphoreType.DMA(())],
        ),
        out_shape=jax.ShapeDtypeStruct((p_max * ROW_SUB, LANES), jnp.uint32),
        compiler_params=pltpu.CompilerParams(
            dimension_semantics=("arbitrary",), vmem_limit_bytes=VMEM_LIMIT),
        name="dispatch",
    )(pos_tiles, zero_flags, xn_tiles)


_FLAG_NEW_WEIGHTS = 1
WEIGHT_DMA_PRIORITY = 1


def _advance_weights(te_ref, ne_ref, fl_ref, slot_ref, copies):
    j = pl.program_id(0)
    t = pl.program_id(1)
    n_j = pl.num_programs(0)

    @pl.when((fl_ref[t] & _FLAG_NEW_WEIGHTS) != 0)
    def _():
        first = (j == 0) & (t == 0)

        @pl.when(first)
        def _():
            slot_ref[0] = 0
            for c in copies(te_ref[t], j, 0):
                c.start(priority=WEIGHT_DMA_PRIORITY)

        @pl.when(jnp.logical_not(first))
        def _():
            slot_ref[0] = 1 - slot_ref[0]

        s = slot_ref[0]
        for c in copies(te_ref[t], j, s):
            c.wait()
        ne = ne_ref[t]

        @pl.when(ne >= 0)
        def _():
            for c in copies(ne, j, 1 - s):
                c.start(priority=WEIGHT_DMA_PRIORITY)

        @pl.when((ne < 0) & (j + 1 < n_j))
        def _():
            for c in copies(te_ref[0], j + 1, 1 - s):
                c.start(priority=WEIGHT_DMA_PRIORITY)


def _bf16_dot(x_bf, w_f32):
    return lax.dot_general(x_bf, w_f32, (((1,), (0,)), ((), ())), preferred_element_type=F32)


def _moe_up_kernel(te_ref, tb_ref, fl_ref, ne_ref, nv_ref, x_ref, bg_ref, bu_ref, w_hbm, h_ref, wbuf, sem,
                   slot_ref):
    t = pl.program_id(1)
    n_valid = nv_ref[t]

    def copies(e, j, slot):
        cg = pl.multiple_of(j * BF_UP, BF_UP)
        cu = pl.multiple_of(D_FF + j * BF_UP, BF_UP)
        return (pltpu.make_async_copy(w_hbm.at[e, :, pl.ds(cg, BF_UP)], wbuf.at[slot, 0], sem.at[slot]),
                pltpu.make_async_copy(w_hbm.at[e, :, pl.ds(cu, BF_UP)], wbuf.at[slot, 1], sem.at[slot]))

    _advance_weights(te_ref, ne_ref, fl_ref, slot_ref, copies)

    for r0 in range(0, TM_MOE, SUB_MOE):
        rows = slice(r0, r0 + SUB_MOE)

        @pl.when(n_valid > r0)
        def _():
            s = slot_ref[0]
            halves = [_unpack_hi_lo(c) for c in _load_row_tiles(x_ref, r0, SUB_MOE)]
            x = jnp.concatenate([hl[0].astype(BF16) for hl in halves] + [hl[1].astype(BF16) for hl in halves],
                                axis=1)
            gate = _bf16_dot(x, wbuf[s, 0]) + bg_ref[...]
            up = _bf16_dot(x, wbuf[s, 1]) + bu_ref[...]
            gate = jnp.minimum(gate, SWIGLU_LIMIT)
            up = jnp.clip(up, -SWIGLU_LIMIT, SWIGLU_LIMIT)
            act = gate * _sigmoid(SWIGLU_ALPHA * gate) * (up + 1.0)
            h_ref[rows, :] = act.astype(BF16)

        @pl.when(n_valid <= r0)
        def _():
            h_ref[rows, :] = jnp.zeros((SUB_MOE, h_ref.shape[1]), BF16)


def _moe_up(te, tb, fl, ne, nv, xs, w_gate_up, b_gate_up3):
    p_max = xs.shape[0] // ROW_SUB
    d = D_MODEL
    tm = TM_MOE
    n_tiles = p_max // tm
    n_j = D_FF // BF_UP
    return pl.pallas_call(
        _moe_up_kernel,
        grid_spec=pltpu.PrefetchScalarGridSpec(
            num_scalar_prefetch=5,
            grid=(n_j, n_tiles),
            in_specs=[
                pl.BlockSpec((tm * ROW_SUB, LANES), lambda j, t, te, tb, fl, ne, nv: (tb[t], 0)),
                pl.BlockSpec((None, 1, BF_UP), lambda j, t, te, tb, fl, ne, nv: (te[t], 0, j)),
                pl.BlockSpec((None, 1, BF_UP), lambda j, t, te, tb, fl, ne, nv: (te[t], 0, n_j + j)),
                pl.BlockSpec(memory_space=pl.ANY),
            ],
            out_specs=pl.BlockSpec((tm, BF_UP), lambda j, t, te, tb, fl, ne, nv: (t, j)),
            scratch_shapes=[pltpu.VMEM((2, 2, d, BF_UP), F32), pltpu.SemaphoreType.DMA((2,)),
                            pltpu.SMEM((1,), jnp.int32)],
        ),
        out_shape=jax.ShapeDtypeStruct((p_max, D_FF), BF16),
        compiler_params=pltpu.CompilerParams(
            dimension_semantics=("arbitrary", "arbitrary"), vmem_limit_bytes=VMEM_LIMIT),
        name="moe_up",
    )(te, tb, fl, ne, nv, xs, b_gate_up3, b_gate_up3, w_gate_up)


def _moe_down_kernel(te_ref, tb_ref, fl_ref, ne_ref, nv_ref, h_ref, bd_ref, w_hbm, y_ref, wbuf, sem, slot_ref):
    t = pl.program_id(1)
    n_valid = nv_ref[t]

    def copies(e, j, slot):
        c0 = pl.multiple_of(j * BN_DOWN, BN_DOWN)
        return (pltpu.make_async_copy(w_hbm.at[e, :, pl.ds(c0, BN_DOWN)], wbuf.at[slot], sem.at[slot]),)

    _advance_weights(te_ref, ne_ref, fl_ref, slot_ref, copies)

    for r0 in range(0, TM_MOE, SUB_MOE):
        @pl.when(n_valid > r0)
        def _():
            y = _bf16_dot(h_ref[r0:r0 + SUB_MOE, :], wbuf[slot_ref[0]]) + bd_ref[...]
            _store_row_tiles(y_ref, r0, _pack_bf16_pairs(y))

        @pl.when(n_valid <= r0)
        def _():
            y_ref[r0 * ROW_SUB:(r0 + SUB_MOE) * ROW_SUB, :] = jnp.zeros((SUB_MOE * ROW_SUB, LANES), y_ref.dtype)


def _moe_down(te, tb, fl, ne, nv, h, w_down, b_down3):
    p_max, f = h.shape
    tm = TM_MOE
    n_tiles = p_max // tm
    n_j = D_MODEL // BN_DOWN
    return pl.pallas_call(
        _moe_down_kernel,
        grid_spec=pltpu.PrefetchScalarGridSpec(
            num_scalar_prefetch=5,
            grid=(n_j, n_tiles),
            in_specs=[
                pl.BlockSpec((tm, f), lambda j, t, te, tb, fl, ne, nv: (tb[t], 0)),
                pl.BlockSpec((None, 1, BN_DOWN), lambda j, t, te, tb, fl, ne, nv: (te[t], 0, j)),
                pl.BlockSpec(memory_space=pl.ANY),
            ],
            out_specs=pl.BlockSpec((tm * ROW_SUB, LANES), lambda j, t, te, tb, fl, ne, nv: (t, 0)),
            scratch_shapes=[pltpu.VMEM((2, f, BN_DOWN), F32), pltpu.SemaphoreType.DMA((2,)),
                            pltpu.SMEM((1,), jnp.int32)],
        ),
        out_shape=jax.ShapeDtypeStruct((p_max * ROW_SUB, LANES), jnp.uint32),
        compiler_params=pltpu.CompilerParams(
            dimension_semantics=("arbitrary", "arbitrary"), vmem_limit_bytes=VMEM_LIMIT),
        name="moe_down",
    )(te, tb, fl, ne, nv, h, b_down3, w_down)


def _combine_kernel(pos_ref, x1_ref, gate_ref, gf_ref, y_hbm, op_ref, os_ref, buf, sem, *, n_prompt_tiles):
    i = pl.program_id(0)
    n_i = pl.num_programs(0)
    tc = x1_ref.shape[0]
    rows = TOP_K * tc

    def issue(tile, slot):
        base = tile * rows

        def body(g, carry):
            r0 = g * ISSUE_UNROLL
            for u in range(ISSUE_UNROLL):
                p = pos_ref[base + r0 + u]
                pltpu.make_async_copy(_row_tile(y_hbm, p), _row_tile(buf.at[slot], r0 + u),
                                      sem.at[slot]).start(priority=u % 2)
            return carry
        lax.fori_loop(0, rows // ISSUE_UNROLL, body, 0)

    @pl.when(i == 0)
    def _():
        issue(0, 0)

    @pl.when(i + 1 < n_i)
    def _():
        issue(i + 1, (i + 1) % 2)

    slot = i % 2
    pltpu.make_async_copy(y_hbm.at[pl.ds(0, rows * ROW_SUB)], buf.at[slot], sem.at[slot]).wait()
    his = [None] * ROW_SUB
    los = [None] * ROW_SUB
    for k in range(TOP_K):
        g = gate_ref[:, k:k + 1]
        for s, chunk in enumerate(_load_row_tiles(buf.at[slot], k * tc, tc)):
            hi, lo = _unpack_hi_lo(chunk)
            his[s] = g * hi if k == 0 else his[s] + g * hi
            los[s] = g * lo if k == 0 else los[s] + g * lo
    acc = x1_ref[...] + jnp.concatenate(his + los, axis=1)
    out = _rms_rows(acc, gf_ref[...])

    @pl.when(i < n_prompt_tiles)
    def _():
        op_ref[...] = out

    @pl.when(i >= n_prompt_tiles)
    def _():
        os_ref[...] = out


def _combine(pos_flat, x1, gates_nk, gf, y_sorted, n_prompt):
    n, d = x1.shape
    tc = TC_COMB
    n_tiles = n // tc
    n_pt = n_prompt // tc
    kern = functools.partial(_combine_kernel, n_prompt_tiles=n_pt)
    return pl.pallas_call(
        kern,
        grid_spec=pltpu.PrefetchScalarGridSpec(
            num_scalar_prefetch=1,
            grid=(n_tiles,),
            in_specs=[
                pl.BlockSpec((tc, d), lambda i, p: (i, 0)),
                pl.BlockSpec((tc, TOP_K), lambda i, p: (i, 0)),
                pl.BlockSpec((1, d), lambda i, p: (0, 0)),
                pl.BlockSpec(memory_space=pl.ANY),
            ],
            out_specs=[
                pl.BlockSpec((tc, d), lambda i, p: (jnp.minimum(i, n_pt - 1), 0)),
                pl.BlockSpec((tc, d), lambda i, p: (jnp.maximum(i - n_pt, 0), 0)),
            ],
            scratch_shapes=[pltpu.VMEM((2, TOP_K * tc * ROW_SUB, LANES), jnp.uint32),
                            pltpu.SemaphoreType.DMA((2,))],
        ),
        out_shape=[
            jax.ShapeDtypeStruct((n_prompt, d), F32),
            jax.ShapeDtypeStruct((n - n_prompt, d), F32),
        ],
        compiler_params=pltpu.CompilerParams(
            dimension_semantics=("arbitrary",), vmem_limit_bytes=VMEM_LIMIT),
        name="combine",
    )(pos_flat, x1, gates_nk, gf, y_sorted)


def kernel(x_prompt, x_sample, state_conv_a, state_conv_b, meta_tokens, norm1_g, w_in, conv_a_w, conv_b_w,
           conv_b_b, ln_b_g, ln_b_b, w_out, norm2_g, w_router, b_router, w_gate_up, b_gate_up, w_down,
           b_down, final_norm_g):
    bp, seq, d = x_prompt.shape
    n_seq, n_t, _ = x_sample.shape
    n_prompt = bp * seq
    n_sample = n_seq * n_t
    n = n_prompt + n_sample
    assert norm1_g.shape[0] == 1, "single layer"
    assert seq % T_MIX == 0 and n_prompt % TM_POST == 0 and n_sample == TM_POST
    assert n_prompt % n_sample == 0 and n % TC_COMB == 0 and n_prompt % TC_COMB == 0

    g1 = norm1_g[0][None]
    win_bf = w_in[0].astype(BF16)
    wout_bf = w_out[0].astype(BF16)
    caw, cbw = conv_a_w[0], conv_b_w[0]
    cbb, lng, lnb = conv_b_b[0][None], ln_b_g[0][None], ln_b_b[0][None]

    xs_t = jnp.transpose(x_sample, (1, 0, 2)).reshape(n_sample, d)
    sa_t = jnp.transpose(state_conv_a[0], (1, 0, 2))
    sb_t = jnp.transpose(state_conv_b[0], (1, 0, 2))

    ymix_p, pa, pb = _mixer_prompt(x_prompt, meta_tokens, g1, win_bf, caw, cbw, cbb, lng, lnb)
    ymix_s, na_t, glu_t = _mixer_sample(xs_t, g1, win_bf, sa_t, sb_t, caw, cbw, cbb, lng, lnb, n_seq, n_t)

    x1, xn, idx, gates, rank, cnt = _post_mixer(
        ymix_p, ymix_s, x_prompt.reshape(n_prompt, d), xs_t, wout_bf, norm2_g[0][None],
        jnp.transpose(w_router[0]), b_router[0][:, None])

    tm = TM_MOE
    n_assign = n * TOP_K
    n_tiles = (n_assign + N_EXPERTS * (tm - 1)) // tm
    p_max = n_tiles * tm
    counts = cnt[:, 0]
    tiles_per_e = (counts + tm - 1) // tm
    tile_end = jnp.cumsum(tiles_per_e)
    tile_start = tile_end - tiles_per_e
    n_used = tile_end[-1]
    e_ar = jnp.arange(N_EXPERTS, dtype=jnp.int32)
    start_of = jnp.sum(jnp.where(idx[None] == e_ar[:, None, None], tile_start[:, None, None], 0), axis=0)
    pos = (start_of * tm + rank).astype(jnp.int32)
    tid = jnp.arange(n_tiles, dtype=jnp.int32)
    tb = jnp.maximum(jnp.minimum(tid, n_used - 1), 0).astype(jnp.int32)
    expert_of = lambda tile: jnp.minimum(
        jnp.sum((tile_end[None, :] <= tile[:, None]).astype(jnp.int32), axis=1), N_EXPERTS - 1).astype(jnp.int32)
    te = expert_of(tb)
    valid = tid < n_used
    new_w = valid & ((tid == 0) | (te != jnp.roll(te, 1)))
    fl = new_w.astype(jnp.int32) * _FLAG_NEW_WEIGHTS
    mine = te[:, None] == e_ar[None, :]
    rows_left = (jnp.sum(jnp.where(mine, counts[None, :], 0), axis=1)
                 - (tid - jnp.sum(jnp.where(mine, tile_start[None, :], 0), axis=1)) * tm)
    nv = jnp.where(valid, jnp.clip(rows_left, 0, tm), 0).astype(jnp.int32)
    next_tile = jnp.sum(jnp.where(te[:, None] == e_ar[None, :], tile_end[None, :], 0), axis=1)
    ne = jnp.where(next_tile < n_used, expert_of(next_tile), -1).astype(jnp.int32)
    partial = (counts % tm) != 0
    zero_flags = ((tid >= n_used) | jnp.any((tid[:, None] == (tile_end - 1)[None, :]) & partial[None, :],
                                            axis=1)).astype(jnp.int32)

    by_tile = lambda tc: pos.reshape(TOP_K, n // tc, tc).transpose(1, 0, 2).reshape(-1)
    xs_sorted = _dispatch(by_tile(TC_DISP), zero_flags, xn, p_max)
    h = _moe_up(te, tb, fl, ne, nv, xs_sorted, w_gate_up[0], b_gate_up[0][:, None, :])
    y_sorted = _moe_down(te, tb, fl, ne, nv, h, w_down[0], b_down[0][:, None, :])

    yp, ys_t = _combine(by_tile(TC_COMB), x1, jnp.transpose(gates), final_norm_g[None], y_sorted, n_prompt)

    y_prompt = yp.reshape(bp, seq, d)
    y_sample = jnp.transpose(ys_t.reshape(n_t, n_seq, d), (1, 0, 2))
    new_a_prompt = pa[None]
    new_b_prompt = pb[None]
    new_a_sample = jnp.transpose(na_t, (1, 0, 2))[None]
    glu_s = jnp.transpose(glu_t, (1, 0, 2))
    new_b_sample = jnp.concatenate([state_conv_b[0][:, n_t:], glu_s], axis=1)[None]
    return (y_prompt, y_sample, new_a_prompt, new_b_prompt, new_a_sample, new_b_sample)
```

```python
import functools

import jax
import jax.numpy as jnp
from jax import lax
from jax.experimental import pallas as pl
from jax.experimental.pallas import tpu as pltpu

F32 = jnp.float32
BF16 = jnp.bfloat16

D_MODEL = 2048
N_META = 16
C_A = 1024
C_B = 1024
K_A = 3
K_B = 31
N_EXPERTS = 32
TOP_K = 4
D_FF = 2048
SWIGLU_LIMIT = 7.0
SWIGLU_ALPHA = 1.702
RMS_EPS = 1e-5
LN_EPS = 1e-5

LANES = 128
N_CHUNK = C_B // LANES
HALO_B = 32
HALO_A = 8
T_MIX = 256
TM_POST = 512
TM_MOE = 512
SUB_MOE = 256
BF_UP = 1024
BN_DOWN = 2048
ROW_SUB = D_MODEL // 2 // LANES
TC_COMB = 128
TC_DISP = 512
ISSUE_UNROLL = 8
VMEM_LIMIT = 56 * 1024 * 1024


def _sigmoid(x):
    return jax.nn.sigmoid(x)


def _rms_rows(x, g):
    ms = jnp.mean(x * x, axis=-1, keepdims=True)
    return (x * lax.rsqrt(ms + RMS_EPS)) * g


def _pack_bf16_pairs(x):
    c = x.shape[1] // 2
    hi = lax.bitcast_convert_type(x[:, :c].astype(BF16).astype(F32), jnp.uint32)
    lo = lax.bitcast_convert_type(x[:, c:].astype(BF16).astype(F32), jnp.uint32)
    return hi | (lo >> 16)


def _unpack_hi_lo(p):
    hi = lax.bitcast_convert_type(p & jnp.uint32(0xFFFF0000), F32)
    lo = lax.bitcast_convert_type(p << 16, F32)
    return hi, lo


def _store_row_tiles(ref, first_row, packed):
    r = packed.shape[0]
    for s in range(ROW_SUB):
        ref[pl.ds(first_row * ROW_SUB + s, r, stride=ROW_SUB), :] = packed[:, s * LANES:(s + 1) * LANES]


def _load_row_tiles(ref, first_row, r):
    return [ref[pl.ds(first_row * ROW_SUB + s, r, stride=ROW_SUB), :] for s in range(ROW_SUB)]


def _mixer_prompt_kernel(x_ref, meta_ref, g1_ref, win_ref, caw_ref, cbw_ref, cbb_ref, lng_ref, lnb_ref,
                         y_ref, pa_ref, pb_ref,
                         u_ref, gb_ref, cv_ref, mgb_ref, mcv_ref, bc_ref):
    b = pl.program_id(0)
    i = pl.program_id(1)
    n_i = pl.num_programs(1)
    t = T_MIX

    def in_proj(rows):
        h = _rms_rows(rows, g1_ref[...]).astype(BF16)
        return jnp.dot(h, win_ref[...], preferred_element_type=F32)

    @pl.when((b == 0) & (i == 0))
    def _():
        um = in_proj(meta_ref[...])
        cvm = um[:, C_A:2 * C_A] * um[:, 2 * C_A:3 * C_A]
        glum = um[:, 3 * C_A:3 * C_A + C_B] * _sigmoid(um[:, 3 * C_A + C_B:])
        for c in range(N_CHUNK):
            sl = slice(c * LANES, (c + 1) * LANES)
            mgb_ref[c, 0:HALO_B - N_META, :] = jnp.zeros((HALO_B - N_META, LANES), F32)
            mgb_ref[c, HALO_B - N_META:HALO_B, :] = glum[:, sl]
            mcv_ref[c] = cvm[N_META - HALO_A:, sl]

    @pl.when(i == 0)
    def _():
        gb_ref[:, 0:HALO_B, :] = mgb_ref[...]
        cv_ref[:, 0:HALO_A, :] = mcv_ref[...]

    u_ref[...] = in_proj(x_ref[...])

    for c in range(N_CHUNK):
        lo = c * LANES
        bg = u_ref[:, lo:lo + LANES]
        cg = u_ref[:, C_A + lo:C_A + lo + LANES]
        v = u_ref[:, 2 * C_A + lo:2 * C_A + lo + LANES]
        ga = u_ref[:, 3 * C_A + lo:3 * C_A + lo + LANES]
        gbv = u_ref[:, 3 * C_A + C_B + lo:3 * C_A + C_B + lo + LANES]
        cv_ref[c, HALO_A:, :] = cg * v
        gb_ref[c, HALO_B:, :] = ga * _sigmoid(gbv)
        acc = cv_ref[c, HALO_A - 2:HALO_A - 2 + t, :] * caw_ref[0:1, lo:lo + LANES]
        for k in range(1, K_A):
            acc = acc + cv_ref[c, HALO_A - 2 + k:HALO_A - 2 + k + t, :] * caw_ref[k:k + 1, lo:lo + LANES]
        y_ref[:, lo:lo + LANES] = (bg * acc).astype(BF16)
        off = HALO_B - (K_B - 1)
        accb = gb_ref[c, off:off + t, :] * cbw_ref[0:1, lo:lo + LANES]
        for k in range(1, K_B):
            accb = accb + gb_ref[c, off + k:off + k + t, :] * cbw_ref[k:k + 1, lo:lo + LANES]
        bc_ref[:, lo:lo + LANES] = accb + cbb_ref[:, lo:lo + LANES]

    rb = 64
    for r in range(t // rb):
        xb = bc_ref[r * rb:(r + 1) * rb, :]
        mu = jnp.mean(xb, axis=-1, keepdims=True)
        var = jnp.mean(jnp.square(xb - mu), axis=-1, keepdims=True)
        bn = (xb - mu) * lax.rsqrt(var + LN_EPS) * lng_ref[...] + lnb_ref[...]
        y_ref[r * rb:(r + 1) * rb, C_A:] = (bn * _sigmoid(bn)).astype(BF16)

    @pl.when(i == n_i - 1)
    def _():
        for c in range(N_CHUNK):
            sl = slice(c * LANES, (c + 1) * LANES)
            pa_ref[:, sl] = cv_ref[c, HALO_A + t - (K_A - 1):HALO_A + t, :]
            pb_ref[:, sl] = gb_ref[c, HALO_B + t - (K_B - 1):HALO_B + t, :]

    gb_ref[:, 0:HALO_B, :] = gb_ref[:, t:t + HALO_B, :]
    cv_ref[:, 0:HALO_A, :] = cv_ref[:, t:t + HALO_A, :]


def _mixer_prompt(x_prompt, meta, g1, win_bf, caw, cbw, cbb, lng, lnb):
    bp, seq, d = x_prompt.shape
    n_i = seq // T_MIX
    full = lambda shape: pl.BlockSpec(shape, lambda b, i: (0,) * len(shape))
    return pl.pallas_call(
        _mixer_prompt_kernel,
        grid=(bp, n_i),
        in_specs=[
            pl.BlockSpec((None, T_MIX, d), lambda b, i: (b, i, 0)),
            full((N_META, d)),
            full((1, d)),
            pl.BlockSpec(win_bf.shape, lambda b, i: (0, 0), pipeline_mode=pl.Buffered(1)),
            full((K_A, C_A)),
            full((K_B, C_B)),
            full((1, C_B)),
            full((1, C_B)),
            full((1, C_B)),
        ],
        out_specs=[
            pl.BlockSpec((T_MIX, d), lambda b, i: (b * n_i + i, 0)),
            pl.BlockSpec((None, K_A - 1, C_A), lambda b, i: (b, 0, 0)),
            pl.BlockSpec((None, K_B - 1, C_B), lambda b, i: (b, 0, 0)),
        ],
        out_shape=[
            jax.ShapeDtypeStruct((bp * seq, d), BF16),
            jax.ShapeDtypeStruct((bp, K_A - 1, C_A), F32),
            jax.ShapeDtypeStruct((bp, K_B - 1, C_B), F32),
        ],
        scratch_shapes=[
            pltpu.VMEM((T_MIX, win_bf.shape[1]), F32),
            pltpu.VMEM((N_CHUNK, T_MIX + HALO_B, LANES), F32),
            pltpu.VMEM((N_CHUNK, T_MIX + HALO_A, LANES), F32),
            pltpu.VMEM((N_CHUNK, HALO_B, LANES), F32),
            pltpu.VMEM((N_CHUNK, HALO_A, LANES), F32),
            pltpu.VMEM((T_MIX, C_B), F32),
        ],
        compiler_params=pltpu.CompilerParams(
            dimension_semantics=("arbitrary", "arbitrary"), vmem_limit_bytes=VMEM_LIMIT),
        name="mixer_prompt",
    )(x_prompt, meta, g1, win_bf, caw, cbw, cbb, lng, lnb)


def _mixer_sample_kernel(x_ref, g1_ref, wbg_ref, wcg_ref, wv_ref, wga_ref, wgb_ref, sa_ref, sb_ref,
                         caw_ref, cbw_ref, cbb_ref, lng_ref, lnb_ref,
                         y_ref, na_ref, glu_ref,
                         h_ref, ya_ref, bc_ref, *, n_seq, n_t):
    c = pl.program_id(0)

    @pl.when(c == 0)
    def _():
        h_ref[...] = _rms_rows(x_ref[...], g1_ref[...]).astype(BF16)

    @pl.when(c < N_CHUNK)
    def _():
        h = h_ref[...]
        proj = lambda w_ref: jnp.dot(h, w_ref[...], preferred_element_type=F32)
        bg, cg, v, ga, gbv = proj(wbg_ref), proj(wcg_ref), proj(wv_ref), proj(wga_ref), proj(wgb_ref)
        cv = cg * v
        glu = ga * _sigmoid(gbv)
        row = lambda a, tt: a[tt * n_seq:(tt + 1) * n_seq, :]
        xa = [sa_ref[j] for j in range(K_A - 1)] + [row(cv, tt) for tt in range(n_t)]
        xb = [sb_ref[j] for j in range(K_B - 1)] + [row(glu, tt) for tt in range(n_t)]
        for tt in range(n_t):
            acc = xa[tt] * caw_ref[0:1, :]
            for k in range(1, K_A):
                acc = acc + xa[tt + k] * caw_ref[k:k + 1, :]
            ya_ref[c, tt * n_seq:(tt + 1) * n_seq, :] = row(bg, tt) * acc
            accb = xb[tt] * cbw_ref[0:1, :]
            for k in range(1, K_B):
                accb = accb + xb[tt + k] * cbw_ref[k:k + 1, :]
            bc_ref[c, tt * n_seq:(tt + 1) * n_seq, :] = accb + cbb_ref[...]
            glu_ref[tt] = row(glu, tt)
        for j in range(K_A - 1):
            na_ref[j] = row(cv, n_t - (K_A - 1) + j)

    @pl.when(c == N_CHUNK)
    def _():
        xb = jnp.concatenate([bc_ref[cc] for cc in range(N_CHUNK)], axis=1)
        mu = jnp.mean(xb, axis=-1, keepdims=True)
        var = jnp.mean(jnp.square(xb - mu), axis=-1, keepdims=True)
        bn = (xb - mu) * lax.rsqrt(var + LN_EPS) * lng_ref[...] + lnb_ref[...]
        for cc in range(N_CHUNK):
            y_ref[:, cc * LANES:(cc + 1) * LANES] = ya_ref[cc].astype(BF16)
        y_ref[:, C_A:] = (bn * _sigmoid(bn)).astype(BF16)


def _mixer_sample(xs_t, g1, win_bf, sa_t, sb_t, caw, cbw, cbb, lng, lnb, n_seq, n_t):
    rows, d = xs_t.shape
    cc = lambda c: jnp.minimum(c, N_CHUNK - 1)
    wspec = lambda g: pl.BlockSpec((d, LANES), lambda c, g=g: (0, g * N_CHUNK + cc(c)))
    full = lambda shape: pl.BlockSpec(shape, lambda c: (0,) * len(shape))
    kern = functools.partial(_mixer_sample_kernel, n_seq=n_seq, n_t=n_t)
    return pl.pallas_call(
        kern,
        grid=(N_CHUNK + 1,),
        in_specs=[
            full((rows, d)),
            full((1, d)),
            wspec(0), wspec(1), wspec(2), wspec(3), wspec(4),
            pl.BlockSpec((K_A - 1, n_seq, LANES), lambda c: (0, 0, cc(c))),
            pl.BlockSpec((K_B - 1, n_seq, LANES), lambda c: (0, 0, cc(c))),
            pl.BlockSpec((K_A, LANES), lambda c: (0, cc(c))),
            pl.BlockSpec((K_B, LANES), lambda c: (0, cc(c))),
            pl.BlockSpec((1, LANES), lambda c: (0, cc(c))),
            full((1, C_B)),
            full((1, C_B)),
        ],
        out_specs=[
            full((rows, d)),
            pl.BlockSpec((K_A - 1, n_seq, LANES), lambda c: (0, 0, cc(c))),
            pl.BlockSpec((n_t, n_seq, LANES), lambda c: (0, 0, cc(c))),
        ],
        out_shape=[
            jax.ShapeDtypeStruct((rows, d), BF16),
            jax.ShapeDtypeStruct((K_A - 1, n_seq, C_A), F32),
            jax.ShapeDtypeStruct((n_t, n_seq, C_B), F32),
        ],
        scratch_shapes=[
            pltpu.VMEM((rows, d), BF16),
            pltpu.VMEM((N_CHUNK, rows, LANES), F32),
            pltpu.VMEM((N_CHUNK, rows, LANES), F32),
        ],
        compiler_params=pltpu.CompilerParams(
            dimension_semantics=("arbitrary",), vmem_limit_bytes=VMEM_LIMIT),
        name="mixer_sample",
    )(xs_t, g1, win_bf, win_bf, win_bf, win_bf, win_bf, sa_t, sb_t, caw, cbw, cbb, lng, lnb)


def _post_mixer_kernel(yp_ref, ys_ref, xp_ref, xs_ref, wout_ref, g2_ref, wr_ref, br_ref,
                       x1_ref, xn_ref, idx_ref, gate_ref, rank_ref, cnt_ref,
                       run_ref, *, n_prompt_tiles):
    i = pl.program_id(0)
    tm = yp_ref.shape[0]

    @pl.when(i == 0)
    def _():
        run_ref[...] = jnp.zeros(run_ref.shape, F32)

    is_prompt = i < n_prompt_tiles
    x = jnp.where(is_prompt, xp_ref[...], xs_ref[...])
    y = jnp.where(is_prompt, yp_ref[...], ys_ref[...])
    x1 = x + jnp.dot(y, wout_ref[...], preferred_element_type=F32)
    x1_ref[...] = x1
    xn = _rms_rows(x1, g2_ref[...])
    _store_row_tiles(xn_ref, 0, _pack_bf16_pairs(xn))

    logits = lax.dot_general(wr_ref[...], xn, (((1,), (1,)), ((), ())),
                             precision=lax.Precision.HIGHEST, preferred_element_type=F32) + br_ref[...]
    eidx = lax.broadcasted_iota(jnp.int32, logits.shape, 0)
    work = logits
    vals, sels, hots = [], [], []
    for _ in range(TOP_K):
        m = jnp.max(work, axis=0, keepdims=True)
        sel = jnp.min(jnp.where(work == m, eidx, N_EXPERTS), axis=0, keepdims=True)
        hot = eidx == sel
        vals.append(m)
        sels.append(sel)
        hots.append(hot)
        work = jnp.where(hot, -jnp.inf, work)
    exps = [jnp.exp(v - vals[0]) for v in vals]
    denom = exps[0] + exps[1] + exps[2] + exps[3]
    for k in range(TOP_K):
        idx_ref[k:k + 1, :] = sels[k]
        gate_ref[k:k + 1, :] = exps[k] / denom

    chosen = (hots[0] | hots[1] | hots[2] | hots[3])
    chosen_bf = chosen.astype(F32).astype(BF16)
    s_io = lax.broadcasted_iota(jnp.int32, (tm, tm), 0)
    t_io = lax.broadcasted_iota(jnp.int32, (tm, tm), 1)
    upper = (s_io < t_io).astype(F32).astype(BF16)
    before = jnp.dot(chosen_bf, upper, preferred_element_type=F32) + run_ref[:, 0:1]
    for k in range(TOP_K):
        r = jnp.sum(jnp.where(hots[k], before, 0.0), axis=0, keepdims=True)
        rank_ref[k:k + 1, :] = r.astype(jnp.int32)
    run_ref[...] = run_ref[...] + jnp.sum(chosen.astype(F32), axis=1, keepdims=True)
    cnt_ref[...] = run_ref[...].astype(jnp.int32)


def _post_mixer(yp, ys, xp2, xs_t, wout_bf, g2, wr_t, br_col):
    d = yp.shape[1]
    n = yp.shape[0] + ys.shape[0]
    tm = TM_POST
    n_tiles = n // tm
    n_pt = xp2.shape[0] // tm
    kern = functools.partial(_post_mixer_kernel, n_prompt_tiles=n_pt)
    full = lambda shape: pl.BlockSpec(shape, lambda i: (0,) * len(shape))
    return pl.pallas_call(
        kern,
        grid=(n_tiles,),
        in_specs=[
            pl.BlockSpec((tm, d), lambda i: (jnp.minimum(i, n_pt - 1), 0)),
            pl.BlockSpec((tm, d), lambda i: (jnp.maximum(i - n_pt, 0), 0)),
            pl.BlockSpec((tm, d), lambda i: (jnp.minimum(i, n_pt - 1), 0)),
            pl.BlockSpec((tm, d), lambda i: (jnp.maximum(i - n_pt, 0), 0)),
            pl.BlockSpec(wout_bf.shape, lambda i: (0, 0), pipeline_mode=pl.Buffered(1)),
            full((1, d)),
            full((N_EXPERTS, d)),
            full((N_EXPERTS, 1)),
        ],
        out_specs=[
            pl.BlockSpec((tm, d), lambda i: (i, 0)),
            pl.BlockSpec((tm * ROW_SUB, LANES), lambda i: (i, 0)),
            pl.BlockSpec((TOP_K, tm), lambda i: (0, i)),
            pl.BlockSpec((TOP_K, tm), lambda i: (0, i)),
            pl.BlockSpec((TOP_K, tm), lambda i: (0, i)),
            full((N_EXPERTS, LANES)),
        ],
        out_shape=[
            jax.ShapeDtypeStruct((n, d), F32),
            jax.ShapeDtypeStruct((n * ROW_SUB, LANES), jnp.uint32),
            jax.ShapeDtypeStruct((TOP_K, n), jnp.int32),
            jax.ShapeDtypeStruct((TOP_K, n), F32),
            jax.ShapeDtypeStruct((TOP_K, n), jnp.int32),
            jax.ShapeDtypeStruct((N_EXPERTS, LANES), jnp.int32),
        ],
        scratch_shapes=[pltpu.VMEM((N_EXPERTS, LANES), F32)],
        compiler_params=pltpu.CompilerParams(
            dimension_semantics=("arbitrary",), vmem_limit_bytes=VMEM_LIMIT),
        name="post_mixer",
    )(yp, ys, xp2, xs_t, wout_bf, g2, wr_t, br_col)


def _row_tile(ref, row):
    return ref.at[pl.ds(pl.multiple_of(row * ROW_SUB, ROW_SUB), ROW_SUB)]


def _dispatch_kernel(pos_ref, zf_ref, x_ref, xs_hbm, zbuf, sem, zsem):
    i = pl.program_id(0)
    tc = x_ref.shape[0] // ROW_SUB
    zrows = zbuf.shape[0]
    n_tiles = xs_hbm.shape[0] // zrows

    def zero_copy(t):
        return pltpu.make_async_copy(zbuf, xs_hbm.at[pl.ds(pl.multiple_of(t * zrows, zrows), zrows)], zsem)

    @pl.when(i == 0)
    def _():
        zbuf[...] = jnp.zeros(zbuf.shape, zbuf.dtype)

        def start(t, carry):
            @pl.when(zf_ref[t] != 0)
            def _():
                zero_copy(t).start()
            return carry
        lax.fori_loop(0, n_tiles, start, 0)

        def wait(t, carry):
            @pl.when(zf_ref[t] != 0)
            def _():
                zero_copy(t).wait()
            return carry
        lax.fori_loop(0, n_tiles, wait, 0)

    base = i * (TOP_K * tc)

    def body(g, carry):
        r0 = g * ISSUE_UNROLL
        for u in range(ISSUE_UNROLL):
            for k in range(TOP_K):
                p = pos_ref[base + k * tc + r0 + u]
                pltpu.make_async_copy(_row_tile(x_ref, r0 + u), _row_tile(xs_hbm, p), sem).start(
                    priority=(u * TOP_K + k) % 2)
        return carry
    lax.fori_loop(0, tc // ISSUE_UNROLL, body, 0)
    for k in range(TOP_K):
        pltpu.make_async_copy(x_ref, xs_hbm.at[pl.ds(0, tc * ROW_SUB)], sem).wait()


def _dispatch(pos_tiles, zero_flags, xn_tiles, p_max):
    n = xn_tiles.shape[0] // ROW_SUB
    tc = TC_DISP
    return pl.pallas_call(
        _dispatch_kernel,
        grid_spec=pltpu.PrefetchScalarGridSpec(
            num_scalar_prefetch=2,
            grid=(n // tc,),
            in_specs=[pl.BlockSpec((tc * ROW_SUB, LANES), lambda i, p, z: (i, 0))],
            out_specs=pl.BlockSpec(memory_space=pl.ANY),
            scratch_shapes=[pltpu.VMEM((TM_MOE * ROW_SUB, LANES), jnp.uint32), pltpu.SemaphoreType.DMA(()),
                            pltpu.SemaphoreType.DMA(())],
        ),
        out_shape=jax.ShapeDtypeStruct((p_max * ROW_SUB, LANES), jnp.uint32),
        compiler_params=pltpu.CompilerParams(
            dimension_semantics=("arbitrary",), vmem_limit_bytes=VMEM_LIMIT),
        name="dispatch",
    )(pos_tiles, zero_flags, xn_tiles)


_FLAG_NEW_WEIGHTS = 1
WEIGHT_DMA_PRIORITY = 1


def _advance_weights(te_ref, ne_ref, fl_ref, slot_ref, copies):
    j = pl.program_id(0)
    t = pl.program_id(1)
    n_j = pl.num_programs(0)

    @pl.when((fl_ref[t] & _FLAG_NEW_WEIGHTS) != 0)
    def _():
        first = (j == 0) & (t == 0)

        @pl.when(first)
        def _():
            slot_ref[0] = 0
            for c in copies(te_ref[t], j, 0):
                c.start(priority=WEIGHT_DMA_PRIORITY)

        @pl.when(jnp.logical_not(first))
        def _():
            slot_ref[0] = 1 - slot_ref[0]

        s = slot_ref[0]
        for c in copies(te_ref[t], j, s):
            c.wait()
        ne = ne_ref[t]

        @pl.when(ne >= 0)
        def _():
            for c in copies(ne, j, 1 - s):
                c.start(priority=WEIGHT_DMA_PRIORITY)

        @pl.when((ne < 0) & (j + 1 < n_j))
        def _():
            for c in copies(te_ref[0], j + 1, 1 - s):
                c.start(priority=WEIGHT_DMA_PRIORITY)


def _by_valid_rows(n_valid, compute, zero_from):
    @pl.when(n_valid > SUB_MOE)
    def _():
        compute(TM_MOE)

    @pl.when((n_valid > 0) & (n_valid <= SUB_MOE))
    def _():
        compute(SUB_MOE)
        zero_from(SUB_MOE)

    @pl.when(n_valid == 0)
    def _():
        zero_from(0)


def _bf16_dot(x_bf, w_f32):
    return lax.dot_general(x_bf, w_f32, (((1,), (0,)), ((), ())), preferred_element_type=F32)


def _moe_up_kernel(te_ref, tb_ref, fl_ref, ne_ref, nv_ref, x_ref, bg_ref, bu_ref, w_hbm, h_ref, wbuf, sem,
                   slot_ref):
    t = pl.program_id(1)
    n_valid = nv_ref[t]

    def copies(e, j, slot):
        cg = pl.multiple_of(j * BF_UP, BF_UP)
        cu = pl.multiple_of(D_FF + j * BF_UP, BF_UP)
        return (pltpu.make_async_copy(w_hbm.at[e, :, pl.ds(cg, BF_UP)], wbuf.at[slot, 0], sem.at[slot]),
                pltpu.make_async_copy(w_hbm.at[e, :, pl.ds(cu, BF_UP)], wbuf.at[slot, 1], sem.at[slot]))

    _advance_weights(te_ref, ne_ref, fl_ref, slot_ref, copies)

    def compute(n_rows):
        s = slot_ref[0]
        halves = [_unpack_hi_lo(c) for c in _load_row_tiles(x_ref, 0, n_rows)]
        x = jnp.concatenate([hl[0].astype(BF16) for hl in halves] + [hl[1].astype(BF16) for hl in halves], axis=1)
        gate = _bf16_dot(x, wbuf[s, 0]) + bg_ref[...]
        up = _bf16_dot(x, wbuf[s, 1]) + bu_ref[...]
        gate = jnp.minimum(gate, SWIGLU_LIMIT)
        up = jnp.clip(up, -SWIGLU_LIMIT, SWIGLU_LIMIT)
        act = gate * _sigmoid(SWIGLU_ALPHA * gate) * (up + 1.0)
        h_ref[0:n_rows, :] = act.astype(BF16)

    def zero_from(r0):
        h_ref[r0:TM_MOE, :] = jnp.zeros((TM_MOE - r0, h_ref.shape[1]), BF16)

    _by_valid_rows(n_valid, compute, zero_from)


def _moe_up(te, tb, fl, ne, nv, xs, w_gate_up, b_gate_up3):
    p_max = xs.shape[0] // ROW_SUB
    d = D_MODEL
    tm = TM_MOE
    n_tiles = p_max // tm
    n_j = D_FF // BF_UP
    return pl.pallas_call(
        _moe_up_kernel,
        grid_spec=pltpu.PrefetchScalarGridSpec(
            num_scalar_prefetch=5,
            grid=(n_j, n_tiles),
            in_specs=[
                pl.BlockSpec((tm * ROW_SUB, LANES), lambda j, t, te, tb, fl, ne, nv: (tb[t], 0)),
                pl.BlockSpec((None, 1, BF_UP), lambda j, t, te, tb, fl, ne, nv: (te[t], 0, j)),
                pl.BlockSpec((None, 1, BF_UP), lambda j, t, te, tb, fl, ne, nv: (te[t], 0, n_j + j)),
                pl.BlockSpec(memory_space=pl.ANY),
            ],
            out_specs=pl.BlockSpec((tm, BF_UP), lambda j, t, te, tb, fl, ne, nv: (t, j)),
            scratch_shapes=[pltpu.VMEM((2, 2, d, BF_UP), F32), pltpu.SemaphoreType.DMA((2,)),
                            pltpu.SMEM((1,), jnp.int32)],
        ),
        out_shape=jax.ShapeDtypeStruct((p_max, D_FF), BF16),
        compiler_params=pltpu.CompilerParams(
            dimension_semantics=("arbitrary", "arbitrary"), vmem_limit_bytes=VMEM_LIMIT),
        name="moe_up",
    )(te, tb, fl, ne, nv, xs, b_gate_up3, b_gate_up3, w_gate_up)


def _moe_down_kernel(te_ref, tb_ref, fl_ref, ne_ref, nv_ref, h_ref, bd_ref, w_hbm, y_ref, wbuf, sem, slot_ref):
    t = pl.program_id(1)
    n_valid = nv_ref[t]

    def copies(e, j, slot):
        c0 = pl.multiple_of(j * BN_DOWN, BN_DOWN)
        return (pltpu.make_async_copy(w_hbm.at[e, :, pl.ds(c0, BN_DOWN)], wbuf.at[slot], sem.at[slot]),)

    _advance_weights(te_ref, ne_ref, fl_ref, slot_ref, copies)

    def compute(n_rows):
        y = _bf16_dot(h_ref[0:n_rows, :], wbuf[slot_ref[0]]) + bd_ref[...]
        _store_row_tiles(y_ref, 0, _pack_bf16_pairs(y))

    def zero_from(r0):
        y_ref[r0 * ROW_SUB:TM_MOE * ROW_SUB, :] = jnp.zeros(((TM_MOE - r0) * ROW_SUB, LANES), y_ref.dtype)

    _by_valid_rows(n_valid, compute, zero_from)


def _moe_down(te, tb, fl, ne, nv, h, w_down, b_down3):
    p_max, f = h.shape
    tm = TM_MOE
    n_tiles = p_max // tm
    n_j = D_MODEL // BN_DOWN
    return pl.pallas_call(
        _moe_down_kernel,
        grid_spec=pltpu.PrefetchScalarGridSpec(
            num_scalar_prefetch=5,
            grid=(n_j, n_tiles),
            in_specs=[
                pl.BlockSpec((tm, f), lambda j, t, te, tb, fl, ne, nv: (tb[t], 0)),
                pl.BlockSpec((None, 1, BN_DOWN), lambda j, t, te, tb, fl, ne, nv: (te[t], 0, j)),
                pl.BlockSpec(memory_space=pl.ANY),
            ],
            out_specs=pl.BlockSpec((tm * ROW_SUB, LANES), lambda j, t, te, tb, fl, ne, nv: (t, 0)),
            scratch_shapes=[pltpu.VMEM((2, f, BN_DOWN), F32), pltpu.SemaphoreType.DMA((2,)),
                            pltpu.SMEM((1,), jnp.int32)],
        ),
        out_shape=jax.ShapeDtypeStruct((p_max * ROW_SUB, LANES), jnp.uint32),
        compiler_params=pltpu.CompilerParams(
            dimension_semantics=("arbitrary", "arbitrary"), vmem_limit_bytes=VMEM_LIMIT),
        name="moe_down",
    )(te, tb, fl, ne, nv, h, b_down3, w_down)


def _combine_kernel(pos_ref, x1_ref, gate_ref, gf_ref, y_hbm, op_ref, os_ref, buf, sem, *, n_prompt_tiles):
    i = pl.program_id(0)
    n_i = pl.num_programs(0)
    tc = x1_ref.shape[0]
    rows = TOP_K * tc

    def issue(tile, slot):
        base = tile * rows

        def body(g, carry):
            r0 = g * ISSUE_UNROLL
            for u in range(ISSUE_UNROLL):
                p = pos_ref[base + r0 + u]
                pltpu.make_async_copy(_row_tile(y_hbm, p), _row_tile(buf.at[slot], r0 + u),
                                      sem.at[slot]).start(priority=u % 2)
            return carry
        lax.fori_loop(0, rows // ISSUE_UNROLL, body, 0)

    @pl.when(i == 0)
    def _():
        issue(0, 0)

    @pl.when(i + 1 < n_i)
    def _():
        issue(i + 1, (i + 1) % 2)

    slot = i % 2
    pltpu.make_async_copy(y_hbm.at[pl.ds(0, rows * ROW_SUB)], buf.at[slot], sem.at[slot]).wait()
    his = [None] * ROW_SUB
    los = [None] * ROW_SUB
    for k in range(TOP_K):
        g = gate_ref[:, k:k + 1]
        for s, chunk in enumerate(_load_row_tiles(buf.at[slot], k * tc, tc)):
            hi, lo = _unpack_hi_lo(chunk)
            his[s] = g * hi if k == 0 else his[s] + g * hi
            los[s] = g * lo if k == 0 else los[s] + g * lo
    acc = x1_ref[...] + jnp.concatenate(his + los, axis=1)
    out = _rms_rows(acc, gf_ref[...])

    @pl.when(i < n_prompt_tiles)
    def _():
        op_ref[...] = out

    @pl.when(i >= n_prompt_tiles)
    def _():
        os_ref[...] = out


def _combine(pos_flat, x1, gates_nk, gf, y_sorted, n_prompt):
    n, d = x1.shape
    tc = TC_COMB
    n_tiles = n // tc
    n_pt = n_prompt // tc
    kern = functools.partial(_combine_kernel, n_prompt_tiles=n_pt)
    return pl.pallas_call(
        kern,
        grid_spec=pltpu.PrefetchScalarGridSpec(
            num_scalar_prefetch=1,
            grid=(n_tiles,),
            in_specs=[
                pl.BlockSpec((tc, d), lambda i, p: (i, 0)),
                pl.BlockSpec((tc, TOP_K), lambda i, p: (i, 0)),
                pl.BlockSpec((1, d), lambda i, p: (0, 0)),
                pl.BlockSpec(memory_space=pl.ANY),
            ],
            out_specs=[
                pl.BlockSpec((tc, d), lambda i, p: (jnp.minimum(i, n_pt - 1), 0)),
                pl.BlockSpec((tc, d), lambda i, p: (jnp.maximum(i - n_pt, 0), 0)),
            ],
            scratch_shapes=[pltpu.VMEM((2, TOP_K * tc * ROW_SUB, LANES), jnp.uint32),
                            pltpu.SemaphoreType.DMA((2,))],
        ),
        out_shape=[
            jax.ShapeDtypeStruct((n_prompt, d), F32),
            jax.ShapeDtypeStruct((n - n_prompt, d), F32),
        ],
        compiler_params=pltpu.CompilerParams(
            dimension_semantics=("arbitrary",), vmem_limit_bytes=VMEM_LIMIT),
        name="combine",
    )(pos_flat, x1, gates_nk, gf, y_sorted)


def kernel(x_prompt, x_sample, state_conv_a, state_conv_b, meta_tokens, norm1_g, w_in, conv_a_w, conv_b_w,
           conv_b_b, ln_b_g, ln_b_b, w_out, norm2_g, w_router, b_router, w_gate_up, b_gate_up, w_down,
           b_down, final_norm_g):
    bp, seq, d = x_prompt.shape
    n_seq, n_t, _ = x_sample.shape
    n_prompt = bp * seq
    n_sample = n_seq * n_t
    n = n_prompt + n_sample
    assert norm1_g.shape[0] == 1, "single layer"
    assert seq % T_MIX == 0 and n_prompt % TM_POST == 0 and n_sample == TM_POST
    assert n_prompt % n_sample == 0 and n % TC_COMB == 0 and n_prompt % TC_COMB == 0

    g1 = norm1_g[0][None]
    win_bf = w_in[0].astype(BF16)
    wout_bf = w_out[0].astype(BF16)
    caw, cbw = conv_a_w[0], conv_b_w[0]
    cbb, lng, lnb = conv_b_b[0][None], ln_b_g[0][None], ln_b_b[0][None]

    xs_t = jnp.transpose(x_sample, (1, 0, 2)).reshape(n_sample, d)
    sa_t = jnp.transpose(state_conv_a[0], (1, 0, 2))
    sb_t = jnp.transpose(state_conv_b[0], (1, 0, 2))

    ymix_p, pa, pb = _mixer_prompt(x_prompt, meta_tokens, g1, win_bf, caw, cbw, cbb, lng, lnb)
    ymix_s, na_t, glu_t = _mixer_sample(xs_t, g1, win_bf, sa_t, sb_t, caw, cbw, cbb, lng, lnb, n_seq, n_t)

    x1, xn, idx, gates, rank, cnt = _post_mixer(
        ymix_p, ymix_s, x_prompt.reshape(n_prompt, d), xs_t, wout_bf, norm2_g[0][None],
        jnp.transpose(w_router[0]), b_router[0][:, None])

    tm = TM_MOE
    n_assign = n * TOP_K
    n_tiles = (n_assign + N_EXPERTS * (tm - 1)) // tm
    p_max = n_tiles * tm
    counts = cnt[:, 0]
    tiles_per_e = (counts + tm - 1) // tm
    tile_end = jnp.cumsum(tiles_per_e)
    tile_start = tile_end - tiles_per_e
    n_used = tile_end[-1]
    e_ar = jnp.arange(N_EXPERTS, dtype=jnp.int32)
    start_of = jnp.sum(jnp.where(idx[None] == e_ar[:, None, None], tile_start[:, None, None], 0), axis=0)
    pos = (start_of * tm + rank).astype(jnp.int32)
    tid = jnp.arange(n_tiles, dtype=jnp.int32)
    tb = jnp.maximum(jnp.minimum(tid, n_used - 1), 0).astype(jnp.int32)
    expert_of = lambda tile: jnp.minimum(
        jnp.sum((tile_end[None, :] <= tile[:, None]).astype(jnp.int32), axis=1), N_EXPERTS - 1).astype(jnp.int32)
    te = expert_of(tb)
    valid = tid < n_used
    new_w = valid & ((tid == 0) | (te != jnp.roll(te, 1)))
    fl = new_w.astype(jnp.int32) * _FLAG_NEW_WEIGHTS
    mine = te[:, None] == e_ar[None, :]
    rows_left = (jnp.sum(jnp.where(mine, counts[None, :], 0), axis=1)
                 - (tid - jnp.sum(jnp.where(mine, tile_start[None, :], 0), axis=1)) * tm)
    nv = jnp.where(valid, jnp.clip(rows_left, 0, tm), 0).astype(jnp.int32)
    next_tile = jnp.sum(jnp.where(te[:, None] == e_ar[None, :], tile_end[None, :], 0), axis=1)
    ne = jnp.where(next_tile < n_used, expert_of(next_tile), -1).astype(jnp.int32)
    partial = (counts % tm) != 0
    zero_flags = ((tid >= n_used) | jnp.any((tid[:, None] == (tile_end - 1)[None, :]) & partial[None, :],
                                            axis=1)).astype(jnp.int32)

    by_tile = lambda tc: pos.reshape(TOP_K, n // tc, tc).transpose(1, 0, 2).reshape(-1)
    xs_sorted = _dispatch(by_tile(TC_DISP), zero_flags, xn, p_max)
    h = _moe_up(te, tb, fl, ne, nv, xs_sorted, w_gate_up[0], b_gate_up[0][:, None, :])
    y_sorted = _moe_down(te, tb, fl, ne, nv, h, w_down[0], b_down[0][:, None, :])

    yp, ys_t = _combine(by_tile(TC_COMB), x1, jnp.transpose(gates), final_norm_g[None], y_sorted, n_prompt)

    y_prompt = yp.reshape(bp, seq, d)
    y_sample = jnp.transpose(ys_t.reshape(n_t, n_seq, d), (1, 0, 2))
    new_a_prompt = pa[None]
    new_b_prompt = pb[None]
    new_a_sample = jnp.transpose(na_t, (1, 0, 2))[None]
    glu_s = jnp.transpose(glu_t, (1, 0, 2))
    new_b_sample = jnp.concatenate([state_conv_b[0][:, n_t:], glu_s], axis=1)[None]
    return (y_prompt, y_sample, new_a_prompt, new_b_prompt, new_a_sample, new_b_sample)
```

```python
import functools

import jax
import jax.numpy as jnp
from jax import lax
from jax.experimental import pallas as pl
from jax.experimental.pallas import tpu as pltpu

F32 = jnp.float32
BF16 = jnp.bfloat16

D_MODEL = 2048
N_META = 16
C_A = 1024
C_B = 1024
K_A = 3
K_B = 31
N_EXPERTS = 32
TOP_K = 4
D_FF = 2048
SWIGLU_LIMIT = 7.0
SWIGLU_ALPHA = 1.702
RMS_EPS = 1e-5
LN_EPS = 1e-5

LANES = 128
N_CHUNK = C_B // LANES
HALO_B = 32
HALO_A = 8
T_MIX = 256
TM_POST = 512
TM_MOE = 512
SUB_MOE = 128
BF_UP = 1024
BN_DOWN = 2048
ROW_SUB = D_MODEL // 2 // LANES
TC_COMB = 256
TC_DISP = 512
ISSUE_UNROLL = 8
VMEM_LIMIT = 56 * 1024 * 1024


def _sigmoid(x):
    return jax.nn.sigmoid(x)


def _rms_rows(x, g):
    ms = jnp.mean(x * x, axis=-1, keepdims=True)
    return (x * lax.rsqrt(ms + RMS_EPS)) * g


def _pack_bf16_pairs(x):
    c = x.shape[1] // 2
    hi = lax.bitcast_convert_type(x[:, :c].astype(BF16).astype(F32), jnp.uint32)
    lo = lax.bitcast_convert_type(x[:, c:].astype(BF16).astype(F32), jnp.uint32)
    return hi | (lo >> 16)


def _unpack_hi_lo(p):
    hi = lax.bitcast_convert_type(p & jnp.uint32(0xFFFF0000), F32)
    lo = lax.bitcast_convert_type(p << 16, F32)
    return hi, lo


def _store_row_tiles(ref, first_row, packed):
    r = packed.shape[0]
    for s in range(ROW_SUB):
        ref[pl.ds(first_row * ROW_SUB + s, r, stride=ROW_SUB), :] = packed[:, s * LANES:(s + 1) * LANES]


def _load_row_tiles(ref, first_row, r):
    return [ref[pl.ds(first_row * ROW_SUB + s, r, stride=ROW_SUB), :] for s in range(ROW_SUB)]


def _mixer_prompt_kernel(x_ref, meta_ref, g1_ref, win_ref, caw_ref, cbw_ref, cbb_ref, lng_ref, lnb_ref,
                         y_ref, pa_ref, pb_ref,
                         u_ref, gb_ref, cv_ref, mgb_ref, mcv_ref, bc_ref):
    b = pl.program_id(0)
    i = pl.program_id(1)
    n_i = pl.num_programs(1)
    t = T_MIX

    def in_proj(rows):
        h = _rms_rows(rows, g1_ref[...]).astype(BF16)
        return jnp.dot(h, win_ref[...], preferred_element_type=F32)

    @pl.when((b == 0) & (i == 0))
    def _():
        um = in_proj(meta_ref[...])
        cvm = um[:, C_A:2 * C_A] * um[:, 2 * C_A:3 * C_A]
        glum = um[:, 3 * C_A:3 * C_A + C_B] * _sigmoid(um[:, 3 * C_A + C_B:])
        for c in range(N_CHUNK):
            sl = slice(c * LANES, (c + 1) * LANES)
            mgb_ref[c, 0:HALO_B - N_META, :] = jnp.zeros((HALO_B - N_META, LANES), F32)
            mgb_ref[c, HALO_B - N_META:HALO_B, :] = glum[:, sl]
            mcv_ref[c] = cvm[N_META - HALO_A:, sl]

    @pl.when(i == 0)
    def _():
        gb_ref[:, 0:HALO_B, :] = mgb_ref[...]
        cv_ref[:, 0:HALO_A, :] = mcv_ref[...]

    u_ref[...] = in_proj(x_ref[...])

    for c in range(N_CHUNK):
        lo = c * LANES
        bg = u_ref[:, lo:lo + LANES]
        cg = u_ref[:, C_A + lo:C_A + lo + LANES]
        v = u_ref[:, 2 * C_A + lo:2 * C_A + lo + LANES]
        ga = u_ref[:, 3 * C_A + lo:3 * C_A + lo + LANES]
        gbv = u_ref[:, 3 * C_A + C_B + lo:3 * C_A + C_B + lo + LANES]
        cv_ref[c, HALO_A:, :] = cg * v
        gb_ref[c, HALO_B:, :] = ga * _sigmoid(gbv)
        acc = cv_ref[c, HALO_A - 2:HALO_A - 2 + t, :] * caw_ref[0:1, lo:lo + LANES]
        for k in range(1, K_A):
            acc = acc + cv_ref[c, HALO_A - 2 + k:HALO_A - 2 + k + t, :] * caw_ref[k:k + 1, lo:lo + LANES]
        y_ref[:, lo:lo + LANES] = (bg * acc).astype(BF16)
        off = HALO_B - (K_B - 1)
        accb = gb_ref[c, off:off + t, :] * cbw_ref[0:1, lo:lo + LANES]
        for k in range(1, K_B):
            accb = accb + gb_ref[c, off + k:off + k + t, :] * cbw_ref[k:k + 1, lo:lo + LANES]
        bc_ref[:, lo:lo + LANES] = accb + cbb_ref[:, lo:lo + LANES]

    rb = 64
    for r in range(t // rb):
        xb = bc_ref[r * rb:(r + 1) * rb, :]
        mu = jnp.mean(xb, axis=-1, keepdims=True)
        var = jnp.mean(jnp.square(xb - mu), axis=-1, keepdims=True)
        bn = (xb - mu) * lax.rsqrt(var + LN_EPS) * lng_ref[...] + lnb_ref[...]
        y_ref[r * rb:(r + 1) * rb, C_A:] = (bn * _sigmoid(bn)).astype(BF16)

    @pl.when(i == n_i - 1)
    def _():
        for c in range(N_CHUNK):
            sl = slice(c * LANES, (c + 1) * LANES)
            pa_ref[:, sl] = cv_ref[c, HALO_A + t - (K_A - 1):HALO_A + t, :]
            pb_ref[:, sl] = gb_ref[c, HALO_B + t - (K_B - 1):HALO_B + t, :]

    gb_ref[:, 0:HALO_B, :] = gb_ref[:, t:t + HALO_B, :]
    cv_ref[:, 0:HALO_A, :] = cv_ref[:, t:t + HALO_A, :]


def _mixer_prompt(x_prompt, meta, g1, win_bf, caw, cbw, cbb, lng, lnb):
    bp, seq, d = x_prompt.shape
    n_i = seq // T_MIX
    full = lambda shape: pl.BlockSpec(shape, lambda b, i: (0,) * len(shape))
    return pl.pallas_call(
        _mixer_prompt_kernel,
        grid=(bp, n_i),
        in_specs=[
            pl.BlockSpec((None, T_MIX, d), lambda b, i: (b, i, 0)),
            full((N_META, d)),
            full((1, d)),
            pl.BlockSpec(win_bf.shape, lambda b, i: (0, 0), pipeline_mode=pl.Buffered(1)),
            full((K_A, C_A)),
            full((K_B, C_B)),
            full((1, C_B)),
            full((1, C_B)),
            full((1, C_B)),
        ],
        out_specs=[
            pl.BlockSpec((T_MIX, d), lambda b, i: (b * n_i + i, 0)),
            pl.BlockSpec((None, K_A - 1, C_A), lambda b, i: (b, 0, 0)),
            pl.BlockSpec((None, K_B - 1, C_B), lambda b, i: (b, 0, 0)),
        ],
        out_shape=[
            jax.ShapeDtypeStruct((bp * seq, d), BF16),
            jax.ShapeDtypeStruct((bp, K_A - 1, C_A), F32),
            jax.ShapeDtypeStruct((bp, K_B - 1, C_B), F32),
        ],
        scratch_shapes=[
            pltpu.VMEM((T_MIX, win_bf.shape[1]), F32),
            pltpu.VMEM((N_CHUNK, T_MIX + HALO_B, LANES), F32),
            pltpu.VMEM((N_CHUNK, T_MIX + HALO_A, LANES), F32),
            pltpu.VMEM((N_CHUNK, HALO_B, LANES), F32),
            pltpu.VMEM((N_CHUNK, HALO_A, LANES), F32),
            pltpu.VMEM((T_MIX, C_B), F32),
        ],
        compiler_params=pltpu.CompilerParams(
            dimension_semantics=("arbitrary", "arbitrary"), vmem_limit_bytes=VMEM_LIMIT),
        name="mixer_prompt",
    )(x_prompt, meta, g1, win_bf, caw, cbw, cbb, lng, lnb)


def _mixer_sample_kernel(x_ref, g1_ref, wbg_ref, wcg_ref, wv_ref, wga_ref, wgb_ref, sa_ref, sb_ref,
                         caw_ref, cbw_ref, cbb_ref, lng_ref, lnb_ref,
                         y_ref, na_ref, glu_ref,
                         h_ref, ya_ref, bc_ref, *, n_seq, n_t):
    c = pl.program_id(0)

    @pl.when(c == 0)
    def _():
        h_ref[...] = _rms_rows(x_ref[...], g1_ref[...]).astype(BF16)

    @pl.when(c < N_CHUNK)
    def _():
        h = h_ref[...]
        proj = lambda w_ref: jnp.dot(h, w_ref[...], preferred_element_type=F32)
        bg, cg, v, ga, gbv = proj(wbg_ref), proj(wcg_ref), proj(wv_ref), proj(wga_ref), proj(wgb_ref)
        cv = cg * v
        glu = ga * _sigmoid(gbv)
        row = lambda a, tt: a[tt * n_seq:(tt + 1) * n_seq, :]
        xa = [sa_ref[j] for j in range(K_A - 1)] + [row(cv, tt) for tt in range(n_t)]
        xb = [sb_ref[j] for j in range(K_B - 1)] + [row(glu, tt) for tt in range(n_t)]
        for tt in range(n_t):
            acc = xa[tt] * caw_ref[0:1, :]
            for k in range(1, K_A):
                acc = acc + xa[tt + k] * caw_ref[k:k + 1, :]
            ya_ref[c, tt * n_seq:(tt + 1) * n_seq, :] = row(bg, tt) * acc
            accb = xb[tt] * cbw_ref[0:1, :]
            for k in range(1, K_B):
                accb = accb + xb[tt + k] * cbw_ref[k:k + 1, :]
            bc_ref[c, tt * n_seq:(tt + 1) * n_seq, :] = accb + cbb_ref[...]
            glu_ref[tt] = row(glu, tt)
        for j in range(K_A - 1):
            na_ref[j] = row(cv, n_t - (K_A - 1) + j)

    @pl.when(c == N_CHUNK)
    def _():
        xb = jnp.concatenate([bc_ref[cc] for cc in range(N_CHUNK)], axis=1)
        mu = jnp.mean(xb, axis=-1, keepdims=True)
        var = jnp.mean(jnp.square(xb - mu), axis=-1, keepdims=True)
        bn = (xb - mu) * lax.rsqrt(var + LN_EPS) * lng_ref[...] + lnb_ref[...]
        for cc in range(N_CHUNK):
            y_ref[:, cc * LANES:(cc + 1) * LANES] = ya_ref[cc].astype(BF16)
        y_ref[:, C_A:] = (bn * _sigmoid(bn)).astype(BF16)


def _mixer_sample(xs_t, g1, win_bf, sa_t, sb_t, caw, cbw, cbb, lng, lnb, n_seq, n_t):
    rows, d = xs_t.shape
    cc = lambda c: jnp.minimum(c, N_CHUNK - 1)
    wspec = lambda g: pl.BlockSpec((d, LANES), lambda c, g=g: (0, g * N_CHUNK + cc(c)))
    full = lambda shape: pl.BlockSpec(shape, lambda c: (0,) * len(shape))
    kern = functools.partial(_mixer_sample_kernel, n_seq=n_seq, n_t=n_t)
    return pl.pallas_call(
        kern,
        grid=(N_CHUNK + 1,),
        in_specs=[
            full((rows, d)),
            full((1, d)),
            wspec(0), wspec(1), wspec(2), wspec(3), wspec(4),
            pl.BlockSpec((K_A - 1, n_seq, LANES), lambda c: (0, 0, cc(c))),
            pl.BlockSpec((K_B - 1, n_seq, LANES), lambda c: (0, 0, cc(c))),
            pl.BlockSpec((K_A, LANES), lambda c: (0, cc(c))),
            pl.BlockSpec((K_B, LANES), lambda c: (0, cc(c))),
            pl.BlockSpec((1, LANES), lambda c: (0, cc(c))),
            full((1, C_B)),
            full((1, C_B)),
        ],
        out_specs=[
            full((rows, d)),
            pl.BlockSpec((K_A - 1, n_seq, LANES), lambda c: (0, 0, cc(c))),
            pl.BlockSpec((n_t, n_seq, LANES), lambda c: (0, 0, cc(c))),
        ],
        out_shape=[
            jax.ShapeDtypeStruct((rows, d), BF16),
            jax.ShapeDtypeStruct((K_A - 1, n_seq, C_A), F32),
            jax.ShapeDtypeStruct((n_t, n_seq, C_B), F32),
        ],
        scratch_shapes=[
            pltpu.VMEM((rows, d), BF16),
            pltpu.VMEM((N_CHUNK, rows, LANES), F32),
            pltpu.VMEM((N_CHUNK, rows, LANES), F32),
        ],
        compiler_params=pltpu.CompilerParams(
            dimension_semantics=("arbitrary",), vmem_limit_bytes=VMEM_LIMIT),
        name="mixer_sample",
    )(xs_t, g1, win_bf, win_bf, win_bf, win_bf, win_bf, sa_t, sb_t, caw, cbw, cbb, lng, lnb)


def _post_mixer_kernel(yp_ref, ys_ref, xp_ref, xs_ref, wout_ref, g2_ref, wr_ref, br_ref,
                       x1_ref, xn_ref, idx_ref, gate_ref, rank_ref, cnt_ref,
                       run_ref, *, n_prompt_tiles):
    i = pl.program_id(0)
    tm = yp_ref.shape[0]

    @pl.when(i == 0)
    def _():
        run_ref[...] = jnp.zeros(run_ref.shape, F32)

    is_prompt = i < n_prompt_tiles
    x = jnp.where(is_prompt, xp_ref[...], xs_ref[...])
    y = jnp.where(is_prompt, yp_ref[...], ys_ref[...])
    x1 = x + jnp.dot(y, wout_ref[...], preferred_element_type=F32)
    x1_ref[...] = x1
    xn = _rms_rows(x1, g2_ref[...])
    _store_row_tiles(xn_ref, 0, _pack_bf16_pairs(xn))

    logits = lax.dot_general(wr_ref[...], xn, (((1,), (1,)), ((), ())),
                             precision=lax.Precision.HIGHEST, preferred_element_type=F32) + br_ref[...]
    eidx = lax.broadcasted_iota(jnp.int32, logits.shape, 0)
    work = logits
    vals, sels, hots = [], [], []
    for _ in range(TOP_K):
        m = jnp.max(work, axis=0, keepdims=True)
        sel = jnp.min(jnp.where(work == m, eidx, N_EXPERTS), axis=0, keepdims=True)
        hot = eidx == sel
        vals.append(m)
        sels.append(sel)
        hots.append(hot)
        work = jnp.where(hot, -jnp.inf, work)
    exps = [jnp.exp(v - vals[0]) for v in vals]
    denom = exps[0] + exps[1] + exps[2] + exps[3]
    for k in range(TOP_K):
        idx_ref[k:k + 1, :] = sels[k]
        gate_ref[k:k + 1, :] = exps[k] / denom

    chosen = (hots[0] | hots[1] | hots[2] | hots[3])
    chosen_bf = chosen.astype(F32).astype(BF16)
    s_io = lax.broadcasted_iota(jnp.int32, (tm, tm), 0)
    t_io = lax.broadcasted_iota(jnp.int32, (tm, tm), 1)
    upper = (s_io < t_io).astype(F32).astype(BF16)
    before = jnp.dot(chosen_bf, upper, preferred_element_type=F32) + run_ref[:, 0:1]
    for k in range(TOP_K):
        r = jnp.sum(jnp.where(hots[k], before, 0.0), axis=0, keepdims=True)
        rank_ref[k:k + 1, :] = r.astype(jnp.int32)
    run_ref[...] = run_ref[...] + jnp.sum(chosen.astype(F32), axis=1, keepdims=True)
    cnt_ref[...] = run_ref[...].astype(jnp.int32)


def _post_mixer(yp, ys, xp2, xs_t, wout_bf, g2, wr_t, br_col):
    d = yp.shape[1]
    n = yp.shape[0] + ys.shape[0]
    tm = TM_POST
    n_tiles = n // tm
    n_pt = xp2.shape[0] // tm
    kern = functools.partial(_post_mixer_kernel, n_prompt_tiles=n_pt)
    full = lambda shape: pl.BlockSpec(shape, lambda i: (0,) * len(shape))
    return pl.pallas_call(
        kern,
        grid=(n_tiles,),
        in_specs=[
            pl.BlockSpec((tm, d), lambda i: (jnp.minimum(i, n_pt - 1), 0)),
            pl.BlockSpec((tm, d), lambda i: (jnp.maximum(i - n_pt, 0), 0)),
            pl.BlockSpec((tm, d), lambda i: (jnp.minimum(i, n_pt - 1), 0)),
            pl.BlockSpec((tm, d), lambda i: (jnp.maximum(i - n_pt, 0), 0)),
            pl.BlockSpec(wout_bf.shape, lambda i: (0, 0), pipeline_mode=pl.Buffered(1)),
            full((1, d)),
            full((N_EXPERTS, d)),
            full((N_EXPERTS, 1)),
        ],
        out_specs=[
            pl.BlockSpec((tm, d), lambda i: (i, 0)),
            pl.BlockSpec((tm * ROW_SUB, LANES), lambda i: (i, 0)),
            pl.BlockSpec((TOP_K, tm), lambda i: (0, i)),
            pl.BlockSpec((TOP_K, tm), lambda i: (0, i)),
            pl.BlockSpec((TOP_K, tm), lambda i: (0, i)),
            full((N_EXPERTS, LANES)),
        ],
        out_shape=[
            jax.ShapeDtypeStruct((n, d), F32),
            jax.ShapeDtypeStruct((n * ROW_SUB, LANES), jnp.uint32),
            jax.ShapeDtypeStruct((TOP_K, n), jnp.int32),
            jax.ShapeDtypeStruct((TOP_K, n), F32),
            jax.ShapeDtypeStruct((TOP_K, n), jnp.int32),
            jax.ShapeDtypeStruct((N_EXPERTS, LANES), jnp.int32),
        ],
        scratch_shapes=[pltpu.VMEM((N_EXPERTS, LANES), F32)],
        compiler_params=pltpu.CompilerParams(
            dimension_semantics=("arbitrary",), vmem_limit_bytes=VMEM_LIMIT),
        name="post_mixer",
    )(yp, ys, xp2, xs_t, wout_bf, g2, wr_t, br_col)


def _row_tile(ref, row):
    return ref.at[pl.ds(pl.multiple_of(row * ROW_SUB, ROW_SUB), ROW_SUB)]


def _dispatch_kernel(pos_ref, zf_ref, x_ref, xs_hbm, zbuf, sem, zsem):
    i = pl.program_id(0)
    tc = x_ref.shape[0] // ROW_SUB
    zrows = zbuf.shape[0]
    n_tiles = xs_hbm.shape[0] // zrows

    def zero_copy(t):
        return pltpu.make_async_copy(zbuf, xs_hbm.at[pl.ds(pl.multiple_of(t * zrows, zrows), zrows)], zsem)

    @pl.when(i == 0)
    def _():
        zbuf[...] = jnp.zeros(zbuf.shape, zbuf.dtype)

        def start(t, carry):
            @pl.when(zf_ref[t] != 0)
            def _():
                zero_copy(t).start()
            return carry
        lax.fori_loop(0, n_tiles, start, 0)

        def wait(t, carry):
            @pl.when(zf_ref[t] != 0)
            def _():
                zero_copy(t).wait()
            return carry
        lax.fori_loop(0, n_tiles, wait, 0)

    base = i * (TOP_K * tc)

    def body(g, carry):
        r0 = g * ISSUE_UNROLL
        for u in range(ISSUE_UNROLL):
            for k in range(TOP_K):
                p = pos_ref[base + k * tc + r0 + u]
                pltpu.make_async_copy(_row_tile(x_ref, r0 + u), _row_tile(xs_hbm, p), sem).start(
                    priority=(u * TOP_K + k) % 2)
        return carry
    lax.fori_loop(0, tc // ISSUE_UNROLL, body, 0)
    for k in range(TOP_K):
        pltpu.make_async_copy(x_ref, xs_hbm.at[pl.ds(0, tc * ROW_SUB)], sem).wait()


def _dispatch(pos_tiles, zero_flags, xn_tiles, p_max):
    n = xn_tiles.shape[0] // ROW_SUB
    tc = TC_DISP
    return pl.pallas_call(
        _dispatch_kernel,
        grid_spec=pltpu.PrefetchScalarGridSpec(
            num_scalar_prefetch=2,
            grid=(n // tc,),
            in_specs=[pl.BlockSpec((tc * ROW_SUB, LANES), lambda i, p, z: (i, 0))],
            out_specs=pl.BlockSpec(memory_space=pl.ANY),
            scratch_shapes=[pltpu.VMEM((TM_MOE * ROW_SUB, LANES), jnp.uint32), pltpu.SemaphoreType.DMA(()),
                            pltpu.SemaphoreType.DMA(())],
        ),
        out_shape=jax.ShapeDtypeStruct((p_max * ROW_SUB, LANES), jnp.uint32),
        compiler_params=pltpu.CompilerParams(
            dimension_semantics=("arbitrary",), vmem_limit_bytes=VMEM_LIMIT),
        name="dispatch",
    )(pos_tiles, zero_flags, xn_tiles)


_FLAG_NEW_WEIGHTS = 1
WEIGHT_DMA_PRIORITY = 1


def _advance_weights(te_ref, ne_ref, fl_ref, slot_ref, copies):
    j = pl.program_id(0)
    t = pl.program_id(1)
    n_j = pl.num_programs(0)

    @pl.when((fl_ref[t] & _FLAG_NEW_WEIGHTS) != 0)
    def _():
        first = (j == 0) & (t == 0)

        @pl.when(first)
        def _():
            slot_ref[0] = 0
            for c in copies(te_ref[t], j, 0):
                c.start(priority=WEIGHT_DMA_PRIORITY)

        @pl.when(jnp.logical_not(first))
        def _():
            slot_ref[0] = 1 - slot_ref[0]

        s = slot_ref[0]
        for c in copies(te_ref[t], j, s):
            c.wait()
        ne = ne_ref[t]

        @pl.when(ne >= 0)
        def _():
            for c in copies(ne, j, 1 - s):
                c.start(priority=WEIGHT_DMA_PRIORITY)

        @pl.when((ne < 0) & (j + 1 < n_j))
        def _():
            for c in copies(te_ref[0], j + 1, 1 - s):
                c.start(priority=WEIGHT_DMA_PRIORITY)


def _by_valid_rows(n_valid, compute, zero_from):
    for rows in range(0, TM_MOE + 1, SUB_MOE):
        @pl.when((n_valid > rows - SUB_MOE) & (n_valid <= rows))
        def _():
            if rows > 0:
                compute(rows)
            if rows < TM_MOE:
                zero_from(rows)


def _bf16_dot(x_bf, w_f32):
    return lax.dot_general(x_bf, w_f32, (((1,), (0,)), ((), ())), preferred_element_type=F32)


def _moe_up_kernel(te_ref, tb_ref, fl_ref, ne_ref, nv_ref, x_ref, bg_ref, bu_ref, w_hbm, h_ref, wbuf, sem,
                   slot_ref):
    t = pl.program_id(1)
    n_valid = nv_ref[t]

    def copies(e, j, slot):
        cg = pl.multiple_of(j * BF_UP, BF_UP)
        cu = pl.multiple_of(D_FF + j * BF_UP, BF_UP)
        return (pltpu.make_async_copy(w_hbm.at[e, :, pl.ds(cg, BF_UP)], wbuf.at[slot, 0], sem.at[slot]),
                pltpu.make_async_copy(w_hbm.at[e, :, pl.ds(cu, BF_UP)], wbuf.at[slot, 1], sem.at[slot]))

    _advance_weights(te_ref, ne_ref, fl_ref, slot_ref, copies)

    def compute(n_rows):
        s = slot_ref[0]
        halves = [_unpack_hi_lo(c) for c in _load_row_tiles(x_ref, 0, n_rows)]
        x = jnp.concatenate([hl[0].astype(BF16) for hl in halves] + [hl[1].astype(BF16) for hl in halves], axis=1)
        gate = _bf16_dot(x, wbuf[s, 0]) + bg_ref[...]
        up = _bf16_dot(x, wbuf[s, 1]) + bu_ref[...]
        gate = jnp.minimum(gate, SWIGLU_LIMIT)
        up = jnp.clip(up, -SWIGLU_LIMIT, SWIGLU_LIMIT)
        act = gate * _sigmoid(SWIGLU_ALPHA * gate) * (up + 1.0)
        h_ref[0:n_rows, :] = act.astype(BF16)

    def zero_from(r0):
        h_ref[r0:TM_MOE, :] = jnp.zeros((TM_MOE - r0, h_ref.shape[1]), BF16)

    _by_valid_rows(n_valid, compute, zero_from)


def _moe_up(te, tb, fl, ne, nv, xs, w_gate_up, b_gate_up3):
    p_max = xs.shape[0] // ROW_SUB
    d = D_MODEL
    tm = TM_MOE
    n_tiles = p_max // tm
    n_j = D_FF // BF_UP
    return pl.pallas_call(
        _moe_up_kernel,
        grid_spec=pltpu.PrefetchScalarGridSpec(
            num_scalar_prefetch=5,
            grid=(n_j, n_tiles),
            in_specs=[
                pl.BlockSpec((tm * ROW_SUB, LANES), lambda j, t, te, tb, fl, ne, nv: (tb[t], 0)),
                pl.BlockSpec((None, 1, BF_UP), lambda j, t, te, tb, fl, ne, nv: (te[t], 0, j)),
                pl.BlockSpec((None, 1, BF_UP), lambda j, t, te, tb, fl, ne, nv: (te[t], 0, n_j + j)),
                pl.BlockSpec(memory_space=pl.ANY),
            ],
            out_specs=pl.BlockSpec((tm, BF_UP), lambda j, t, te, tb, fl, ne, nv: (t, j)),
            scratch_shapes=[pltpu.VMEM((2, 2, d, BF_UP), F32), pltpu.SemaphoreType.DMA((2,)),
                            pltpu.SMEM((1,), jnp.int32)],
        ),
        out_shape=jax.ShapeDtypeStruct((p_max, D_FF), BF16),
        compiler_params=pltpu.CompilerParams(
            dimension_semantics=("arbitrary", "arbitrary"), vmem_limit_bytes=VMEM_LIMIT),
        name="moe_up",
    )(te, tb, fl, ne, nv, xs, b_gate_up3, b_gate_up3, w_gate_up)


def _moe_down_kernel(te_ref, tb_ref, fl_ref, ne_ref, nv_ref, h_ref, bd_ref, w_hbm, y_ref, wbuf, sem, slot_ref):
    t = pl.program_id(1)
    n_valid = nv_ref[t]

    def copies(e, j, slot):
        c0 = pl.multiple_of(j * BN_DOWN, BN_DOWN)
        return (pltpu.make_async_copy(w_hbm.at[e, :, pl.ds(c0, BN_DOWN)], wbuf.at[slot], sem.at[slot]),)

    _advance_weights(te_ref, ne_ref, fl_ref, slot_ref, copies)

    def compute(n_rows):
        y = _bf16_dot(h_ref[0:n_rows, :], wbuf[slot_ref[0]]) + bd_ref[...]
        _store_row_tiles(y_ref, 0, _pack_bf16_pairs(y))

    def zero_from(r0):
        y_ref[r0 * ROW_SUB:TM_MOE * ROW_SUB, :] = jnp.zeros(((TM_MOE - r0) * ROW_SUB, LANES), y_ref.dtype)

    _by_valid_rows(n_valid, compute, zero_from)


def _moe_down(te, tb, fl, ne, nv, h, w_down, b_down3):
    p_max, f = h.shape
    tm = TM_MOE
    n_tiles = p_max // tm
    n_j = D_MODEL // BN_DOWN
    return pl.pallas_call(
        _moe_down_kernel,
        grid_spec=pltpu.PrefetchScalarGridSpec(
            num_scalar_prefetch=5,
            grid=(n_j, n_tiles),
            in_specs=[
                pl.BlockSpec((tm, f), lambda j, t, te, tb, fl, ne, nv: (tb[t], 0)),
                pl.BlockSpec((None, 1, BN_DOWN), lambda j, t, te, tb, fl, ne, nv: (te[t], 0, j)),
                pl.BlockSpec(memory_space=pl.ANY),
            ],
            out_specs=pl.BlockSpec((tm * ROW_SUB, LANES), lambda j, t, te, tb, fl, ne, nv: (t, 0)),
            scratch_shapes=[pltpu.VMEM((2, f, BN_DOWN), F32), pltpu.SemaphoreType.DMA((2,)),
                            pltpu.SMEM((1,), jnp.int32)],
        ),
        out_shape=jax.ShapeDtypeStruct((p_max * ROW_SUB, LANES), jnp.uint32),
        compiler_params=pltpu.CompilerParams(
            dimension_semantics=("arbitrary", "arbitrary"), vmem_limit_bytes=VMEM_LIMIT),
        name="moe_down",
    )(te, tb, fl, ne, nv, h, b_down3, w_down)


def _combine_kernel(pos_ref, x1_ref, gate_ref, gf_ref, y_hbm, op_ref, os_ref, buf, sem, *, n_prompt_tiles):
    i = pl.program_id(0)
    n_i = pl.num_programs(0)
    tc = x1_ref.shape[0]
    rows = TOP_K * tc

    def issue(tile, slot):
        base = tile * rows

        def body(g, carry):
            r0 = g * ISSUE_UNROLL
            for u in range(ISSUE_UNROLL):
                p = pos_ref[base + r0 + u]
                pltpu.make_async_copy(_row_tile(y_hbm, p), _row_tile(buf.at[slot], r0 + u),
                                      sem.at[slot]).start(priority=u % 2)
            return carry
        lax.fori_loop(0, rows // ISSUE_UNROLL, body, 0)

    @pl.when(i == 0)
    def _():
        issue(0, 0)

    @pl.when(i + 1 < n_i)
    def _():
        issue(i + 1, (i + 1) % 2)

    slot = i % 2
    pltpu.make_async_copy(y_hbm.at[pl.ds(0, rows * ROW_SUB)], buf.at[slot], sem.at[slot]).wait()
    his = [None] * ROW_SUB
    los = [None] * ROW_SUB
    for k in range(TOP_K):
        g = gate_ref[:, k:k + 1]
        for s, chunk in enumerate(_load_row_tiles(buf.at[slot], k * tc, tc)):
            hi, lo = _unpack_hi_lo(chunk)
            his[s] = g * hi if k == 0 else his[s] + g * hi
            los[s] = g * lo if k == 0 else los[s] + g * lo
    acc = x1_ref[...] + jnp.concatenate(his + los, axis=1)
    out = _rms_rows(acc, gf_ref[...])

    @pl.when(i < n_prompt_tiles)
    def _():
        op_ref[...] = out

    @pl.when(i >= n_prompt_tiles)
    def _():
        os_ref[...] = out


def _combine(pos_flat, x1, gates_nk, gf, y_sorted, n_prompt):
    n, d = x1.shape
    tc = TC_COMB
    n_tiles = n // tc
    n_pt = n_prompt // tc
    kern = functools.partial(_combine_kernel, n_prompt_tiles=n_pt)
    return pl.pallas_call(
        kern,
        grid_spec=pltpu.PrefetchScalarGridSpec(
            num_scalar_prefetch=1,
            grid=(n_tiles,),
            in_specs=[
                pl.BlockSpec((tc, d), lambda i, p: (i, 0)),
                pl.BlockSpec((tc, TOP_K), lambda i, p: (i, 0)),
                pl.BlockSpec((1, d), lambda i, p: (0, 0)),
                pl.BlockSpec(memory_space=pl.ANY),
            ],
            out_specs=[
                pl.BlockSpec((tc, d), lambda i, p: (jnp.minimum(i, n_pt - 1), 0)),
                pl.BlockSpec((tc, d), lambda i, p: (jnp.maximum(i - n_pt, 0), 0)),
            ],
            scratch_shapes=[pltpu.VMEM((2, TOP_K * tc * ROW_SUB, LANES), jnp.uint32),
                            pltpu.SemaphoreType.DMA((2,))],
        ),
        out_shape=[
            jax.ShapeDtypeStruct((n_prompt, d), F32),
            jax.ShapeDtypeStruct((n - n_prompt, d), F32),
        ],
        compiler_params=pltpu.CompilerParams(
            dimension_semantics=("arbitrary",), vmem_limit_bytes=VMEM_LIMIT),
        name="combine",
    )(pos_flat, x1, gates_nk, gf, y_sorted)


def kernel(x_prompt, x_sample, state_conv_a, state_conv_b, meta_tokens, norm1_g, w_in, conv_a_w, conv_b_w,
           conv_b_b, ln_b_g, ln_b_b, w_out, norm2_g, w_router, b_router, w_gate_up, b_gate_up, w_down,
           b_down, final_norm_g):
    bp, seq, d = x_prompt.shape
    n_seq, n_t, _ = x_sample.shape
    n_prompt = bp * seq
    n_sample = n_seq * n_t
    n = n_prompt + n_sample
    assert norm1_g.shape[0] == 1, "single layer"
    assert seq % T_MIX == 0 and n_prompt % TM_POST == 0 and n_sample == TM_POST
    assert n_prompt % n_sample == 0 and n % TC_COMB == 0 and n_prompt % TC_COMB == 0

    g1 = norm1_g[0][None]
    win_bf = w_in[0].astype(BF16)
    wout_bf = w_out[0].astype(BF16)
    caw, cbw = conv_a_w[0], conv_b_w[0]
    cbb, lng, lnb = conv_b_b[0][None], ln_b_g[0][None], ln_b_b[0][None]

    xs_t = jnp.transpose(x_sample, (1, 0, 2)).reshape(n_sample, d)
    sa_t = jnp.transpose(state_conv_a[0], (1, 0, 2))
    sb_t = jnp.transpose(state_conv_b[0], (1, 0, 2))

    ymix_p, pa, pb = _mixer_prompt(x_prompt, meta_tokens, g1, win_bf, caw, cbw, cbb, lng, lnb)
    ymix_s, na_t, glu_t = _mixer_sample(xs_t, g1, win_bf, sa_t, sb_t, caw, cbw, cbb, lng, lnb, n_seq, n_t)

    x1, xn, idx, gates, rank, cnt = _post_mixer(
        ymix_p, ymix_s, x_prompt.reshape(n_prompt, d), xs_t, wout_bf, norm2_g[0][None],
        jnp.transpose(w_router[0]), b_router[0][:, None])

    tm = TM_MOE
    n_assign = n * TOP_K
    n_tiles = (n_assign + N_EXPERTS * (tm - 1)) // tm
    p_max = n_tiles * tm
    counts = cnt[:, 0]
    tiles_per_e = (counts + tm - 1) // tm
    tile_end = jnp.cumsum(tiles_per_e)
    tile_start = tile_end - tiles_per_e
    n_used = tile_end[-1]
    e_ar = jnp.arange(N_EXPERTS, dtype=jnp.int32)
    start_of = jnp.sum(jnp.where(idx[None] == e_ar[:, None, None], tile_start[:, None, None], 0), axis=0)
    pos = (start_of * tm + rank).astype(jnp.int32)
    tid = jnp.arange(n_tiles, dtype=jnp.int32)
    tb = jnp.maximum(jnp.minimum(tid, n_used - 1), 0).astype(jnp.int32)
    expert_of = lambda tile: jnp.minimum(
        jnp.sum((tile_end[None, :] <= tile[:, None]).astype(jnp.int32), axis=1), N_EXPERTS - 1).astype(jnp.int32)
    te = expert_of(tb)
    valid = tid < n_used
    new_w = valid & ((tid == 0) | (te != jnp.roll(te, 1)))
    fl = new_w.astype(jnp.int32) * _FLAG_NEW_WEIGHTS
    mine = te[:, None] == e_ar[None, :]
    rows_left = (jnp.sum(jnp.where(mine, counts[None, :], 0), axis=1)
                 - (tid - jnp.sum(jnp.where(mine, tile_start[None, :], 0), axis=1)) * tm)
    nv = jnp.where(valid, jnp.clip(rows_left, 0, tm), 0).astype(jnp.int32)
    next_tile = jnp.sum(jnp.where(te[:, None] == e_ar[None, :], tile_end[None, :], 0), axis=1)
    ne = jnp.where(next_tile < n_used, expert_of(next_tile), -1).astype(jnp.int32)
    partial = (counts % tm) != 0
    zero_flags = ((tid >= n_used) | jnp.any((tid[:, None] == (tile_end - 1)[None, :]) & partial[None, :],
                                            axis=1)).astype(jnp.int32)

    by_tile = lambda tc: pos.reshape(TOP_K, n // tc, tc).transpose(1, 0, 2).reshape(-1)
    xs_sorted = _dispatch(by_tile(TC_DISP), zero_flags, xn, p_max)
    h = _moe_up(te, tb, fl, ne, nv, xs_sorted, w_gate_up[0], b_gate_up[0][:, None, :])
    y_sorted = _moe_down(te, tb, fl, ne, nv, h, w_down[0], b_down[0][:, None, :])

    yp, ys_t = _combine(by_tile(TC_COMB), x1, jnp.transpose(gates), final_norm_g[None], y_sorted, n_prompt)

    y_prompt = yp.reshape(bp, seq, d)
    y_sample = jnp.transpose(ys_t.reshape(n_t, n_seq, d), (1, 0, 2))
    new_a_prompt = pa[None]
    new_b_prompt = pb[None]
    new_a_sample = jnp.transpose(na_t, (1, 0, 2))[None]
    glu_s = jnp.transpose(glu_t, (1, 0, 2))
    new_b_sample = jnp.concatenate([state_conv_b[0][:, n_t:], glu_s], axis=1)[None]
    return (y_prompt, y_sample, new_a_prompt, new_b_prompt, new_a_sample, new_b_sample)
```

```python
import functools

import jax
import jax.numpy as jnp
from jax import lax
from jax.experimental import pallas as pl
from jax.experimental.pallas import tpu as pltpu

F32 = jnp.float32
BF16 = jnp.bfloat16

D_MODEL = 2048
N_META = 16
C_A = 1024
C_B = 1024
K_A = 3
K_B = 31
N_EXPERTS = 32
TOP_K = 4
D_FF = 2048
SWIGLU_LIMIT = 7.0
SWIGLU_ALPHA = 1.702
RMS_EPS = 1e-5
LN_EPS = 1e-5

LANES = 128
N_CHUNK = C_B // LANES
HALO_B = 32
HALO_A = 8
T_MIX = 256
TM_POST = 512
TM_MOE = 512
SUB_MOE = 128
BF_UP = 1024
BN_DOWN = 2048
ROW_SUB = D_MODEL // 2 // LANES
TC_COMB = 256
TC_DISP = 512
ISSUE_UNROLL = 8
VMEM_LIMIT = 56 * 1024 * 1024


def _sigmoid(x):
    return jax.nn.sigmoid(x)


def _rms_rows(x, g):
    ms = jnp.mean(x * x, axis=-1, keepdims=True)
    return (x * lax.rsqrt(ms + RMS_EPS)) * g


def _pack_bf16_pairs(x):
    c = x.shape[1] // 2
    hi = lax.bitcast_convert_type(x[:, :c].astype(BF16).astype(F32), jnp.uint32)
    lo = lax.bitcast_convert_type(x[:, c:].astype(BF16).astype(F32), jnp.uint32)
    return hi | (lo >> 16)


def _unpack_hi_lo(p):
    hi = lax.bitcast_convert_type(p & jnp.uint32(0xFFFF0000), F32)
    lo = lax.bitcast_convert_type(p << 16, F32)
    return hi, lo


def _store_row_tiles(ref, first_row, packed):
    r = packed.shape[0]
    for s in range(ROW_SUB):
        ref[pl.ds(first_row * ROW_SUB + s, r, stride=ROW_SUB), :] = packed[:, s * LANES:(s + 1) * LANES]


def _load_row_tiles(ref, first_row, r):
    return [ref[pl.ds(first_row * ROW_SUB + s, r, stride=ROW_SUB), :] for s in range(ROW_SUB)]


def _mixer_prompt_kernel(x_ref, meta_ref, g1_ref, win_ref, caw_ref, cbw_ref, cbb_ref, lng_ref, lnb_ref,
                         y_ref, pa_ref, pb_ref,
                         u_ref, gb_ref, cv_ref, mgb_ref, mcv_ref, bc_ref):
    b = pl.program_id(0)
    i = pl.program_id(1)
    n_i = pl.num_programs(1)
    t = T_MIX

    def in_proj(rows):
        h = _rms_rows(rows, g1_ref[...]).astype(BF16)
        return jnp.dot(h, win_ref[...], preferred_element_type=F32)

    @pl.when((b == 0) & (i == 0))
    def _():
        um = in_proj(meta_ref[...])
        cvm = um[:, C_A:2 * C_A] * um[:, 2 * C_A:3 * C_A]
        glum = um[:, 3 * C_A:3 * C_A + C_B] * _sigmoid(um[:, 3 * C_A + C_B:])
        for c in range(N_CHUNK):
            sl = slice(c * LANES, (c + 1) * LANES)
            mgb_ref[c, 0:HALO_B - N_META, :] = jnp.zeros((HALO_B - N_META, LANES), F32)
            mgb_ref[c, HALO_B - N_META:HALO_B, :] = glum[:, sl]
            mcv_ref[c] = cvm[N_META - HALO_A:, sl]

    @pl.when(i == 0)
    def _():
        gb_ref[:, 0:HALO_B, :] = mgb_ref[...]
        cv_ref[:, 0:HALO_A, :] = mcv_ref[...]

    u_ref[...] = in_proj(x_ref[...])

    for c in range(N_CHUNK):
        lo = c * LANES
        bg = u_ref[:, lo:lo + LANES]
        cg = u_ref[:, C_A + lo:C_A + lo + LANES]
        v = u_ref[:, 2 * C_A + lo:2 * C_A + lo + LANES]
        ga = u_ref[:, 3 * C_A + lo:3 * C_A + lo + LANES]
        gbv = u_ref[:, 3 * C_A + C_B + lo:3 * C_A + C_B + lo + LANES]
        cv_ref[c, HALO_A:, :] = cg * v
        gb_ref[c, HALO_B:, :] = ga * _sigmoid(gbv)
        acc = cv_ref[c, HALO_A - 2:HALO_A - 2 + t, :] * caw_ref[0:1, lo:lo + LANES]
        for k in range(1, K_A):
            acc = acc + cv_ref[c, HALO_A - 2 + k:HALO_A - 2 + k + t, :] * caw_ref[k:k + 1, lo:lo + LANES]
        y_ref[:, lo:lo + LANES] = (bg * acc).astype(BF16)
        off = HALO_B - (K_B - 1)
        accb = gb_ref[c, off:off + t, :] * cbw_ref[0:1, lo:lo + LANES]
        for k in range(1, K_B):
            accb = accb + gb_ref[c, off + k:off + k + t, :] * cbw_ref[k:k + 1, lo:lo + LANES]
        bc_ref[:, lo:lo + LANES] = accb + cbb_ref[:, lo:lo + LANES]

    rb = 64
    for r in range(t // rb):
        xb = bc_ref[r * rb:(r + 1) * rb, :]
        mu = jnp.mean(xb, axis=-1, keepdims=True)
        var = jnp.mean(jnp.square(xb - mu), axis=-1, keepdims=True)
        bn = (xb - mu) * lax.rsqrt(var + LN_EPS) * lng_ref[...] + lnb_ref[...]
        y_ref[r * rb:(r + 1) * rb, C_A:] = (bn * _sigmoid(bn)).astype(BF16)

    @pl.when(i == n_i - 1)
    def _():
        for c in range(N_CHUNK):
            sl = slice(c * LANES, (c + 1) * LANES)
            pa_ref[:, sl] = cv_ref[c, HALO_A + t - (K_A - 1):HALO_A + t, :]
            pb_ref[:, sl] = gb_ref[c, HALO_B + t - (K_B - 1):HALO_B + t, :]

    gb_ref[:, 0:HALO_B, :] = gb_ref[:, t:t + HALO_B, :]
    cv_ref[:, 0:HALO_A, :] = cv_ref[:, t:t + HALO_A, :]


def _mixer_prompt(x_prompt, meta, g1, win_bf, caw, cbw, cbb, lng, lnb):
    bp, seq, d = x_prompt.shape
    n_i = seq // T_MIX
    full = lambda shape: pl.BlockSpec(shape, lambda b, i: (0,) * len(shape))
    return pl.pallas_call(
        _mixer_prompt_kernel,
        grid=(bp, n_i),
        in_specs=[
            pl.BlockSpec((None, T_MIX, d), lambda b, i: (b, i, 0)),
            full((N_META, d)),
            full((1, d)),
            pl.BlockSpec(win_bf.shape, lambda b, i: (0, 0), pipeline_mode=pl.Buffered(1)),
            full((K_A, C_A)),
            full((K_B, C_B)),
            full((1, C_B)),
            full((1, C_B)),
            full((1, C_B)),
        ],
        out_specs=[
            pl.BlockSpec((T_MIX, d), lambda b, i: (b * n_i + i, 0)),
            pl.BlockSpec((None, K_A - 1, C_A), lambda b, i: (b, 0, 0)),
            pl.BlockSpec((None, K_B - 1, C_B), lambda b, i: (b, 0, 0)),
        ],
        out_shape=[
            jax.ShapeDtypeStruct((bp * seq, d), BF16),
            jax.ShapeDtypeStruct((bp, K_A - 1, C_A), F32),
            jax.ShapeDtypeStruct((bp, K_B - 1, C_B), F32),
        ],
        scratch_shapes=[
            pltpu.VMEM((T_MIX, win_bf.shape[1]), F32),
            pltpu.VMEM((N_CHUNK, T_MIX + HALO_B, LANES), F32),
            pltpu.VMEM((N_CHUNK, T_MIX + HALO_A, LANES), F32),
            pltpu.VMEM((N_CHUNK, HALO_B, LANES), F32),
            pltpu.VMEM((N_CHUNK, HALO_A, LANES), F32),
            pltpu.VMEM((T_MIX, C_B), F32),
        ],
        compiler_params=pltpu.CompilerParams(
            dimension_semantics=("arbitrary", "arbitrary"), vmem_limit_bytes=VMEM_LIMIT),
        name="mixer_prompt",
    )(x_prompt, meta, g1, win_bf, caw, cbw, cbb, lng, lnb)


def _mixer_sample_kernel(x_ref, g1_ref, wbg_ref, wcg_ref, wv_ref, wga_ref, wgb_ref, sa_ref, sb_ref,
                         caw_ref, cbw_ref, cbb_ref, lng_ref, lnb_ref,
                         y_ref, na_ref, nb_ref,
                         h_ref, ya_ref, bc_ref, *, n_seq, n_t):
    c = pl.program_id(0)

    @pl.when(c == 0)
    def _():
        h_ref[...] = _rms_rows(x_ref[...], g1_ref[...]).astype(BF16)

    @pl.when(c < N_CHUNK)
    def _():
        h = h_ref[...]
        proj = lambda w_ref: jnp.dot(h, w_ref[...], preferred_element_type=F32)
        bg, cg, v, ga, gbv = proj(wbg_ref), proj(wcg_ref), proj(wv_ref), proj(wga_ref), proj(wgb_ref)
        cv = cg * v
        glu = ga * _sigmoid(gbv)
        row = lambda a, tt: a[tt * n_seq:(tt + 1) * n_seq, :]
        xa = [sa_ref[j] for j in range(K_A - 1)] + [row(cv, tt) for tt in range(n_t)]
        xb = [sb_ref[j] for j in range(K_B - 1)] + [row(glu, tt) for tt in range(n_t)]
        for tt in range(n_t):
            acc = xa[tt] * caw_ref[0:1, :]
            for k in range(1, K_A):
                acc = acc + xa[tt + k] * caw_ref[k:k + 1, :]
            ya_ref[c, tt * n_seq:(tt + 1) * n_seq, :] = row(bg, tt) * acc
            accb = xb[tt] * cbw_ref[0:1, :]
            for k in range(1, K_B):
                accb = accb + xb[tt + k] * cbw_ref[k:k + 1, :]
            bc_ref[c, tt * n_seq:(tt + 1) * n_seq, :] = accb + cbb_ref[...]
        for j in range(K_B - 1):
            nb_ref[j] = xb[n_t + j]
        for j in range(K_A - 1):
            na_ref[j] = row(cv, n_t - (K_A - 1) + j)

    @pl.when(c == N_CHUNK)
    def _():
        xb = jnp.concatenate([bc_ref[cc] for cc in range(N_CHUNK)], axis=1)
        mu = jnp.mean(xb, axis=-1, keepdims=True)
        var = jnp.mean(jnp.square(xb - mu), axis=-1, keepdims=True)
        bn = (xb - mu) * lax.rsqrt(var + LN_EPS) * lng_ref[...] + lnb_ref[...]
        for cc in range(N_CHUNK):
            y_ref[:, cc * LANES:(cc + 1) * LANES] = ya_ref[cc].astype(BF16)
        y_ref[:, C_A:] = (bn * _sigmoid(bn)).astype(BF16)


def _mixer_sample(xs_t, g1, win_bf, sa_t, sb_t, caw, cbw, cbb, lng, lnb, n_seq, n_t):
    rows, d = xs_t.shape
    cc = lambda c: jnp.minimum(c, N_CHUNK - 1)
    wspec = lambda g: pl.BlockSpec((d, LANES), lambda c, g=g: (0, g * N_CHUNK + cc(c)))
    full = lambda shape: pl.BlockSpec(shape, lambda c: (0,) * len(shape))
    kern = functools.partial(_mixer_sample_kernel, n_seq=n_seq, n_t=n_t)
    return pl.pallas_call(
        kern,
        grid=(N_CHUNK + 1,),
        in_specs=[
            full((rows, d)),
            full((1, d)),
            wspec(0), wspec(1), wspec(2), wspec(3), wspec(4),
            pl.BlockSpec((K_A - 1, n_seq, LANES), lambda c: (0, 0, cc(c))),
            pl.BlockSpec((K_B - 1, n_seq, LANES), lambda c: (0, 0, cc(c))),
            pl.BlockSpec((K_A, LANES), lambda c: (0, cc(c))),
            pl.BlockSpec((K_B, LANES), lambda c: (0, cc(c))),
            pl.BlockSpec((1, LANES), lambda c: (0, cc(c))),
            full((1, C_B)),
            full((1, C_B)),
        ],
        out_specs=[
            full((rows, d)),
            pl.BlockSpec((K_A - 1, n_seq, LANES), lambda c: (0, 0, cc(c))),
            pl.BlockSpec((K_B - 1, n_seq, LANES), lambda c: (0, 0, cc(c))),
        ],
        out_shape=[
            jax.ShapeDtypeStruct((rows, d), BF16),
            jax.ShapeDtypeStruct((K_A - 1, n_seq, C_A), F32),
            jax.ShapeDtypeStruct((K_B - 1, n_seq, C_B), F32),
        ],
        scratch_shapes=[
            pltpu.VMEM((rows, d), BF16),
            pltpu.VMEM((N_CHUNK, rows, LANES), F32),
            pltpu.VMEM((N_CHUNK, rows, LANES), F32),
        ],
        compiler_params=pltpu.CompilerParams(
            dimension_semantics=("arbitrary",), vmem_limit_bytes=VMEM_LIMIT),
        name="mixer_sample",
    )(xs_t, g1, win_bf, win_bf, win_bf, win_bf, win_bf, sa_t, sb_t, caw, cbw, cbb, lng, lnb)


def _post_mixer_kernel(yp_ref, ys_ref, xp_ref, xs_ref, wout_ref, g2_ref, wr_ref, br_ref,
                       x1_ref, xn_ref, idx_ref, gate_ref, rank_ref, cnt_ref,
                       run_ref, tri_ref, *, n_prompt_tiles):
    i = pl.program_id(0)
    tm = yp_ref.shape[0]

    @pl.when(i == 0)
    def _():
        run_ref[...] = jnp.zeros(run_ref.shape, F32)
        s_io = lax.broadcasted_iota(jnp.int32, (tm, tm), 0)
        t_io = lax.broadcasted_iota(jnp.int32, (tm, tm), 1)
        tri_ref[...] = (s_io < t_io).astype(F32).astype(BF16)

    is_prompt = i < n_prompt_tiles
    x = jnp.where(is_prompt, xp_ref[...], xs_ref[...])
    y = jnp.where(is_prompt, yp_ref[...], ys_ref[...])
    x1 = x + jnp.dot(y, wout_ref[...], preferred_element_type=F32)
    x1_ref[...] = x1
    xn = _rms_rows(x1, g2_ref[...])
    _store_row_tiles(xn_ref, 0, _pack_bf16_pairs(xn))

    logits = lax.dot_general(wr_ref[...], xn, (((1,), (1,)), ((), ())),
                             precision=lax.Precision.HIGHEST, preferred_element_type=F32) + br_ref[...]
    eidx = lax.broadcasted_iota(jnp.int32, logits.shape, 0)
    work = logits
    vals, sels, hots = [], [], []
    for _ in range(TOP_K):
        m = jnp.max(work, axis=0, keepdims=True)
        sel = jnp.min(jnp.where(work == m, eidx, N_EXPERTS), axis=0, keepdims=True)
        hot = eidx == sel
        vals.append(m)
        sels.append(sel)
        hots.append(hot)
        work = jnp.where(hot, -jnp.inf, work)
    exps = [jnp.exp(v - vals[0]) for v in vals]
    denom = exps[0] + exps[1] + exps[2] + exps[3]
    for k in range(TOP_K):
        idx_ref[k:k + 1, :] = sels[k]
        gate_ref[k:k + 1, :] = exps[k] / denom

    chosen = (hots[0] | hots[1] | hots[2] | hots[3])
    chosen_bf = chosen.astype(F32).astype(BF16)
    before = jnp.dot(chosen_bf, tri_ref[...], preferred_element_type=F32) + run_ref[:, 0:1]
    for k in range(TOP_K):
        r = jnp.sum(jnp.where(hots[k], before, 0.0), axis=0, keepdims=True)
        rank_ref[k:k + 1, :] = r.astype(jnp.int32)
    run_ref[...] = run_ref[...] + jnp.sum(chosen.astype(F32), axis=1, keepdims=True)
    cnt_ref[...] = run_ref[...].astype(jnp.int32)


def _post_mixer(yp, ys, xp2, xs_t, wout_bf, g2, wr_t, br_col):
    d = yp.shape[1]
    n = yp.shape[0] + ys.shape[0]
    tm = TM_POST
    n_tiles = n // tm
    n_pt = xp2.shape[0] // tm
    kern = functools.partial(_post_mixer_kernel, n_prompt_tiles=n_pt)
    full = lambda shape: pl.BlockSpec(shape, lambda i: (0,) * len(shape))
    return pl.pallas_call(
        kern,
        grid=(n_tiles,),
        in_specs=[
            pl.BlockSpec((tm, d), lambda i: (jnp.minimum(i, n_pt - 1), 0)),
            pl.BlockSpec((tm, d), lambda i: (jnp.maximum(i - n_pt, 0), 0)),
            pl.BlockSpec((tm, d), lambda i: (jnp.minimum(i, n_pt - 1), 0)),
            pl.BlockSpec((tm, d), lambda i: (jnp.maximum(i - n_pt, 0), 0)),
            pl.BlockSpec(wout_bf.shape, lambda i: (0, 0), pipeline_mode=pl.Buffered(1)),
            full((1, d)),
            full((N_EXPERTS, d)),
            full((N_EXPERTS, 1)),
        ],
        out_specs=[
            pl.BlockSpec((tm, d), lambda i: (i, 0)),
            pl.BlockSpec((tm * ROW_SUB, LANES), lambda i: (i, 0)),
            pl.BlockSpec((TOP_K, tm), lambda i: (0, i)),
            pl.BlockSpec((TOP_K, tm), lambda i: (0, i)),
            pl.BlockSpec((TOP_K, tm), lambda i: (0, i)),
            full((N_EXPERTS, LANES)),
        ],
        out_shape=[
            jax.ShapeDtypeStruct((n, d), F32),
            jax.ShapeDtypeStruct((n * ROW_SUB, LANES), jnp.uint32),
            jax.ShapeDtypeStruct((TOP_K, n), jnp.int32),
            jax.ShapeDtypeStruct((TOP_K, n), F32),
            jax.ShapeDtypeStruct((TOP_K, n), jnp.int32),
            jax.ShapeDtypeStruct((N_EXPERTS, LANES), jnp.int32),
        ],
        scratch_shapes=[pltpu.VMEM((N_EXPERTS, LANES), F32), pltpu.VMEM((tm, tm), BF16)],
        compiler_params=pltpu.CompilerParams(
            dimension_semantics=("arbitrary",), vmem_limit_bytes=VMEM_LIMIT),
        name="post_mixer",
    )(yp, ys, xp2, xs_t, wout_bf, g2, wr_t, br_col)


def _row_tile(ref, row):
    return ref.at[pl.ds(pl.multiple_of(row * ROW_SUB, ROW_SUB), ROW_SUB)]


def _dispatch_kernel(pos_ref, zf_ref, x_ref, xs_hbm, zbuf, sem, zsem):
    i = pl.program_id(0)
    tc = x_ref.shape[0] // ROW_SUB
    zrows = zbuf.shape[0]
    n_tiles = xs_hbm.shape[0] // zrows

    def zero_copy(t):
        return pltpu.make_async_copy(zbuf, xs_hbm.at[pl.ds(pl.multiple_of(t * zrows, zrows), zrows)], zsem)

    @pl.when(i == 0)
    def _():
        zbuf[...] = jnp.zeros(zbuf.shape, zbuf.dtype)

        def start(t, carry):
            @pl.when(zf_ref[t] != 0)
            def _():
                zero_copy(t).start()
            return carry
        lax.fori_loop(0, n_tiles, start, 0)

        def wait(t, carry):
            @pl.when(zf_ref[t] != 0)
            def _():
                zero_copy(t).wait()
            return carry
        lax.fori_loop(0, n_tiles, wait, 0)

    base = i * (TOP_K * tc)

    def body(g, carry):
        r0 = g * ISSUE_UNROLL
        for u in range(ISSUE_UNROLL):
            for k in range(TOP_K):
                p = pos_ref[base + k * tc + r0 + u]
                pltpu.make_async_copy(_row_tile(x_ref, r0 + u), _row_tile(xs_hbm, p), sem).start(
                    priority=(u * TOP_K + k) % 2)
        return carry
    lax.fori_loop(0, tc // ISSUE_UNROLL, body, 0)
    for k in range(TOP_K):
        pltpu.make_async_copy(x_ref, xs_hbm.at[pl.ds(0, tc * ROW_SUB)], sem).wait()


def _dispatch(pos_tiles, zero_flags, xn_tiles, p_max):
    n = xn_tiles.shape[0] // ROW_SUB
    tc = TC_DISP
    return pl.pallas_call(
        _dispatch_kernel,
        grid_spec=pltpu.PrefetchScalarGridSpec(
            num_scalar_prefetch=2,
            grid=(n // tc,),
            in_specs=[pl.BlockSpec((tc * ROW_SUB, LANES), lambda i, p, z: (i, 0))],
            out_specs=pl.BlockSpec(memory_space=pl.ANY),
            scratch_shapes=[pltpu.VMEM((TM_MOE * ROW_SUB, LANES), jnp.uint32), pltpu.SemaphoreType.DMA(()),
                            pltpu.SemaphoreType.DMA(())],
        ),
        out_shape=jax.ShapeDtypeStruct((p_max * ROW_SUB, LANES), jnp.uint32),
        compiler_params=pltpu.CompilerParams(
            dimension_semantics=("arbitrary",), vmem_limit_bytes=VMEM_LIMIT),
        name="dispatch",
    )(pos_tiles, zero_flags, xn_tiles)


_FLAG_NEW_WEIGHTS = 1
WEIGHT_DMA_PRIORITY = 1


def _advance_weights(te_ref, ne_ref, fl_ref, slot_ref, copies):
    j = pl.program_id(0)
    t = pl.program_id(1)
    n_j = pl.num_programs(0)

    @pl.when((fl_ref[t] & _FLAG_NEW_WEIGHTS) != 0)
    def _():
        first = (j == 0) & (t == 0)

        @pl.when(first)
        def _():
            slot_ref[0] = 0
            for c in copies(te_ref[t], j, 0):
                c.start(priority=WEIGHT_DMA_PRIORITY)

        @pl.when(jnp.logical_not(first))
        def _():
            slot_ref[0] = 1 - slot_ref[0]

        s = slot_ref[0]
        for c in copies(te_ref[t], j, s):
            c.wait()
        ne = ne_ref[t]

        @pl.when(ne >= 0)
        def _():
            for c in copies(ne, j, 1 - s):
                c.start(priority=WEIGHT_DMA_PRIORITY)

        @pl.when((ne < 0) & (j + 1 < n_j))
        def _():
            for c in copies(te_ref[0], j + 1, 1 - s):
                c.start(priority=WEIGHT_DMA_PRIORITY)


def _by_valid_rows(n_valid, compute, zero_from):
    for rows in range(0, TM_MOE + 1, SUB_MOE):
        @pl.when((n_valid > rows - SUB_MOE) & (n_valid <= rows))
        def _():
            if rows > 0:
                compute(rows)
            if rows < TM_MOE:
                zero_from(rows)


def _bf16_dot(x_bf, w_f32):
    return lax.dot_general(x_bf, w_f32, (((1,), (0,)), ((), ())), preferred_element_type=F32)


def _moe_up_kernel(te_ref, tb_ref, fl_ref, ne_ref, nv_ref, x_ref, bg_ref, bu_ref, w_hbm, h_ref, wbuf, sem,
                   slot_ref):
    t = pl.program_id(1)
    n_valid = nv_ref[t]

    def copies(e, j, slot):
        cg = pl.multiple_of(j * BF_UP, BF_UP)
        cu = pl.multiple_of(D_FF + j * BF_UP, BF_UP)
        return (pltpu.make_async_copy(w_hbm.at[e, :, pl.ds(cg, BF_UP)], wbuf.at[slot, 0], sem.at[slot]),
                pltpu.make_async_copy(w_hbm.at[e, :, pl.ds(cu, BF_UP)], wbuf.at[slot, 1], sem.at[slot]))

    _advance_weights(te_ref, ne_ref, fl_ref, slot_ref, copies)

    def compute(n_rows):
        s = slot_ref[0]
        halves = [_unpack_hi_lo(c) for c in _load_row_tiles(x_ref, 0, n_rows)]
        x = jnp.concatenate([hl[0].astype(BF16) for hl in halves] + [hl[1].astype(BF16) for hl in halves], axis=1)
        gate = _bf16_dot(x, wbuf[s, 0]) + bg_ref[...]
        up = _bf16_dot(x, wbuf[s, 1]) + bu_ref[...]
        gate = jnp.minimum(gate, SWIGLU_LIMIT)
        up = jnp.clip(up, -SWIGLU_LIMIT, SWIGLU_LIMIT)
        act = gate * _sigmoid(SWIGLU_ALPHA * gate) * (up + 1.0)
        h_ref[0:n_rows, :] = act.astype(BF16)

    def zero_from(r0):
        h_ref[r0:TM_MOE, :] = jnp.zeros((TM_MOE - r0, h_ref.shape[1]), BF16)

    _by_valid_rows(n_valid, compute, zero_from)


def _moe_up(te, tb, fl, ne, nv, xs, w_gate_up, b_gate_up3):
    p_max = xs.shape[0] // ROW_SUB
    d = D_MODEL
    tm = TM_MOE
    n_tiles = p_max // tm
    n_j = D_FF // BF_UP
    return pl.pallas_call(
        _moe_up_kernel,
        grid_spec=pltpu.PrefetchScalarGridSpec(
            num_scalar_prefetch=5,
            grid=(n_j, n_tiles),
            in_specs=[
                pl.BlockSpec((tm * ROW_SUB, LANES), lambda j, t, te, tb, fl, ne, nv: (tb[t], 0)),
                pl.BlockSpec((None, 1, BF_UP), lambda j, t, te, tb, fl, ne, nv: (te[t], 0, j)),
                pl.BlockSpec((None, 1, BF_UP), lambda j, t, te, tb, fl, ne, nv: (te[t], 0, n_j + j)),
                pl.BlockSpec(memory_space=pl.ANY),
            ],
            out_specs=pl.BlockSpec((tm, BF_UP), lambda j, t, te, tb, fl, ne, nv: (t, j)),
            scratch_shapes=[pltpu.VMEM((2, 2, d, BF_UP), F32), pltpu.SemaphoreType.DMA((2,)),
                            pltpu.SMEM((1,), jnp.int32)],
        ),
        out_shape=jax.ShapeDtypeStruct((p_max, D_FF), BF16),
        compiler_params=pltpu.CompilerParams(
            dimension_semantics=("arbitrary", "arbitrary"), vmem_limit_bytes=VMEM_LIMIT),
        name="moe_up",
    )(te, tb, fl, ne, nv, xs, b_gate_up3, b_gate_up3, w_gate_up)


def _moe_down_kernel(te_ref, tb_ref, fl_ref, ne_ref, nv_ref, h_ref, bd_ref, w_hbm, y_ref, wbuf, sem, slot_ref):
    t = pl.program_id(1)
    n_valid = nv_ref[t]

    def copies(e, j, slot):
        c0 = pl.multiple_of(j * BN_DOWN, BN_DOWN)
        return (pltpu.make_async_copy(w_hbm.at[e, :, pl.ds(c0, BN_DOWN)], wbuf.at[slot], sem.at[slot]),)

    _advance_weights(te_ref, ne_ref, fl_ref, slot_ref, copies)

    def compute(n_rows):
        y = _bf16_dot(h_ref[0:n_rows, :], wbuf[slot_ref[0]]) + bd_ref[...]
        _store_row_tiles(y_ref, 0, _pack_bf16_pairs(y))

    def zero_from(r0):
        y_ref[r0 * ROW_SUB:TM_MOE * ROW_SUB, :] = jnp.zeros(((TM_MOE - r0) * ROW_SUB, LANES), y_ref.dtype)

    _by_valid_rows(n_valid, compute, zero_from)


def _moe_down(te, tb, fl, ne, nv, h, w_down, b_down3):
    p_max, f = h.shape
    tm = TM_MOE
    n_tiles = p_max // tm
    n_j = D_MODEL // BN_DOWN
    return pl.pallas_call(
        _moe_down_kernel,
        grid_spec=pltpu.PrefetchScalarGridSpec(
            num_scalar_prefetch=5,
            grid=(n_j, n_tiles),
            in_specs=[
                pl.BlockSpec((tm, f), lambda j, t, te, tb, fl, ne, nv: (tb[t], 0)),
                pl.BlockSpec((None, 1, BN_DOWN), lambda j, t, te, tb, fl, ne, nv: (te[t], 0, j)),
                pl.BlockSpec(memory_space=pl.ANY),
            ],
            out_specs=pl.BlockSpec((tm * ROW_SUB, LANES), lambda j, t, te, tb, fl, ne, nv: (t, 0)),
            scratch_shapes=[pltpu.VMEM((2, f, BN_DOWN), F32), pltpu.SemaphoreType.DMA((2,)),
                            pltpu.SMEM((1,), jnp.int32)],
        ),
        out_shape=jax.ShapeDtypeStruct((p_max * ROW_SUB, LANES), jnp.uint32),
        compiler_params=pltpu.CompilerParams(
            dimension_semantics=("arbitrary", "arbitrary"), vmem_limit_bytes=VMEM_LIMIT),
        name="moe_down",
    )(te, tb, fl, ne, nv, h, b_down3, w_down)


def _combine_kernel(pos_ref, x1_ref, gate_ref, gf_ref, y_hbm, op_ref, os_ref, buf, sem, *, n_prompt_tiles):
    i = pl.program_id(0)
    n_i = pl.num_programs(0)
    tc = x1_ref.shape[0]
    rows = TOP_K * tc

    def issue(tile, slot):
        base = tile * rows

        def body(g, carry):
            r0 = g * ISSUE_UNROLL
            for u in range(ISSUE_UNROLL):
                p = pos_ref[base + r0 + u]
                pltpu.make_async_copy(_row_tile(y_hbm, p), _row_tile(buf.at[slot], r0 + u),
                                      sem.at[slot]).start(priority=u % 2)
            return carry
        lax.fori_loop(0, rows // ISSUE_UNROLL, body, 0)

    @pl.when(i == 0)
    def _():
        issue(0, 0)

    @pl.when(i + 1 < n_i)
    def _():
        issue(i + 1, (i + 1) % 2)

    slot = i % 2
    pltpu.make_async_copy(y_hbm.at[pl.ds(0, rows * ROW_SUB)], buf.at[slot], sem.at[slot]).wait()
    his = [None] * ROW_SUB
    los = [None] * ROW_SUB
    for k in range(TOP_K):
        g = gate_ref[:, k:k + 1]
        for s, chunk in enumerate(_load_row_tiles(buf.at[slot], k * tc, tc)):
            hi, lo = _unpack_hi_lo(chunk)
            his[s] = g * hi if k == 0 else his[s] + g * hi
            los[s] = g * lo if k == 0 else los[s] + g * lo
    acc = x1_ref[...] + jnp.concatenate(his + los, axis=1)
    out = _rms_rows(acc, gf_ref[...])

    @pl.when(i < n_prompt_tiles)
    def _():
        op_ref[...] = out

    @pl.when(i >= n_prompt_tiles)
    def _():
        os_ref[...] = out


def _combine(pos_flat, x1, gates_nk, gf, y_sorted, n_prompt):
    n, d = x1.shape
    tc = TC_COMB
    n_tiles = n // tc
    n_pt = n_prompt // tc
    kern = functools.partial(_combine_kernel, n_prompt_tiles=n_pt)
    return pl.pallas_call(
        kern,
        grid_spec=pltpu.PrefetchScalarGridSpec(
            num_scalar_prefetch=1,
            grid=(n_tiles,),
            in_specs=[
                pl.BlockSpec((tc, d), lambda i, p: (i, 0)),
                pl.BlockSpec((tc, TOP_K), lambda i, p: (i, 0)),
                pl.BlockSpec((1, d), lambda i, p: (0, 0)),
                pl.BlockSpec(memory_space=pl.ANY),
            ],
            out_specs=[
                pl.BlockSpec((tc, d), lambda i, p: (jnp.minimum(i, n_pt - 1), 0)),
                pl.BlockSpec((tc, d), lambda i, p: (jnp.maximum(i - n_pt, 0), 0)),
            ],
            scratch_shapes=[pltpu.VMEM((2, TOP_K * tc * ROW_SUB, LANES), jnp.uint32),
                            pltpu.SemaphoreType.DMA((2,))],
        ),
        out_shape=[
            jax.ShapeDtypeStruct((n_prompt, d), F32),
            jax.ShapeDtypeStruct((n - n_prompt, d), F32),
        ],
        compiler_params=pltpu.CompilerParams(
            dimension_semantics=("arbitrary",), vmem_limit_bytes=VMEM_LIMIT),
        name="combine",
    )(pos_flat, x1, gates_nk, gf, y_sorted)


def kernel(x_prompt, x_sample, state_conv_a, state_conv_b, meta_tokens, norm1_g, w_in, conv_a_w, conv_b_w,
           conv_b_b, ln_b_g, ln_b_b, w_out, norm2_g, w_router, b_router, w_gate_up, b_gate_up, w_down,
           b_down, final_norm_g):
    bp, seq, d = x_prompt.shape
    n_seq, n_t, _ = x_sample.shape
    n_prompt = bp * seq
    n_sample = n_seq * n_t
    n = n_prompt + n_sample
    assert norm1_g.shape[0] == 1, "single layer"
    assert seq % T_MIX == 0 and n_prompt % TM_POST == 0 and n_sample == TM_POST
    assert n_prompt % n_sample == 0 and n % TC_COMB == 0 and n_prompt % TC_COMB == 0

    g1 = norm1_g[0][None]
    win_bf = w_in[0].astype(BF16)
    wout_bf = w_out[0].astype(BF16)
    caw, cbw = conv_a_w[0], conv_b_w[0]
    cbb, lng, lnb = conv_b_b[0][None], ln_b_g[0][None], ln_b_b[0][None]

    xs_t = jnp.transpose(x_sample, (1, 0, 2)).reshape(n_sample, d)
    sa_t = jnp.transpose(state_conv_a[0], (1, 0, 2))
    sb_t = jnp.transpose(state_conv_b[0], (1, 0, 2))

    ymix_p, pa, pb = _mixer_prompt(x_prompt, meta_tokens, g1, win_bf, caw, cbw, cbb, lng, lnb)
    ymix_s, na_t, nb_t = _mixer_sample(xs_t, g1, win_bf, sa_t, sb_t, caw, cbw, cbb, lng, lnb, n_seq, n_t)

    x1, xn, idx, gates, rank, cnt = _post_mixer(
        ymix_p, ymix_s, x_prompt.reshape(n_prompt, d), xs_t, wout_bf, norm2_g[0][None],
        jnp.transpose(w_router[0]), b_router[0][:, None])

    tm = TM_MOE
    n_assign = n * TOP_K
    n_tiles = (n_assign + N_EXPERTS * (tm - 1)) // tm
    p_max = n_tiles * tm
    counts = cnt[:, 0]
    tiles_per_e = (counts + tm - 1) // tm
    tile_end = jnp.cumsum(tiles_per_e)
    tile_start = tile_end - tiles_per_e
    n_used = tile_end[-1]
    e_ar = jnp.arange(N_EXPERTS, dtype=jnp.int32)
    start_of = jnp.sum(jnp.where(idx[None] == e_ar[:, None, None], tile_start[:, None, None], 0), axis=0)
    pos = (start_of * tm + rank).astype(jnp.int32)
    tid = jnp.arange(n_tiles, dtype=jnp.int32)
    tb = jnp.maximum(jnp.minimum(tid, n_used - 1), 0).astype(jnp.int32)
    expert_of = lambda tile: jnp.minimum(
        jnp.sum((tile_end[None, :] <= tile[:, None]).astype(jnp.int32), axis=1), N_EXPERTS - 1).astype(jnp.int32)
    te = expert_of(tb)
    valid = tid < n_used
    new_w = valid & ((tid == 0) | (te != jnp.roll(te, 1)))
    fl = new_w.astype(jnp.int32) * _FLAG_NEW_WEIGHTS
    mine = te[:, None] == e_ar[None, :]
    rows_left = (jnp.sum(jnp.where(mine, counts[None, :], 0), axis=1)
                 - (tid - jnp.sum(jnp.where(mine, tile_start[None, :], 0), axis=1)) * tm)
    nv = jnp.where(valid, jnp.clip(rows_left, 0, tm), 0).astype(jnp.int32)
    next_tile = jnp.sum(jnp.where(te[:, None] == e_ar[None, :], tile_end[None, :], 0), axis=1)
    ne = jnp.where(next_tile < n_used, expert_of(next_tile), -1).astype(jnp.int32)
    partial = (counts % tm) != 0
    zero_flags = ((tid >= n_used) | jnp.any((tid[:, None] == (tile_end - 1)[None, :]) & partial[None, :],
                                            axis=1)).astype(jnp.int32)

    by_tile = lambda tc: pos.reshape(TOP_K, n // tc, tc).transpose(1, 0, 2).reshape(-1)
    xs_sorted = _dispatch(by_tile(TC_DISP), zero_flags, xn, p_max)
    h = _moe_up(te, tb, fl, ne, nv, xs_sorted, w_gate_up[0], b_gate_up[0][:, None, :])
    y_sorted = _moe_down(te, tb, fl, ne, nv, h, w_down[0], b_down[0][:, None, :])

    yp, ys_t = _combine(by_tile(TC_COMB), x1, jnp.transpose(gates), final_norm_g[None], y_sorted, n_prompt)

    y_prompt = yp.reshape(bp, seq, d)
    y_sample = jnp.transpose(ys_t.reshape(n_t, n_seq, d), (1, 0, 2))
    new_a_prompt = pa[None]
    new_b_prompt = pb[None]
    new_a_sample = jnp.transpose(na_t, (1, 0, 2))[None]
    new_b_sample = jnp.transpose(nb_t, (1, 0, 2))[None]
    return (y_prompt, y_sample, new_a_prompt, new_b_prompt, new_a_sample, new_b_sample)
```

```python
import functools

import jax
import jax.numpy as jnp
from jax import lax
from jax.experimental import pallas as pl
from jax.experimental.pallas import tpu as pltpu

F32 = jnp.float32
BF16 = jnp.bfloat16

D_MODEL = 2048
N_META = 16
C_A = 1024
C_B = 1024
K_A = 3
K_B = 31
N_EXPERTS = 32
TOP_K = 4
D_FF = 2048
SWIGLU_LIMIT = 7.0
SWIGLU_ALPHA = 1.702
RMS_EPS = 1e-5
LN_EPS = 1e-5

LANES = 128
N_CHUNK = C_B // LANES
HALO_B = 32
HALO_A = 8
T_MIX = 512
TM_POST = 512
TM_MOE = 512
SUB_MOE = 128
BF_UP = 1024
BN_DOWN = 2048
ROW_SUB = D_MODEL // 2 // LANES
TC_COMB = 256
TC_DISP = 512
ISSUE_UNROLL = 8
VMEM_LIMIT = 56 * 1024 * 1024


def _sigmoid(x):
    return jax.nn.sigmoid(x)


def _rms_rows(x, g):
    ms = jnp.mean(x * x, axis=-1, keepdims=True)
    return (x * lax.rsqrt(ms + RMS_EPS)) * g


def _pack_bf16_pairs(x):
    c = x.shape[1] // 2
    hi = lax.bitcast_convert_type(x[:, :c].astype(BF16).astype(F32), jnp.uint32)
    lo = lax.bitcast_convert_type(x[:, c:].astype(BF16).astype(F32), jnp.uint32)
    return hi | (lo >> 16)


def _unpack_hi_lo(p):
    hi = lax.bitcast_convert_type(p & jnp.uint32(0xFFFF0000), F32)
    lo = lax.bitcast_convert_type(p << 16, F32)
    return hi, lo


def _store_row_tiles(ref, first_row, packed):
    r = packed.shape[0]
    for s in range(ROW_SUB):
        ref[pl.ds(first_row * ROW_SUB + s, r, stride=ROW_SUB), :] = packed[:, s * LANES:(s + 1) * LANES]


def _load_row_tiles(ref, first_row, r):
    return [ref[pl.ds(first_row * ROW_SUB + s, r, stride=ROW_SUB), :] for s in range(ROW_SUB)]


def _mixer_prompt_kernel(x_ref, meta_ref, g1_ref, win_ref, caw_ref, cbw_ref, cbb_ref, lng_ref, lnb_ref,
                         y_ref, pa_ref, pb_ref,
                         u_ref, gb_ref, cv_ref, mgb_ref, mcv_ref, bc_ref):
    b = pl.program_id(0)
    i = pl.program_id(1)
    n_i = pl.num_programs(1)
    t = T_MIX

    def in_proj(rows):
        h = _rms_rows(rows, g1_ref[...]).astype(BF16)
        return jnp.dot(h, win_ref[...], preferred_element_type=F32)

    @pl.when((b == 0) & (i == 0))
    def _():
        um = in_proj(meta_ref[...])
        cvm = um[:, C_A:2 * C_A] * um[:, 2 * C_A:3 * C_A]
        glum = um[:, 3 * C_A:3 * C_A + C_B] * _sigmoid(um[:, 3 * C_A + C_B:])
        for c in range(N_CHUNK):
            sl = slice(c * LANES, (c + 1) * LANES)
            mgb_ref[c, 0:HALO_B - N_META, :] = jnp.zeros((HALO_B - N_META, LANES), F32)
            mgb_ref[c, HALO_B - N_META:HALO_B, :] = glum[:, sl]
            mcv_ref[c] = cvm[N_META - HALO_A:, sl]

    @pl.when(i == 0)
    def _():
        gb_ref[:, 0:HALO_B, :] = mgb_ref[...]
        cv_ref[:, 0:HALO_A, :] = mcv_ref[...]

    u_ref[...] = in_proj(x_ref[...])

    for c in range(N_CHUNK):
        lo = c * LANES
        bg = u_ref[:, lo:lo + LANES]
        cg = u_ref[:, C_A + lo:C_A + lo + LANES]
        v = u_ref[:, 2 * C_A + lo:2 * C_A + lo + LANES]
        ga = u_ref[:, 3 * C_A + lo:3 * C_A + lo + LANES]
        gbv = u_ref[:, 3 * C_A + C_B + lo:3 * C_A + C_B + lo + LANES]
        cv_ref[c, HALO_A:, :] = cg * v
        gb_ref[c, HALO_B:, :] = ga * _sigmoid(gbv)
        acc = cv_ref[c, HALO_A - 2:HALO_A - 2 + t, :] * caw_ref[0:1, lo:lo + LANES]
        for k in range(1, K_A):
            acc = acc + cv_ref[c, HALO_A - 2 + k:HALO_A - 2 + k + t, :] * caw_ref[k:k + 1, lo:lo + LANES]
        y_ref[:, lo:lo + LANES] = (bg * acc).astype(BF16)
        off = HALO_B - (K_B - 1)
        accb = gb_ref[c, off:off + t, :] * cbw_ref[0:1, lo:lo + LANES]
        for k in range(1, K_B):
            accb = accb + gb_ref[c, off + k:off + k + t, :] * cbw_ref[k:k + 1, lo:lo + LANES]
        bc_ref[:, lo:lo + LANES] = accb + cbb_ref[:, lo:lo + LANES]

    rb = 64
    for r in range(t // rb):
        xb = bc_ref[r * rb:(r + 1) * rb, :]
        mu = jnp.mean(xb, axis=-1, keepdims=True)
        var = jnp.mean(jnp.square(xb - mu), axis=-1, keepdims=True)
        bn = (xb - mu) * lax.rsqrt(var + LN_EPS) * lng_ref[...] + lnb_ref[...]
        y_ref[r * rb:(r + 1) * rb, C_A:] = (bn * _sigmoid(bn)).astype(BF16)

    @pl.when(i == n_i - 1)
    def _():
        for c in range(N_CHUNK):
            sl = slice(c * LANES, (c + 1) * LANES)
            pa_ref[:, sl] = cv_ref[c, HALO_A + t - (K_A - 1):HALO_A + t, :]
            pb_ref[:, sl] = gb_ref[c, HALO_B + t - (K_B - 1):HALO_B + t, :]

    gb_ref[:, 0:HALO_B, :] = gb_ref[:, t:t + HALO_B, :]
    cv_ref[:, 0:HALO_A, :] = cv_ref[:, t:t + HALO_A, :]


def _mixer_prompt(x_prompt, meta, g1, win_bf, caw, cbw, cbb, lng, lnb):
    bp, seq, d = x_prompt.shape
    n_i = seq // T_MIX
    full = lambda shape: pl.BlockSpec(shape, lambda b, i: (0,) * len(shape))
    return pl.pallas_call(
        _mixer_prompt_kernel,
        grid=(bp, n_i),
        in_specs=[
            pl.BlockSpec((None, T_MIX, d), lambda b, i: (b, i, 0)),
            full((N_META, d)),
            full((1, d)),
            pl.BlockSpec(win_bf.shape, lambda b, i: (0, 0), pipeline_mode=pl.Buffered(1)),
            full((K_A, C_A)),
            full((K_B, C_B)),
            full((1, C_B)),
            full((1, C_B)),
            full((1, C_B)),
        ],
        out_specs=[
            pl.BlockSpec((T_MIX, d), lambda b, i: (b * n_i + i, 0)),
            pl.BlockSpec((None, K_A - 1, C_A), lambda b, i: (b, 0, 0)),
            pl.BlockSpec((None, K_B - 1, C_B), lambda b, i: (b, 0, 0)),
        ],
        out_shape=[
            jax.ShapeDtypeStruct((bp * seq, d), BF16),
            jax.ShapeDtypeStruct((bp, K_A - 1, C_A), F32),
            jax.ShapeDtypeStruct((bp, K_B - 1, C_B), F32),
        ],
        scratch_shapes=[
            pltpu.VMEM((T_MIX, win_bf.shape[1]), F32),
            pltpu.VMEM((N_CHUNK, T_MIX + HALO_B, LANES), F32),
            pltpu.VMEM((N_CHUNK, T_MIX + HALO_A, LANES), F32),
            pltpu.VMEM((N_CHUNK, HALO_B, LANES), F32),
            pltpu.VMEM((N_CHUNK, HALO_A, LANES), F32),
            pltpu.VMEM((T_MIX, C_B), F32),
        ],
        compiler_params=pltpu.CompilerParams(
            dimension_semantics=("arbitrary", "arbitrary"), vmem_limit_bytes=VMEM_LIMIT),
        name="mixer_prompt",
    )(x_prompt, meta, g1, win_bf, caw, cbw, cbb, lng, lnb)


def _mixer_sample_kernel(x_ref, g1_ref, wbg_ref, wcg_ref, wv_ref, wga_ref, wgb_ref, sa_ref, sb_ref,
                         caw_ref, cbw_ref, cbb_ref, lng_ref, lnb_ref,
                         y_ref, na_ref, nb_ref,
                         h_ref, ya_ref, bc_ref, *, n_seq, n_t):
    c = pl.program_id(0)

    @pl.when(c == 0)
    def _():
        h_ref[...] = _rms_rows(x_ref[...], g1_ref[...]).astype(BF16)

    @pl.when(c < N_CHUNK)
    def _():
        h = h_ref[...]
        proj = lambda w_ref: jnp.dot(h, w_ref[...], preferred_element_type=F32)
        bg, cg, v, ga, gbv = proj(wbg_ref), proj(wcg_ref), proj(wv_ref), proj(wga_ref), proj(wgb_ref)
        cv = cg * v
        glu = ga * _sigmoid(gbv)
        row = lambda a, tt: a[tt * n_seq:(tt + 1) * n_seq, :]
        xa = [sa_ref[j] for j in range(K_A - 1)] + [row(cv, tt) for tt in range(n_t)]
        xb = [sb_ref[j] for j in range(K_B - 1)] + [row(glu, tt) for tt in range(n_t)]
        for tt in range(n_t):
            acc = xa[tt] * caw_ref[0:1, :]
            for k in range(1, K_A):
                acc = acc + xa[tt + k] * caw_ref[k:k + 1, :]
            ya_ref[c, tt * n_seq:(tt + 1) * n_seq, :] = row(bg, tt) * acc
            accb = xb[tt] * cbw_ref[0:1, :]
            for k in range(1, K_B):
                accb = accb + xb[tt + k] * cbw_ref[k:k + 1, :]
            bc_ref[c, tt * n_seq:(tt + 1) * n_seq, :] = accb + cbb_ref[...]
        for j in range(K_B - 1):
            nb_ref[j] = xb[n_t + j]
        for j in range(K_A - 1):
            na_ref[j] = row(cv, n_t - (K_A - 1) + j)

    @pl.when(c == N_CHUNK)
    def _():
        xb = jnp.concatenate([bc_ref[cc] for cc in range(N_CHUNK)], axis=1)
        mu = jnp.mean(xb, axis=-1, keepdims=True)
        var = jnp.mean(jnp.square(xb - mu), axis=-1, keepdims=True)
        bn = (xb - mu) * lax.rsqrt(var + LN_EPS) * lng_ref[...] + lnb_ref[...]
        for cc in range(N_CHUNK):
            y_ref[:, cc * LANES:(cc + 1) * LANES] = ya_ref[cc].astype(BF16)
        y_ref[:, C_A:] = (bn * _sigmoid(bn)).astype(BF16)


def _mixer_sample(xs_t, g1, win_bf, sa_t, sb_t, caw, cbw, cbb, lng, lnb, n_seq, n_t):
    rows, d = xs_t.shape
    cc = lambda c: jnp.minimum(c, N_CHUNK - 1)
    wspec = lambda g: pl.BlockSpec((d, LANES), lambda c, g=g: (0, g * N_CHUNK + cc(c)))
    full = lambda shape: pl.BlockSpec(shape, lambda c: (0,) * len(shape))
    kern = functools.partial(_mixer_sample_kernel, n_seq=n_seq, n_t=n_t)
    return pl.pallas_call(
        kern,
        grid=(N_CHUNK + 1,),
        in_specs=[
            full((rows, d)),
            full((1, d)),
            wspec(0), wspec(1), wspec(2), wspec(3), wspec(4),
            pl.BlockSpec((K_A - 1, n_seq, LANES), lambda c: (0, 0, cc(c))),
            pl.BlockSpec((K_B - 1, n_seq, LANES), lambda c: (0, 0, cc(c))),
            pl.BlockSpec((K_A, LANES), lambda c: (0, cc(c))),
            pl.BlockSpec((K_B, LANES), lambda c: (0, cc(c))),
            pl.BlockSpec((1, LANES), lambda c: (0, cc(c))),
            full((1, C_B)),
            full((1, C_B)),
        ],
        out_specs=[
            full((rows, d)),
            pl.BlockSpec((K_A - 1, n_seq, LANES), lambda c: (0, 0, cc(c))),
            pl.BlockSpec((K_B - 1, n_seq, LANES), lambda c: (0, 0, cc(c))),
        ],
        out_shape=[
            jax.ShapeDtypeStruct((rows, d), BF16),
            jax.ShapeDtypeStruct((K_A - 1, n_seq, C_A), F32),
            jax.ShapeDtypeStruct((K_B - 1, n_seq, C_B), F32),
        ],
        scratch_shapes=[
            pltpu.VMEM((rows, d), BF16),
            pltpu.VMEM((N_CHUNK, rows, LANES), F32),
            pltpu.VMEM((N_CHUNK, rows, LANES), F32),
        ],
        compiler_params=pltpu.CompilerParams(
            dimension_semantics=("arbitrary",), vmem_limit_bytes=VMEM_LIMIT),
        name="mixer_sample",
    )(xs_t, g1, win_bf, win_bf, win_bf, win_bf, win_bf, sa_t, sb_t, caw, cbw, cbb, lng, lnb)


def _post_mixer_kernel(yp_ref, ys_ref, xp_ref, xs_ref, wout_ref, g2_ref, wr_ref, br_ref,
                       x1_ref, xn_ref, idx_ref, gate_ref, rank_ref, cnt_ref,
                       run_ref, tri_ref, *, n_prompt_tiles):
    i = pl.program_id(0)
    tm = yp_ref.shape[0]

    @pl.when(i == 0)
    def _():
        run_ref[...] = jnp.zeros(run_ref.shape, F32)
        s_io = lax.broadcasted_iota(jnp.int32, (tm, tm), 0)
        t_io = lax.broadcasted_iota(jnp.int32, (tm, tm), 1)
        tri_ref[...] = (s_io < t_io).astype(F32).astype(BF16)

    is_prompt = i < n_prompt_tiles
    x = jnp.where(is_prompt, xp_ref[...], xs_ref[...])
    y = jnp.where(is_prompt, yp_ref[...], ys_ref[...])
    x1 = x + jnp.dot(y, wout_ref[...], preferred_element_type=F32)
    x1_ref[...] = x1
    xn = _rms_rows(x1, g2_ref[...])
    _store_row_tiles(xn_ref, 0, _pack_bf16_pairs(xn))

    logits = lax.dot_general(wr_ref[...], xn, (((1,), (1,)), ((), ())),
                             precision=lax.Precision.HIGHEST, preferred_element_type=F32) + br_ref[...]
    eidx = lax.broadcasted_iota(jnp.int32, logits.shape, 0)
    work = logits
    vals, sels, hots = [], [], []
    for _ in range(TOP_K):
        m = jnp.max(work, axis=0, keepdims=True)
        sel = jnp.min(jnp.where(work == m, eidx, N_EXPERTS), axis=0, keepdims=True)
        hot = eidx == sel
        vals.append(m)
        sels.append(sel)
        hots.append(hot)
        work = jnp.where(hot, -jnp.inf, work)
    exps = [jnp.exp(v - vals[0]) for v in vals]
    denom = exps[0] + exps[1] + exps[2] + exps[3]
    for k in range(TOP_K):
        idx_ref[k:k + 1, :] = sels[k]
        gate_ref[k:k + 1, :] = exps[k] / denom

    chosen = (hots[0] | hots[1] | hots[2] | hots[3])
    chosen_bf = chosen.astype(F32).astype(BF16)
    before = jnp.dot(chosen_bf, tri_ref[...], preferred_element_type=F32) + run_ref[:, 0:1]
    for k in range(TOP_K):
        r = jnp.sum(jnp.where(hots[k], before, 0.0), axis=0, keepdims=True)
        rank_ref[k:k + 1, :] = r.astype(jnp.int32)
    run_ref[...] = run_ref[...] + jnp.sum(chosen.astype(F32), axis=1, keepdims=True)
    cnt_ref[...] = run_ref[...].astype(jnp.int32)


def _post_mixer(yp, ys, xp2, xs_t, wout_bf, g2, wr_t, br_col):
    d = yp.shape[1]
    n = yp.shape[0] + ys.shape[0]
    tm = TM_POST
    n_tiles = n // tm
    n_pt = xp2.shape[0] // tm
    kern = functools.partial(_post_mixer_kernel, n_prompt_tiles=n_pt)
    full = lambda shape: pl.BlockSpec(shape, lambda i: (0,) * len(shape))
    return pl.pallas_call(
        kern,
        grid=(n_tiles,),
        in_specs=[
            pl.BlockSpec((tm, d), lambda i: (jnp.minimum(i, n_pt - 1), 0)),
            pl.BlockSpec((tm, d), lambda i: (jnp.maximum(i - n_pt, 0), 0)),
            pl.BlockSpec((tm, d), lambda i: (jnp.minimum(i, n_pt - 1), 0)),
            pl.BlockSpec((tm, d), lambda i: (jnp.maximum(i - n_pt, 0), 0)),
            pl.BlockSpec(wout_bf.shape, lambda i: (0, 0), pipeline_mode=pl.Buffered(1)),
            full((1, d)),
            full((N_EXPERTS, d)),
            full((N_EXPERTS, 1)),
        ],
        out_specs=[
            pl.BlockSpec((tm, d), lambda i: (i, 0)),
            pl.BlockSpec((tm * ROW_SUB, LANES), lambda i: (i, 0)),
            pl.BlockSpec((TOP_K, tm), lambda i: (0, i)),
            pl.BlockSpec((TOP_K, tm), lambda i: (0, i)),
            pl.BlockSpec((TOP_K, tm), lambda i: (0, i)),
            full((N_EXPERTS, LANES)),
        ],
        out_shape=[
            jax.ShapeDtypeStruct((n, d), F32),
            jax.ShapeDtypeStruct((n * ROW_SUB, LANES), jnp.uint32),
            jax.ShapeDtypeStruct((TOP_K, n), jnp.int32),
            jax.ShapeDtypeStruct((TOP_K, n), F32),
            jax.ShapeDtypeStruct((TOP_K, n), jnp.int32),
            jax.ShapeDtypeStruct((N_EXPERTS, LANES), jnp.int32),
        ],
        scratch_shapes=[pltpu.VMEM((N_EXPERTS, LANES), F32), pltpu.VMEM((tm, tm), BF16)],
        compiler_params=pltpu.CompilerParams(
            dimension_semantics=("arbitrary",), vmem_limit_bytes=VMEM_LIMIT),
        name="post_mixer",
    )(yp, ys, xp2, xs_t, wout_bf, g2, wr_t, br_col)


def _row_tile(ref, row):
    return ref.at[pl.ds(pl.multiple_of(row * ROW_SUB, ROW_SUB), ROW_SUB)]


def _dispatch_kernel(pos_ref, zf_ref, x_ref, xs_hbm, zbuf, sem, zsem):
    i = pl.program_id(0)
    tc = x_ref.shape[0] // ROW_SUB
    zrows = zbuf.shape[0]
    n_tiles = xs_hbm.shape[0] // zrows

    def zero_copy(t):
        return pltpu.make_async_copy(zbuf, xs_hbm.at[pl.ds(pl.multiple_of(t * zrows, zrows), zrows)], zsem)

    @pl.when(i == 0)
    def _():
        zbuf[...] = jnp.zeros(zbuf.shape, zbuf.dtype)

        def start(t, carry):
            @pl.when(zf_ref[t] != 0)
            def _():
                zero_copy(t).start()
            return carry
        lax.fori_loop(0, n_tiles, start, 0)

        def wait(t, carry):
            @pl.when(zf_ref[t] != 0)
            def _():
                zero_copy(t).wait()
            return carry
        lax.fori_loop(0, n_tiles, wait, 0)

    base = i * (TOP_K * tc)

    def body(g, carry):
        r0 = g * ISSUE_UNROLL
        for u in range(ISSUE_UNROLL):
            for k in range(TOP_K):
                p = pos_ref[base + k * tc + r0 + u]
                pltpu.make_async_copy(_row_tile(x_ref, r0 + u), _row_tile(xs_hbm, p), sem).start(
                    priority=(u * TOP_K + k) % 2)
        return carry
    lax.fori_loop(0, tc // ISSUE_UNROLL, body, 0)
    for k in range(TOP_K):
        pltpu.make_async_copy(x_ref, xs_hbm.at[pl.ds(0, tc * ROW_SUB)], sem).wait()


def _dispatch(pos_tiles, zero_flags, xn_tiles, p_max):
    n = xn_tiles.shape[0] // ROW_SUB
    tc = TC_DISP
    return pl.pallas_call(
        _dispatch_kernel,
        grid_spec=pltpu.PrefetchScalarGridSpec(
            num_scalar_prefetch=2,
            grid=(n // tc,),
            in_specs=[pl.BlockSpec((tc * ROW_SUB, LANES), lambda i, p, z: (i, 0))],
            out_specs=pl.BlockSpec(memory_space=pl.ANY),
            scratch_shapes=[pltpu.VMEM((TM_MOE * ROW_SUB, LANES), jnp.uint32), pltpu.SemaphoreType.DMA(()),
                            pltpu.SemaphoreType.DMA(())],
        ),
        out_shape=jax.ShapeDtypeStruct((p_max * ROW_SUB, LANES), jnp.uint32),
        compiler_params=pltpu.CompilerParams(
            dimension_semantics=("arbitrary",), vmem_limit_bytes=VMEM_LIMIT),
        name="dispatch",
    )(pos_tiles, zero_flags, xn_tiles)


_FLAG_NEW_WEIGHTS = 1
WEIGHT_DMA_PRIORITY = 1


def _advance_weights(te_ref, ne_ref, fl_ref, slot_ref, copies):
    j = pl.program_id(0)
    t = pl.program_id(1)
    n_j = pl.num_programs(0)

    @pl.when((fl_ref[t] & _FLAG_NEW_WEIGHTS) != 0)
    def _():
        first = (j == 0) & (t == 0)

        @pl.when(first)
        def _():
            slot_ref[0] = 0
            for c in copies(te_ref[t], j, 0):
                c.start(priority=WEIGHT_DMA_PRIORITY)

        @pl.when(jnp.logical_not(first))
        def _():
            slot_ref[0] = 1 - slot_ref[0]

        s = slot_ref[0]
        for c in copies(te_ref[t], j, s):
            c.wait()
        ne = ne_ref[t]

        @pl.when(ne >= 0)
        def _():
            for c in copies(ne, j, 1 - s):
                c.start(priority=WEIGHT_DMA_PRIORITY)

        @pl.when((ne < 0) & (j + 1 < n_j))
        def _():
            for c in copies(te_ref[0], j + 1, 1 - s):
                c.start(priority=WEIGHT_DMA_PRIORITY)


def _by_valid_rows(n_valid, compute, zero_from):
    for rows in range(0, TM_MOE + 1, SUB_MOE):
        @pl.when((n_valid > rows - SUB_MOE) & (n_valid <= rows))
        def _():
            if rows > 0:
                compute(rows)
            if rows < TM_MOE:
                zero_from(rows)


def _bf16_dot(x_bf, w_f32):
    return lax.dot_general(x_bf, w_f32, (((1,), (0,)), ((), ())), preferred_element_type=F32)


def _moe_up_kernel(te_ref, tb_ref, fl_ref, ne_ref, nv_ref, x_ref, bg_ref, bu_ref, w_hbm, h_ref, wbuf, sem,
                   slot_ref):
    t = pl.program_id(1)
    n_valid = nv_ref[t]

    def copies(e, j, slot):
        cg = pl.multiple_of(j * BF_UP, BF_UP)
        cu = pl.multiple_of(D_FF + j * BF_UP, BF_UP)
        return (pltpu.make_async_copy(w_hbm.at[e, :, pl.ds(cg, BF_UP)], wbuf.at[slot, 0], sem.at[slot]),
                pltpu.make_async_copy(w_hbm.at[e, :, pl.ds(cu, BF_UP)], wbuf.at[slot, 1], sem.at[slot]))

    _advance_weights(te_ref, ne_ref, fl_ref, slot_ref, copies)

    def compute(n_rows):
        s = slot_ref[0]
        halves = [_unpack_hi_lo(c) for c in _load_row_tiles(x_ref, 0, n_rows)]
        x = jnp.concatenate([hl[0].astype(BF16) for hl in halves] + [hl[1].astype(BF16) for hl in halves], axis=1)
        gate = _bf16_dot(x, wbuf[s, 0]) + bg_ref[...]
        up = _bf16_dot(x, wbuf[s, 1]) + bu_ref[...]
        gate = jnp.minimum(gate, SWIGLU_LIMIT)
        up = jnp.clip(up, -SWIGLU_LIMIT, SWIGLU_LIMIT)
        act = gate * _sigmoid(SWIGLU_ALPHA * gate) * (up + 1.0)
        h_ref[0:n_rows, :] = act.astype(BF16)

    def zero_from(r0):
        h_ref[r0:TM_MOE, :] = jnp.zeros((TM_MOE - r0, h_ref.shape[1]), BF16)

    _by_valid_rows(n_valid, compute, zero_from)


def _moe_up(te, tb, fl, ne, nv, xs, w_gate_up, b_gate_up3):
    p_max = xs.shape[0] // ROW_SUB
    d = D_MODEL
    tm = TM_MOE
    n_tiles = p_max // tm
    n_j = D_FF // BF_UP
    return pl.pallas_call(
        _moe_up_kernel,
        grid_spec=pltpu.PrefetchScalarGridSpec(
            num_scalar_prefetch=5,
            grid=(n_j, n_tiles),
            in_specs=[
                pl.BlockSpec((tm * ROW_SUB, LANES), lambda j, t, te, tb, fl, ne, nv: (tb[t], 0)),
                pl.BlockSpec((None, 1, BF_UP), lambda j, t, te, tb, fl, ne, nv: (te[t], 0, j)),
                pl.BlockSpec((None, 1, BF_UP), lambda j, t, te, tb, fl, ne, nv: (te[t], 0, n_j + j)),
                pl.BlockSpec(memory_space=pl.ANY),
            ],
            out_specs=pl.BlockSpec((tm, BF_UP), lambda j, t, te, tb, fl, ne, nv: (t, j)),
            scratch_shapes=[pltpu.VMEM((2, 2, d, BF_UP), F32), pltpu.SemaphoreType.DMA((2,)),
                            pltpu.SMEM((1,), jnp.int32)],
        ),
        out_shape=jax.ShapeDtypeStruct((p_max, D_FF), BF16),
        compiler_params=pltpu.CompilerParams(
            dimension_semantics=("arbitrary", "arbitrary"), vmem_limit_bytes=VMEM_LIMIT),
        name="moe_up",
    )(te, tb, fl, ne, nv, xs, b_gate_up3, b_gate_up3, w_gate_up)


def _moe_down_kernel(te_ref, tb_ref, fl_ref, ne_ref, nv_ref, h_ref, bd_ref, w_hbm, y_ref, wbuf, sem, slot_ref):
    t = pl.program_id(1)
    n_valid = nv_ref[t]

    def copies(e, j, slot):
        c0 = pl.multiple_of(j * BN_DOWN, BN_DOWN)
        return (pltpu.make_async_copy(w_hbm.at[e, :, pl.ds(c0, BN_DOWN)], wbuf.at[slot], sem.at[slot]),)

    _advance_weights(te_ref, ne_ref, fl_ref, slot_ref, copies)

    def compute(n_rows):
        y = _bf16_dot(h_ref[0:n_rows, :], wbuf[slot_ref[0]]) + bd_ref[...]
        _store_row_tiles(y_ref, 0, _pack_bf16_pairs(y))

    def zero_from(r0):
        y_ref[r0 * ROW_SUB:TM_MOE * ROW_SUB, :] = jnp.zeros(((TM_MOE - r0) * ROW_SUB, LANES), y_ref.dtype)

    _by_valid_rows(n_valid, compute, zero_from)


def _moe_down(te, tb, fl, ne, nv, h, w_down, b_down3):
    p_max, f = h.shape
    tm = TM_MOE
    n_tiles = p_max // tm
    n_j = D_MODEL // BN_DOWN
    return pl.pallas_call(
        _moe_down_kernel,
        grid_spec=pltpu.PrefetchScalarGridSpec(
            num_scalar_prefetch=5,
            grid=(n_j, n_tiles),
            in_specs=[
                pl.BlockSpec((tm, f), lambda j, t, te, tb, fl, ne, nv: (tb[t], 0)),
                pl.BlockSpec((None, 1, BN_DOWN), lambda j, t, te, tb, fl, ne, nv: (te[t], 0, j)),
                pl.BlockSpec(memory_space=pl.ANY),
            ],
            out_specs=pl.BlockSpec((tm * ROW_SUB, LANES), lambda j, t, te, tb, fl, ne, nv: (t, 0)),
            scratch_shapes=[pltpu.VMEM((2, f, BN_DOWN), F32), pltpu.SemaphoreType.DMA((2,)),
                            pltpu.SMEM((1,), jnp.int32)],
        ),
        out_shape=jax.ShapeDtypeStruct((p_max * ROW_SUB, LANES), jnp.uint32),
        compiler_params=pltpu.CompilerParams(
            dimension_semantics=("arbitrary", "arbitrary"), vmem_limit_bytes=VMEM_LIMIT),
        name="moe_down",
    )(te, tb, fl, ne, nv, h, b_down3, w_down)


def _combine_kernel(pos_ref, x1_ref, gate_ref, gf_ref, y_hbm, op_ref, os_ref, buf, sem, *, n_prompt_tiles):
    i = pl.program_id(0)
    n_i = pl.num_programs(0)
    tc = x1_ref.shape[0]
    rows = TOP_K * tc

    def issue(tile, slot):
        base = tile * rows

        def body(g, carry):
            r0 = g * ISSUE_UNROLL
            for u in range(ISSUE_UNROLL):
                p = pos_ref[base + r0 + u]
                pltpu.make_async_copy(_row_tile(y_hbm, p), _row_tile(buf.at[slot], r0 + u),
                                      sem.at[slot]).start(priority=u % 2)
            return carry
        lax.fori_loop(0, rows // ISSUE_UNROLL, body, 0)

    @pl.when(i == 0)
    def _():
        issue(0, 0)

    @pl.when(i + 1 < n_i)
    def _():
        issue(i + 1, (i + 1) % 2)

    slot = i % 2
    pltpu.make_async_copy(y_hbm.at[pl.ds(0, rows * ROW_SUB)], buf.at[slot], sem.at[slot]).wait()
    his = [None] * ROW_SUB
    los = [None] * ROW_SUB
    for k in range(TOP_K):
        g = gate_ref[:, k:k + 1]
        for s, chunk in enumerate(_load_row_tiles(buf.at[slot], k * tc, tc)):
            hi, lo = _unpack_hi_lo(chunk)
            his[s] = g * hi if k == 0 else his[s] + g * hi
            los[s] = g * lo if k == 0 else los[s] + g * lo
    acc = x1_ref[...] + jnp.concatenate(his + los, axis=1)
    out = _rms_rows(acc, gf_ref[...])

    @pl.when(i < n_prompt_tiles)
    def _():
        op_ref[...] = out

    @pl.when(i >= n_prompt_tiles)
    def _():
        os_ref[...] = out


def _combine(pos_flat, x1, gates_nk, gf, y_sorted, n_prompt):
    n, d = x1.shape
    tc = TC_COMB
    n_tiles = n // tc
    n_pt = n_prompt // tc
    kern = functools.partial(_combine_kernel, n_prompt_tiles=n_pt)
    return pl.pallas_call(
        kern,
        grid_spec=pltpu.PrefetchScalarGridSpec(
            num_scalar_prefetch=1,
            grid=(n_tiles,),
            in_specs=[
                pl.BlockSpec((tc, d), lambda i, p: (i, 0)),
                pl.BlockSpec((tc, TOP_K), lambda i, p: (i, 0)),
                pl.BlockSpec((1, d), lambda i, p: (0, 0)),
                pl.BlockSpec(memory_space=pl.ANY),
            ],
            out_specs=[
                pl.BlockSpec((tc, d), lambda i, p: (jnp.minimum(i, n_pt - 1), 0)),
                pl.BlockSpec((tc, d), lambda i, p: (jnp.maximum(i - n_pt, 0), 0)),
            ],
            scratch_shapes=[pltpu.VMEM((2, TOP_K * tc * ROW_SUB, LANES), jnp.uint32),
                            pltpu.SemaphoreType.DMA((2,))],
        ),
        out_shape=[
            jax.ShapeDtypeStruct((n_prompt, d), F32),
            jax.ShapeDtypeStruct((n - n_prompt, d), F32),
        ],
        compiler_params=pltpu.CompilerParams(
            dimension_semantics=("arbitrary",), vmem_limit_bytes=VMEM_LIMIT),
        name="combine",
    )(pos_flat, x1, gates_nk, gf, y_sorted)


def kernel(x_prompt, x_sample, state_conv_a, state_conv_b, meta_tokens, norm1_g, w_in, conv_a_w, conv_b_w,
           conv_b_b, ln_b_g, ln_b_b, w_out, norm2_g, w_router, b_router, w_gate_up, b_gate_up, w_down,
           b_down, final_norm_g):
    bp, seq, d = x_prompt.shape
    n_seq, n_t, _ = x_sample.shape
    n_prompt = bp * seq
    n_sample = n_seq * n_t
    n = n_prompt + n_sample
    assert norm1_g.shape[0] == 1, "single layer"
    assert seq % T_MIX == 0 and n_prompt % TM_POST == 0 and n_sample == TM_POST
    assert n_prompt % n_sample == 0 and n % TC_COMB == 0 and n_prompt % TC_COMB == 0

    g1 = norm1_g[0][None]
    win_bf = w_in[0].astype(BF16)
    wout_bf = w_out[0].astype(BF16)
    caw, cbw = conv_a_w[0], conv_b_w[0]
    cbb, lng, lnb = conv_b_b[0][None], ln_b_g[0][None], ln_b_b[0][None]

    xs_t = jnp.transpose(x_sample, (1, 0, 2)).reshape(n_sample, d)
    sa_t = jnp.transpose(state_conv_a[0], (1, 0, 2))
    sb_t = jnp.transpose(state_conv_b[0], (1, 0, 2))

    ymix_p, pa, pb = _mixer_prompt(x_prompt, meta_tokens, g1, win_bf, caw, cbw, cbb, lng, lnb)
    ymix_s, na_t, nb_t = _mixer_sample(xs_t, g1, win_bf, sa_t, sb_t, caw, cbw, cbb, lng, lnb, n_seq, n_t)

    x1, xn, idx, gates, rank, cnt = _post_mixer(
        ymix_p, ymix_s, x_prompt.reshape(n_prompt, d), xs_t, wout_bf, norm2_g[0][None],
        jnp.transpose(w_router[0]), b_router[0][:, None])

    tm = TM_MOE
    n_assign = n * TOP_K
    n_tiles = (n_assign + N_EXPERTS * (tm - 1)) // tm
    p_max = n_tiles * tm
    counts = cnt[:, 0]
    tiles_per_e = (counts + tm - 1) // tm
    tile_end = jnp.cumsum(tiles_per_e)
    tile_start = tile_end - tiles_per_e
    n_used = tile_end[-1]
    e_ar = jnp.arange(N_EXPERTS, dtype=jnp.int32)
    start_of = jnp.sum(jnp.where(idx[None] == e_ar[:, None, None], tile_start[:, None, None], 0), axis=0)
    pos = (start_of * tm + rank).astype(jnp.int32)
    tid = jnp.arange(n_tiles, dtype=jnp.int32)
    tb = jnp.maximum(jnp.minimum(tid, n_used - 1), 0).astype(jnp.int32)
    expert_of = lambda tile: jnp.minimum(
        jnp.sum((tile_end[None, :] <= tile[:, None]).astype(jnp.int32), axis=1), N_EXPERTS - 1).astype(jnp.int32)
    te = expert_of(tb)
    valid = tid < n_used
    new_w = valid & ((tid == 0) | (te != jnp.roll(te, 1)))
    fl = new_w.astype(jnp.int32) * _FLAG_NEW_WEIGHTS
    mine = te[:, None] == e_ar[None, :]
    rows_left = (jnp.sum(jnp.where(mine, counts[None, :], 0), axis=1)
                 - (tid - jnp.sum(jnp.where(mine, tile_start[None, :], 0), axis=1)) * tm)
    nv = jnp.where(valid, jnp.clip(rows_left, 0, tm), 0).astype(jnp.int32)
    next_tile = jnp.sum(jnp.where(te[:, None] == e_ar[None, :], tile_end[None, :], 0), axis=1)
    ne = jnp.where(next_tile < n_used, expert_of(next_tile), -1).astype(jnp.int32)
    partial = (counts % tm) != 0
    zero_flags = ((tid >= n_used) | jnp.any((tid[:, None] == (tile_end - 1)[None, :]) & partial[None, :],
                                            axis=1)).astype(jnp.int32)

    by_tile = lambda tc: pos.reshape(TOP_K, n // tc, tc).transpose(1, 0, 2).reshape(-1)
    xs_sorted = _dispatch(by_tile(TC_DISP), zero_flags, xn, p_max)
    h = _moe_up(te, tb, fl, ne, nv, xs_sorted, w_gate_up[0], b_gate_up[0][:, None, :])
    y_sorted = _moe_down(te, tb, fl, ne, nv, h, w_down[0], b_down[0][:, None, :])

    yp, ys_t = _combine(by_tile(TC_COMB), x1, jnp.transpose(gates), final_norm_g[None], y_sorted, n_prompt)

    y_prompt = yp.reshape(bp, seq, d)
    y_sample = jnp.transpose(ys_t.reshape(n_t, n_seq, d), (1, 0, 2))
    new_a_prompt = pa[None]
    new_b_prompt = pb[None]
    new_a_sample = jnp.transpose(na_t, (1, 0, 2))[None]
    new_b_sample = jnp.transpose(nb_t, (1, 0, 2))[None]
    return (y_prompt, y_sample, new_a_prompt, new_b_prompt, new_a_sample, new_b_sample)
```

```python
import functools

import jax
import jax.numpy as jnp
from jax import lax
from jax.experimental import pallas as pl
from jax.experimental.pallas import tpu as pltpu

F32 = jnp.float32
BF16 = jnp.bfloat16

D_MODEL = 2048
N_META = 16
C_A = 1024
C_B = 1024
K_A = 3
K_B = 31
N_EXPERTS = 32
TOP_K = 4
D_FF = 2048
SWIGLU_LIMIT = 7.0
SWIGLU_ALPHA = 1.702
RMS_EPS = 1e-5
LN_EPS = 1e-5

LANES = 128
N_CHUNK = C_B // LANES
HALO_B = 32
HALO_A = 8
T_MIX = 512
TM_POST = 512
TM_MOE = 512
SUB_MOE = 128
BF_UP = 1024
BN_DOWN = 2048
ROW_SUB = D_MODEL // 2 // LANES
TC_COMB = 256
TC_DISP = 512
ISSUE_UNROLL = 16
VMEM_LIMIT = 56 * 1024 * 1024


def _sigmoid(x):
    return jax.nn.sigmoid(x)


def _rms_rows(x, g):
    ms = jnp.mean(x * x, axis=-1, keepdims=True)
    return (x * lax.rsqrt(ms + RMS_EPS)) * g


def _pack_bf16_pairs(x):
    c = x.shape[1] // 2
    hi = lax.bitcast_convert_type(x[:, :c].astype(BF16).astype(F32), jnp.uint32)
    lo = lax.bitcast_convert_type(x[:, c:].astype(BF16).astype(F32), jnp.uint32)
    return hi | (lo >> 16)


def _unpack_hi_lo(p):
    hi = lax.bitcast_convert_type(p & jnp.uint32(0xFFFF0000), F32)
    lo = lax.bitcast_convert_type(p << 16, F32)
    return hi, lo


def _store_row_tiles(ref, first_row, packed):
    r = packed.shape[0]
    for s in range(ROW_SUB):
        ref[pl.ds(first_row * ROW_SUB + s, r, stride=ROW_SUB), :] = packed[:, s * LANES:(s + 1) * LANES]


def _load_row_tiles(ref, first_row, r):
    return [ref[pl.ds(first_row * ROW_SUB + s, r, stride=ROW_SUB), :] for s in range(ROW_SUB)]


def _mixer_prompt_kernel(x_ref, meta_ref, g1_ref, win_ref, caw_ref, cbw_ref, cbb_ref, lng_ref, lnb_ref,
                         y_ref, pa_ref, pb_ref,
                         u_ref, gb_ref, cv_ref, mgb_ref, mcv_ref, bc_ref):
    b = pl.program_id(0)
    i = pl.program_id(1)
    n_i = pl.num_programs(1)
    t = T_MIX

    def in_proj(rows):
        h = _rms_rows(rows, g1_ref[...]).astype(BF16)
        return jnp.dot(h, win_ref[...], preferred_element_type=F32)

    @pl.when((b == 0) & (i == 0))
    def _():
        um = in_proj(meta_ref[...])
        cvm = um[:, C_A:2 * C_A] * um[:, 2 * C_A:3 * C_A]
        glum = um[:, 3 * C_A:3 * C_A + C_B] * _sigmoid(um[:, 3 * C_A + C_B:])
        for c in range(N_CHUNK):
            sl = slice(c * LANES, (c + 1) * LANES)
            mgb_ref[c, 0:HALO_B - N_META, :] = jnp.zeros((HALO_B - N_META, LANES), F32)
            mgb_ref[c, HALO_B - N_META:HALO_B, :] = glum[:, sl]
            mcv_ref[c] = cvm[N_META - HALO_A:, sl]

    @pl.when(i == 0)
    def _():
        gb_ref[:, 0:HALO_B, :] = mgb_ref[...]
        cv_ref[:, 0:HALO_A, :] = mcv_ref[...]

    u_ref[...] = in_proj(x_ref[...])

    for c in range(N_CHUNK):
        lo = c * LANES
        bg = u_ref[:, lo:lo + LANES]
        cg = u_ref[:, C_A + lo:C_A + lo + LANES]
        v = u_ref[:, 2 * C_A + lo:2 * C_A + lo + LANES]
        ga = u_ref[:, 3 * C_A + lo:3 * C_A + lo + LANES]
        gbv = u_ref[:, 3 * C_A + C_B + lo:3 * C_A + C_B + lo + LANES]
        cv_ref[c, HALO_A:, :] = cg * v
        gb_ref[c, HALO_B:, :] = ga * _sigmoid(gbv)
        acc = cv_ref[c, HALO_A - 2:HALO_A - 2 + t, :] * caw_ref[0:1, lo:lo + LANES]
        for k in range(1, K_A):
            acc = acc + cv_ref[c, HALO_A - 2 + k:HALO_A - 2 + k + t, :] * caw_ref[k:k + 1, lo:lo + LANES]
        y_ref[:, lo:lo + LANES] = (bg * acc).astype(BF16)
        off = HALO_B - (K_B - 1)
        accb = gb_ref[c, off:off + t, :] * cbw_ref[0:1, lo:lo + LANES]
        for k in range(1, K_B):
            accb = accb + gb_ref[c, off + k:off + k + t, :] * cbw_ref[k:k + 1, lo:lo + LANES]
        bc_ref[:, lo:lo + LANES] = accb + cbb_ref[:, lo:lo + LANES]

    rb = 64
    for r in range(t // rb):
        xb = bc_ref[r * rb:(r + 1) * rb, :]
        mu = jnp.mean(xb, axis=-1, keepdims=True)
        var = jnp.mean(jnp.square(xb - mu), axis=-1, keepdims=True)
        bn = (xb - mu) * lax.rsqrt(var + LN_EPS) * lng_ref[...] + lnb_ref[...]
        y_ref[r * rb:(r + 1) * rb, C_A:] = (bn * _sigmoid(bn)).astype(BF16)

    @pl.when(i == n_i - 1)
    def _():
        for c in range(N_CHUNK):
            sl = slice(c * LANES, (c + 1) * LANES)
            pa_ref[:, sl] = cv_ref[c, HALO_A + t - (K_A - 1):HALO_A + t, :]
            pb_ref[:, sl] = gb_ref[c, HALO_B + t - (K_B - 1):HALO_B + t, :]

    gb_ref[:, 0:HALO_B, :] = gb_ref[:, t:t + HALO_B, :]
    cv_ref[:, 0:HALO_A, :] = cv_ref[:, t:t + HALO_A, :]


def _mixer_prompt(x_prompt, meta, g1, win_bf, caw, cbw, cbb, lng, lnb):
    bp, seq, d = x_prompt.shape
    n_i = seq // T_MIX
    full = lambda shape: pl.BlockSpec(shape, lambda b, i: (0,) * len(shape))
    return pl.pallas_call(
        _mixer_prompt_kernel,
        grid=(bp, n_i),
        in_specs=[
            pl.BlockSpec((None, T_MIX, d), lambda b, i: (b, i, 0)),
            full((N_META, d)),
            full((1, d)),
            pl.BlockSpec(win_bf.shape, lambda b, i: (0, 0), pipeline_mode=pl.Buffered(1)),
            full((K_A, C_A)),
            full((K_B, C_B)),
            full((1, C_B)),
            full((1, C_B)),
            full((1, C_B)),
        ],
        out_specs=[
            pl.BlockSpec((T_MIX, d), lambda b, i: (b * n_i + i, 0)),
            pl.BlockSpec((None, K_A - 1, C_A), lambda b, i: (b, 0, 0)),
            pl.BlockSpec((None, K_B - 1, C_B), lambda b, i: (b, 0, 0)),
        ],
        out_shape=[
            jax.ShapeDtypeStruct((bp * seq, d), BF16),
            jax.ShapeDtypeStruct((bp, K_A - 1, C_A), F32),
            jax.ShapeDtypeStruct((bp, K_B - 1, C_B), F32),
        ],
        scratch_shapes=[
            pltpu.VMEM((T_MIX, win_bf.shape[1]), F32),
            pltpu.VMEM((N_CHUNK, T_MIX + HALO_B, LANES), F32),
            pltpu.VMEM((N_CHUNK, T_MIX + HALO_A, LANES), F32),
            pltpu.VMEM((N_CHUNK, HALO_B, LANES), F32),
            pltpu.VMEM((N_CHUNK, HALO_A, LANES), F32),
            pltpu.VMEM((T_MIX, C_B), F32),
        ],
        compiler_params=pltpu.CompilerParams(
            dimension_semantics=("arbitrary", "arbitrary"), vmem_limit_bytes=VMEM_LIMIT),
        name="mixer_prompt",
    )(x_prompt, meta, g1, win_bf, caw, cbw, cbb, lng, lnb)


def _mixer_sample_kernel(x_ref, g1_ref, wbg_ref, wcg_ref, wv_ref, wga_ref, wgb_ref, sa_ref, sb_ref,
                         caw_ref, cbw_ref, cbb_ref, lng_ref, lnb_ref,
                         y_ref, na_ref, nb_ref,
                         h_ref, ya_ref, bc_ref, *, n_seq, n_t):
    c = pl.program_id(0)

    @pl.when(c == 0)
    def _():
        h_ref[...] = _rms_rows(x_ref[...], g1_ref[...]).astype(BF16)

    @pl.when(c < N_CHUNK)
    def _():
        h = h_ref[...]
        proj = lambda w_ref: jnp.dot(h, w_ref[...], preferred_element_type=F32)
        bg, cg, v, ga, gbv = proj(wbg_ref), proj(wcg_ref), proj(wv_ref), proj(wga_ref), proj(wgb_ref)
        cv = cg * v
        glu = ga * _sigmoid(gbv)
        row = lambda a, tt: a[tt * n_seq:(tt + 1) * n_seq, :]
        xa = [sa_ref[j] for j in range(K_A - 1)] + [row(cv, tt) for tt in range(n_t)]
        xb = [sb_ref[j] for j in range(K_B - 1)] + [row(glu, tt) for tt in range(n_t)]
        for tt in range(n_t):
            acc = xa[tt] * caw_ref[0:1, :]
            for k in range(1, K_A):
                acc = acc + xa[tt + k] * caw_ref[k:k + 1, :]
            ya_ref[c, tt * n_seq:(tt + 1) * n_seq, :] = row(bg, tt) * acc
            accb = xb[tt] * cbw_ref[0:1, :]
            for k in range(1, K_B):
                accb = accb + xb[tt + k] * cbw_ref[k:k + 1, :]
            bc_ref[c, tt * n_seq:(tt + 1) * n_seq, :] = accb + cbb_ref[...]
        for j in range(K_B - 1):
            nb_ref[j] = xb[n_t + j]
        for j in range(K_A - 1):
            na_ref[j] = row(cv, n_t - (K_A - 1) + j)

    @pl.when(c == N_CHUNK)
    def _():
        xb = jnp.concatenate([bc_ref[cc] for cc in range(N_CHUNK)], axis=1)
        mu = jnp.mean(xb, axis=-1, keepdims=True)
        var = jnp.mean(jnp.square(xb - mu), axis=-1, keepdims=True)
        bn = (xb - mu) * lax.rsqrt(var + LN_EPS) * lng_ref[...] + lnb_ref[...]
        for cc in range(N_CHUNK):
            y_ref[:, cc * LANES:(cc + 1) * LANES] = ya_ref[cc].astype(BF16)
        y_ref[:, C_A:] = (bn * _sigmoid(bn)).astype(BF16)


def _mixer_sample(xs_t, g1, win_bf, sa_t, sb_t, caw, cbw, cbb, lng, lnb, n_seq, n_t):
    rows, d = xs_t.shape
    cc = lambda c: jnp.minimum(c, N_CHUNK - 1)
    wspec = lambda g: pl.BlockSpec((d, LANES), lambda c, g=g: (0, g * N_CHUNK + cc(c)))
    full = lambda shape: pl.BlockSpec(shape, lambda c: (0,) * len(shape))
    kern = functools.partial(_mixer_sample_kernel, n_seq=n_seq, n_t=n_t)
    return pl.pallas_call(
        kern,
        grid=(N_CHUNK + 1,),
        in_specs=[
            full((rows, d)),
            full((1, d)),
            wspec(0), wspec(1), wspec(2), wspec(3), wspec(4),
            pl.BlockSpec((K_A - 1, n_seq, LANES), lambda c: (0, 0, cc(c))),
            pl.BlockSpec((K_B - 1, n_seq, LANES), lambda c: (0, 0, cc(c))),
            pl.BlockSpec((K_A, LANES), lambda c: (0, cc(c))),
            pl.BlockSpec((K_B, LANES), lambda c: (0, cc(c))),
            pl.BlockSpec((1, LANES), lambda c: (0, cc(c))),
            full((1, C_B)),
            full((1, C_B)),
        ],
        out_specs=[
            full((rows, d)),
            pl.BlockSpec((K_A - 1, n_seq, LANES), lambda c: (0, 0, cc(c))),
            pl.BlockSpec((K_B - 1, n_seq, LANES), lambda c: (0, 0, cc(c))),
        ],
        out_shape=[
            jax.ShapeDtypeStruct((rows, d), BF16),
            jax.ShapeDtypeStruct((K_A - 1, n_seq, C_A), F32),
            jax.ShapeDtypeStruct((K_B - 1, n_seq, C_B), F32),
        ],
        scratch_shapes=[
            pltpu.VMEM((rows, d), BF16),
            pltpu.VMEM((N_CHUNK, rows, LANES), F32),
            pltpu.VMEM((N_CHUNK, rows, LANES), F32),
        ],
        compiler_params=pltpu.CompilerParams(
            dimension_semantics=("arbitrary",), vmem_limit_bytes=VMEM_LIMIT),
        name="mixer_sample",
    )(xs_t, g1, win_bf, win_bf, win_bf, win_bf, win_bf, sa_t, sb_t, caw, cbw, cbb, lng, lnb)


def _post_mixer_kernel(yp_ref, ys_ref, xp_ref, xs_ref, wout_ref, g2_ref, wr_ref, br_ref,
                       x1_ref, xn_ref, idx_ref, gate_ref, rank_ref, cnt_ref,
                       run_ref, tri_ref, *, n_prompt_tiles):
    i = pl.program_id(0)
    tm = yp_ref.shape[0]

    @pl.when(i == 0)
    def _():
        run_ref[...] = jnp.zeros(run_ref.shape, F32)
        s_io = lax.broadcasted_iota(jnp.int32, (tm, tm), 0)
        t_io = lax.broadcasted_iota(jnp.int32, (tm, tm), 1)
        tri_ref[...] = (s_io < t_io).astype(F32).astype(BF16)

    is_prompt = i < n_prompt_tiles
    x = jnp.where(is_prompt, xp_ref[...], xs_ref[...])
    y = jnp.where(is_prompt, yp_ref[...], ys_ref[...])
    x1 = x + jnp.dot(y, wout_ref[...], preferred_element_type=F32)
    x1_ref[...] = x1
    xn = _rms_rows(x1, g2_ref[...])
    _store_row_tiles(xn_ref, 0, _pack_bf16_pairs(xn))

    logits = lax.dot_general(wr_ref[...], xn, (((1,), (1,)), ((), ())),
                             precision=lax.Precision.HIGHEST, preferred_element_type=F32) + br_ref[...]
    eidx = lax.broadcasted_iota(jnp.int32, logits.shape, 0)
    work = logits
    vals, sels, hots = [], [], []
    for _ in range(TOP_K):
        m = jnp.max(work, axis=0, keepdims=True)
        sel = jnp.min(jnp.where(work == m, eidx, N_EXPERTS), axis=0, keepdims=True)
        hot = eidx == sel
        vals.append(m)
        sels.append(sel)
        hots.append(hot)
        work = jnp.where(hot, -jnp.inf, work)
    exps = [jnp.exp(v - vals[0]) for v in vals]
    denom = exps[0] + exps[1] + exps[2] + exps[3]
    for k in range(TOP_K):
        idx_ref[k:k + 1, :] = sels[k]
        gate_ref[k:k + 1, :] = exps[k] / denom

    chosen = (hots[0] | hots[1] | hots[2] | hots[3])
    chosen_bf = chosen.astype(F32).astype(BF16)
    before = jnp.dot(chosen_bf, tri_ref[...], preferred_element_type=F32) + run_ref[:, 0:1]
    for k in range(TOP_K):
        r = jnp.sum(jnp.where(hots[k], before, 0.0), axis=0, keepdims=True)
        rank_ref[k:k + 1, :] = r.astype(jnp.int32)
    run_ref[...] = run_ref[...] + jnp.sum(chosen.astype(F32), axis=1, keepdims=True)
    cnt_ref[...] = run_ref[...].astype(jnp.int32)


def _post_mixer(yp, ys, xp2, xs_t, wout_bf, g2, wr_t, br_col):
    d = yp.shape[1]
    n = yp.shape[0] + ys.shape[0]
    tm = TM_POST
    n_tiles = n // tm
    n_pt = xp2.shape[0] // tm
    kern = functools.partial(_post_mixer_kernel, n_prompt_tiles=n_pt)
    full = lambda shape: pl.BlockSpec(shape, lambda i: (0,) * len(shape))
    return pl.pallas_call(
        kern,
        grid=(n_tiles,),
        in_specs=[
            pl.BlockSpec((tm, d), lambda i: (jnp.minimum(i, n_pt - 1), 0)),
            pl.BlockSpec((tm, d), lambda i: (jnp.maximum(i - n_pt, 0), 0)),
            pl.BlockSpec((tm, d), lambda i: (jnp.minimum(i, n_pt - 1), 0)),
            pl.BlockSpec((tm, d), lambda i: (jnp.maximum(i - n_pt, 0), 0)),
            pl.BlockSpec(wout_bf.shape, lambda i: (0, 0), pipeline_mode=pl.Buffered(1)),
            full((1, d)),
            full((N_EXPERTS, d)),
            full((N_EXPERTS, 1)),
        ],
        out_specs=[
            pl.BlockSpec((tm, d), lambda i: (i, 0)),
            pl.BlockSpec((tm * ROW_SUB, LANES), lambda i: (i, 0)),
            pl.BlockSpec((TOP_K, tm), lambda i: (0, i)),
            pl.BlockSpec((TOP_K, tm), lambda i: (0, i)),
            pl.BlockSpec((TOP_K, tm), lambda i: (0, i)),
            full((N_EXPERTS, LANES)),
        ],
        out_shape=[
            jax.ShapeDtypeStruct((n, d), F32),
            jax.ShapeDtypeStruct((n * ROW_SUB, LANES), jnp.uint32),
            jax.ShapeDtypeStruct((TOP_K, n), jnp.int32),
            jax.ShapeDtypeStruct((TOP_K, n), F32),
            jax.ShapeDtypeStruct((TOP_K, n), jnp.int32),
            jax.ShapeDtypeStruct((N_EXPERTS, LANES), jnp.int32),
        ],
        scratch_shapes=[pltpu.VMEM((N_EXPERTS, LANES), F32), pltpu.VMEM((tm, tm), BF16)],
        compiler_params=pltpu.CompilerParams(
            dimension_semantics=("arbitrary",), vmem_limit_bytes=VMEM_LIMIT),
        name="post_mixer",
    )(yp, ys, xp2, xs_t, wout_bf, g2, wr_t, br_col)


def _row_tile(ref, row):
    return ref.at[pl.ds(pl.multiple_of(row * ROW_SUB, ROW_SUB), ROW_SUB)]


def _dispatch_kernel(pos_ref, zf_ref, x_ref, xs_hbm, zbuf, sem, zsem):
    i = pl.program_id(0)
    tc = x_ref.shape[0] // ROW_SUB
    zrows = zbuf.shape[0]
    n_tiles = xs_hbm.shape[0] // zrows

    def zero_copy(t):
        return pltpu.make_async_copy(zbuf, xs_hbm.at[pl.ds(pl.multiple_of(t * zrows, zrows), zrows)], zsem)

    @pl.when(i == 0)
    def _():
        zbuf[...] = jnp.zeros(zbuf.shape, zbuf.dtype)

        def start(t, carry):
            @pl.when(zf_ref[t] != 0)
            def _():
                zero_copy(t).start()
            return carry
        lax.fori_loop(0, n_tiles, start, 0)

        def wait(t, carry):
            @pl.when(zf_ref[t] != 0)
            def _():
                zero_copy(t).wait()
            return carry
        lax.fori_loop(0, n_tiles, wait, 0)

    base = i * (TOP_K * tc)

    def body(g, carry):
        r0 = g * ISSUE_UNROLL
        for u in range(ISSUE_UNROLL):
            for k in range(TOP_K):
                p = pos_ref[base + k * tc + r0 + u]
                pltpu.make_async_copy(_row_tile(x_ref, r0 + u), _row_tile(xs_hbm, p), sem).start(
                    priority=(u * TOP_K + k) % 2)
        return carry
    lax.fori_loop(0, tc // ISSUE_UNROLL, body, 0)
    for k in range(TOP_K):
        pltpu.make_async_copy(x_ref, xs_hbm.at[pl.ds(0, tc * ROW_SUB)], sem).wait()


def _dispatch(pos_tiles, zero_flags, xn_tiles, p_max):
    n = xn_tiles.shape[0] // ROW_SUB
    tc = TC_DISP
    return pl.pallas_call(
        _dispatch_kernel,
        grid_spec=pltpu.PrefetchScalarGridSpec(
            num_scalar_prefetch=2,
            grid=(n // tc,),
            in_specs=[pl.BlockSpec((tc * ROW_SUB, LANES), lambda i, p, z: (i, 0))],
            out_specs=pl.BlockSpec(memory_space=pl.ANY),
            scratch_shapes=[pltpu.VMEM((TM_MOE * ROW_SUB, LANES), jnp.uint32), pltpu.SemaphoreType.DMA(()),
                            pltpu.SemaphoreType.DMA(())],
        ),
        out_shape=jax.ShapeDtypeStruct((p_max * ROW_SUB, LANES), jnp.uint32),
        compiler_params=pltpu.CompilerParams(
            dimension_semantics=("arbitrary",), vmem_limit_bytes=VMEM_LIMIT),
        name="dispatch",
    )(pos_tiles, zero_flags, xn_tiles)


_FLAG_NEW_WEIGHTS = 1
WEIGHT_DMA_PRIORITY = 1


def _advance_weights(te_ref, ne_ref, fl_ref, slot_ref, copies):
    j = pl.program_id(0)
    t = pl.program_id(1)
    n_j = pl.num_programs(0)

    @pl.when((fl_ref[t] & _FLAG_NEW_WEIGHTS) != 0)
    def _():
        first = (j == 0) & (t == 0)

        @pl.when(first)
        def _():
            slot_ref[0] = 0
            for c in copies(te_ref[t], j, 0):
                c.start(priority=WEIGHT_DMA_PRIORITY)

        @pl.when(jnp.logical_not(first))
        def _():
            slot_ref[0] = 1 - slot_ref[0]

        s = slot_ref[0]
        for c in copies(te_ref[t], j, s):
            c.wait()
        ne = ne_ref[t]

        @pl.when(ne >= 0)
        def _():
            for c in copies(ne, j, 1 - s):
                c.start(priority=WEIGHT_DMA_PRIORITY)

        @pl.when((ne < 0) & (j + 1 < n_j))
        def _():
            for c in copies(te_ref[0], j + 1, 1 - s):
                c.start(priority=WEIGHT_DMA_PRIORITY)


def _by_valid_rows(n_valid, compute, zero_from):
    for rows in range(0, TM_MOE + 1, SUB_MOE):
        @pl.when((n_valid > rows - SUB_MOE) & (n_valid <= rows))
        def _():
            if rows > 0:
                compute(rows)
            if rows < TM_MOE:
                zero_from(rows)


def _bf16_dot(x_bf, w_f32):
    return lax.dot_general(x_bf, w_f32, (((1,), (0,)), ((), ())), preferred_element_type=F32)


def _moe_up_kernel(te_ref, tb_ref, fl_ref, ne_ref, nv_ref, x_ref, bg_ref, bu_ref, w_hbm, h_ref, wbuf, sem,
                   slot_ref):
    t = pl.program_id(1)
    n_valid = nv_ref[t]

    def copies(e, j, slot):
        cg = pl.multiple_of(j * BF_UP, BF_UP)
        cu = pl.multiple_of(D_FF + j * BF_UP, BF_UP)
        return (pltpu.make_async_copy(w_hbm.at[e, :, pl.ds(cg, BF_UP)], wbuf.at[slot, 0], sem.at[slot]),
                pltpu.make_async_copy(w_hbm.at[e, :, pl.ds(cu, BF_UP)], wbuf.at[slot, 1], sem.at[slot]))

    _advance_weights(te_ref, ne_ref, fl_ref, slot_ref, copies)

    def compute(n_rows):
        s = slot_ref[0]
        halves = [_unpack_hi_lo(c) for c in _load_row_tiles(x_ref, 0, n_rows)]
        x = jnp.concatenate([hl[0].astype(BF16) for hl in halves] + [hl[1].astype(BF16) for hl in halves], axis=1)
        gate = _bf16_dot(x, wbuf[s, 0]) + bg_ref[...]
        up = _bf16_dot(x, wbuf[s, 1]) + bu_ref[...]
        gate = jnp.minimum(gate, SWIGLU_LIMIT)
        up = jnp.clip(up, -SWIGLU_LIMIT, SWIGLU_LIMIT)
        act = gate * _sigmoid(SWIGLU_ALPHA * gate) * (up + 1.0)
        h_ref[0:n_rows, :] = act.astype(BF16)

    def zero_from(r0):
        h_ref[r0:TM_MOE, :] = jnp.zeros((TM_MOE - r0, h_ref.shape[1]), BF16)

    _by_valid_rows(n_valid, compute, zero_from)


def _moe_up(te, tb, fl, ne, nv, xs, w_gate_up, b_gate_up3):
    p_max = xs.shape[0] // ROW_SUB
    d = D_MODEL
    tm = TM_MOE
    n_tiles = p_max // tm
    n_j = D_FF // BF_UP
    return pl.pallas_call(
        _moe_up_kernel,
        grid_spec=pltpu.PrefetchScalarGridSpec(
            num_scalar_prefetch=5,
            grid=(n_j, n_tiles),
            in_specs=[
                pl.BlockSpec((tm * ROW_SUB, LANES), lambda j, t, te, tb, fl, ne, nv: (tb[t], 0)),
                pl.BlockSpec((None, 1, BF_UP), lambda j, t, te, tb, fl, ne, nv: (te[t], 0, j)),
                pl.BlockSpec((None, 1, BF_UP), lambda j, t, te, tb, fl, ne, nv: (te[t], 0, n_j + j)),
                pl.BlockSpec(memory_space=pl.ANY),
            ],
            out_specs=pl.BlockSpec((tm, BF_UP), lambda j, t, te, tb, fl, ne, nv: (t, j)),
            scratch_shapes=[pltpu.VMEM((2, 2, d, BF_UP), F32), pltpu.SemaphoreType.DMA((2,)),
                            pltpu.SMEM((1,), jnp.int32)],
        ),
        out_shape=jax.ShapeDtypeStruct((p_max, D_FF), BF16),
        compiler_params=pltpu.CompilerParams(
            dimension_semantics=("arbitrary", "arbitrary"), vmem_limit_bytes=VMEM_LIMIT),
        name="moe_up",
    )(te, tb, fl, ne, nv, xs, b_gate_up3, b_gate_up3, w_gate_up)


def _moe_down_kernel(te_ref, tb_ref, fl_ref, ne_ref, nv_ref, h_ref, bd_ref, w_hbm, y_ref, wbuf, sem, slot_ref):
    t = pl.program_id(1)
    n_valid = nv_ref[t]

    def copies(e, j, slot):
        c0 = pl.multiple_of(j * BN_DOWN, BN_DOWN)
        return (pltpu.make_async_copy(w_hbm.at[e, :, pl.ds(c0, BN_DOWN)], wbuf.at[slot], sem.at[slot]),)

    _advance_weights(te_ref, ne_ref, fl_ref, slot_ref, copies)

    def compute(n_rows):
        y = _bf16_dot(h_ref[0:n_rows, :], wbuf[slot_ref[0]]) + bd_ref[...]
        _store_row_tiles(y_ref, 0, _pack_bf16_pairs(y))

    def zero_from(r0):
        y_ref[r0 * ROW_SUB:TM_MOE * ROW_SUB, :] = jnp.zeros(((TM_MOE - r0) * ROW_SUB, LANES), y_ref.dtype)

    _by_valid_rows(n_valid, compute, zero_from)


def _moe_down(te, tb, fl, ne, nv, h, w_down, b_down3):
    p_max, f = h.shape
    tm = TM_MOE
    n_tiles = p_max // tm
    n_j = D_MODEL // BN_DOWN
    return pl.pallas_call(
        _moe_down_kernel,
        grid_spec=pltpu.PrefetchScalarGridSpec(
            num_scalar_prefetch=5,
            grid=(n_j, n_tiles),
            in_specs=[
                pl.BlockSpec((tm, f), lambda j, t, te, tb, fl, ne, nv: (tb[t], 0)),
                pl.BlockSpec((None, 1, BN_DOWN), lambda j, t, te, tb, fl, ne, nv: (te[t], 0, j)),
                pl.BlockSpec(memory_space=pl.ANY),
            ],
            out_specs=pl.BlockSpec((tm * ROW_SUB, LANES), lambda j, t, te, tb, fl, ne, nv: (t, 0)),
            scratch_shapes=[pltpu.VMEM((2, f, BN_DOWN), F32), pltpu.SemaphoreType.DMA((2,)),
                            pltpu.SMEM((1,), jnp.int32)],
        ),
        out_shape=jax.ShapeDtypeStruct((p_max * ROW_SUB, LANES), jnp.uint32),
        compiler_params=pltpu.CompilerParams(
            dimension_semantics=("arbitrary", "arbitrary"), vmem_limit_bytes=VMEM_LIMIT),
        name="moe_down",
    )(te, tb, fl, ne, nv, h, b_down3, w_down)


def _combine_kernel(pos_ref, x1_ref, gate_ref, gf_ref, y_hbm, op_ref, os_ref, buf, sem, *, n_prompt_tiles):
    i = pl.program_id(0)
    n_i = pl.num_programs(0)
    tc = x1_ref.shape[0]
    rows = TOP_K * tc

    def issue(tile, slot):
        base = tile * rows

        def body(g, carry):
            r0 = g * ISSUE_UNROLL
            for u in range(ISSUE_UNROLL):
                p = pos_ref[base + r0 + u]
                pltpu.make_async_copy(_row_tile(y_hbm, p), _row_tile(buf.at[slot], r0 + u),
                                      sem.at[slot]).start(priority=u % 2)
            return carry
        lax.fori_loop(0, rows // ISSUE_UNROLL, body, 0)

    @pl.when(i == 0)
    def _():
        issue(0, 0)

    @pl.when(i + 1 < n_i)
    def _():
        issue(i + 1, (i + 1) % 2)

    slot = i % 2
    pltpu.make_async_copy(y_hbm.at[pl.ds(0, rows * ROW_SUB)], buf.at[slot], sem.at[slot]).wait()
    his = [None] * ROW_SUB
    los = [None] * ROW_SUB
    for k in range(TOP_K):
        g = gate_ref[:, k:k + 1]
        for s, chunk in enumerate(_load_row_tiles(buf.at[slot], k * tc, tc)):
            hi, lo = _unpack_hi_lo(chunk)
            his[s] = g * hi if k == 0 else his[s] + g * hi
            los[s] = g * lo if k == 0 else los[s] + g * lo
    acc = x1_ref[...] + jnp.concatenate(his + los, axis=1)
    out = _rms_rows(acc, gf_ref[...])

    @pl.when(i < n_prompt_tiles)
    def _():
        op_ref[...] = out

    @pl.when(i >= n_prompt_tiles)
    def _():
        os_ref[...] = out


def _combine(pos_flat, x1, gates_nk, gf, y_sorted, n_prompt):
    n, d = x1.shape
    tc = TC_COMB
    n_tiles = n // tc
    n_pt = n_prompt // tc
    kern = functools.partial(_combine_kernel, n_prompt_tiles=n_pt)
    return pl.pallas_call(
        kern,
        grid_spec=pltpu.PrefetchScalarGridSpec(
            num_scalar_prefetch=1,
            grid=(n_tiles,),
            in_specs=[
                pl.BlockSpec((tc, d), lambda i, p: (i, 0)),
                pl.BlockSpec((tc, TOP_K), lambda i, p: (i, 0)),
                pl.BlockSpec((1, d), lambda i, p: (0, 0)),
                pl.BlockSpec(memory_space=pl.ANY),
            ],
            out_specs=[
                pl.BlockSpec((tc, d), lambda i, p: (jnp.minimum(i, n_pt - 1), 0)),
                pl.BlockSpec((tc, d), lambda i, p: (jnp.maximum(i - n_pt, 0), 0)),
            ],
            scratch_shapes=[pltpu.VMEM((2, TOP_K * tc * ROW_SUB, LANES), jnp.uint32),
                            pltpu.SemaphoreType.DMA((2,))],
        ),
        out_shape=[
            jax.ShapeDtypeStruct((n_prompt, d), F32),
            jax.ShapeDtypeStruct((n - n_prompt, d), F32),
        ],
        compiler_params=pltpu.CompilerParams(
            dimension_semantics=("arbitrary",), vmem_limit_bytes=VMEM_LIMIT),
        name="combine",
    )(pos_flat, x1, gates_nk, gf, y_sorted)


def kernel(x_prompt, x_sample, state_conv_a, state_conv_b, meta_tokens, norm1_g, w_in, conv_a_w, conv_b_w,
           conv_b_b, ln_b_g, ln_b_b, w_out, norm2_g, w_router, b_router, w_gate_up, b_gate_up, w_down,
           b_down, final_norm_g):
    bp, seq, d = x_prompt.shape
    n_seq, n_t, _ = x_sample.shape
    n_prompt = bp * seq
    n_sample = n_seq * n_t
    n = n_prompt + n_sample
    assert norm1_g.shape[0] == 1, "single layer"
    assert seq % T_MIX == 0 and n_prompt % TM_POST == 0 and n_sample == TM_POST
    assert n_prompt % n_sample == 0 and n % TC_COMB == 0 and n_prompt % TC_COMB == 0

    g1 = norm1_g[0][None]
    win_bf = w_in[0].astype(BF16)
    wout_bf = w_out[0].astype(BF16)
    caw, cbw = conv_a_w[0], conv_b_w[0]
    cbb, lng, lnb = conv_b_b[0][None], ln_b_g[0][None], ln_b_b[0][None]

    xs_t = jnp.transpose(x_sample, (1, 0, 2)).reshape(n_sample, d)
    sa_t = jnp.transpose(state_conv_a[0], (1, 0, 2))
    sb_t = jnp.transpose(state_conv_b[0], (1, 0, 2))

    ymix_p, pa, pb = _mixer_prompt(x_prompt, meta_tokens, g1, win_bf, caw, cbw, cbb, lng, lnb)
    ymix_s, na_t, nb_t = _mixer_sample(xs_t, g1, win_bf, sa_t, sb_t, caw, cbw, cbb, lng, lnb, n_seq, n_t)

    x1, xn, idx, gates, rank, cnt = _post_mixer(
        ymix_p, ymix_s, x_prompt.reshape(n_prompt, d), xs_t, wout_bf, norm2_g[0][None],
        jnp.transpose(w_router[0]), b_router[0][:, None])

    tm = TM_MOE
    n_assign = n * TOP_K
    n_tiles = (n_assign + N_EXPERTS * (tm - 1)) // tm
    p_max = n_tiles * tm
    counts = cnt[:, 0]
    tiles_per_e = (counts + tm - 1) // tm
    tile_end = jnp.cumsum(tiles_per_e)
    tile_start = tile_end - tiles_per_e
    n_used = tile_end[-1]
    e_ar = jnp.arange(N_EXPERTS, dtype=jnp.int32)
    start_of = jnp.sum(jnp.where(idx[None] == e_ar[:, None, None], tile_start[:, None, None], 0), axis=0)
    pos = (start_of * tm + rank).astype(jnp.int32)
    tid = jnp.arange(n_tiles, dtype=jnp.int32)
    tb = jnp.maximum(jnp.minimum(tid, n_used - 1), 0).astype(jnp.int32)
    expert_of = lambda tile: jnp.minimum(
        jnp.sum((tile_end[None, :] <= tile[:, None]).astype(jnp.int32), axis=1), N_EXPERTS - 1).astype(jnp.int32)
    te = expert_of(tb)
    valid = tid < n_used
    new_w = valid & ((tid == 0) | (te != jnp.roll(te, 1)))
    fl = new_w.astype(jnp.int32) * _FLAG_NEW_WEIGHTS
    mine = te[:, None] == e_ar[None, :]
    rows_left = (jnp.sum(jnp.where(mine, counts[None, :], 0), axis=1)
                 - (tid - jnp.sum(jnp.where(mine, tile_start[None, :], 0), axis=1)) * tm)
    nv = jnp.where(valid, jnp.clip(rows_left, 0, tm), 0).astype(jnp.int32)
    next_tile = jnp.sum(jnp.where(te[:, None] == e_ar[None, :], tile_end[None, :], 0), axis=1)
    ne = jnp.where(next_tile < n_used, expert_of(next_tile), -1).astype(jnp.int32)
    partial = (counts % tm) != 0
    zero_flags = ((tid >= n_used) | jnp.any((tid[:, None] == (tile_end - 1)[None, :]) & partial[None, :],
                                            axis=1)).astype(jnp.int32)

    by_tile = lambda tc: pos.reshape(TOP_K, n // tc, tc).transpose(1, 0, 2).reshape(-1)
    xs_sorted = _dispatch(by_tile(TC_DISP), zero_flags, xn, p_max)
    h = _moe_up(te, tb, fl, ne, nv, xs_sorted, w_gate_up[0], b_gate_up[0][:, None, :])
    y_sorted = _moe_down(te, tb, fl, ne, nv, h, w_down[0], b_down[0][:, None, :])

    yp, ys_t = _combine(by_tile(TC_COMB), x1, jnp.transpose(gates), final_norm_g[None], y_sorted, n_prompt)

    y_prompt = yp.reshape(bp, seq, d)
    y_sample = jnp.transpose(ys_t.reshape(n_t, n_seq, d), (1, 0, 2))
    new_a_prompt = pa[None]
    new_b_prompt = pb[None]
    new_a_sample = jnp.transpose(na_t, (1, 0, 2))[None]
    new_b_sample = jnp.transpose(nb_t, (1, 0, 2))[None]
    return (y_prompt, y_sample, new_a_prompt, new_b_prompt, new_a_sample, new_b_sample)
```

```python
import functools

import jax
import jax.numpy as jnp
from jax import lax
from jax.experimental import pallas as pl
from jax.experimental.pallas import tpu as pltpu

F32 = jnp.float32
BF16 = jnp.bfloat16

D_MODEL = 2048
N_META = 16
C_A = 1024
C_B = 1024
K_A = 3
K_B = 31
N_EXPERTS = 32
TOP_K = 4
D_FF = 2048
SWIGLU_LIMIT = 7.0
SWIGLU_ALPHA = 1.702
RMS_EPS = 1e-5
LN_EPS = 1e-5

LANES = 128
N_CHUNK = C_B // LANES
HALO_B = 32
HALO_A = 8
T_MIX = 512
TM_POST = 512
TM_MOE = 512
SUB_MOE = 128
BF_UP = 1024
BN_DOWN = 2048
ROW_SUB = D_MODEL // 2 // LANES
TC_COMB = 256
TC_DISP = 512
ISSUE_UNROLL = 16
VMEM_LIMIT = 56 * 1024 * 1024
VMEM_LIMIT_POST = 60 * 1024 * 1024


def _sigmoid(x):
    return jax.nn.sigmoid(x)


def _rms_rows(x, g):
    ms = jnp.mean(x * x, axis=-1, keepdims=True)
    return (x * lax.rsqrt(ms + RMS_EPS)) * g


def _pack_bf16_pairs(x):
    c = x.shape[1] // 2
    hi = lax.bitcast_convert_type(x[:, :c].astype(BF16).astype(F32), jnp.uint32)
    lo = lax.bitcast_convert_type(x[:, c:].astype(BF16).astype(F32), jnp.uint32)
    return hi | (lo >> 16)


def _unpack_hi_lo(p):
    hi = lax.bitcast_convert_type(p & jnp.uint32(0xFFFF0000), F32)
    lo = lax.bitcast_convert_type(p << 16, F32)
    return hi, lo


def _store_row_tiles(ref, first_row, packed):
    r = packed.shape[0]
    for s in range(ROW_SUB):
        ref[pl.ds(first_row * ROW_SUB + s, r, stride=ROW_SUB), :] = packed[:, s * LANES:(s + 1) * LANES]


def _load_row_tiles(ref, first_row, r):
    return [ref[pl.ds(first_row * ROW_SUB + s, r, stride=ROW_SUB), :] for s in range(ROW_SUB)]


def _mixer_prompt_kernel(x_ref, meta_ref, g1_ref, win_ref, caw_ref, cbw_ref, cbb_ref, lng_ref, lnb_ref,
                         y_ref, pa_ref, pb_ref,
                         u_ref, gb_ref, cv_ref, mgb_ref, mcv_ref, bc_ref):
    b = pl.program_id(0)
    i = pl.program_id(1)
    n_i = pl.num_programs(1)
    t = T_MIX

    def in_proj(rows):
        h = _rms_rows(rows, g1_ref[...]).astype(BF16)
        return jnp.dot(h, win_ref[...], preferred_element_type=F32)

    @pl.when((b == 0) & (i == 0))
    def _():
        um = in_proj(meta_ref[...])
        cvm = um[:, C_A:2 * C_A] * um[:, 2 * C_A:3 * C_A]
        glum = um[:, 3 * C_A:3 * C_A + C_B] * _sigmoid(um[:, 3 * C_A + C_B:])
        for c in range(N_CHUNK):
            sl = slice(c * LANES, (c + 1) * LANES)
            mgb_ref[c, 0:HALO_B - N_META, :] = jnp.zeros((HALO_B - N_META, LANES), F32)
            mgb_ref[c, HALO_B - N_META:HALO_B, :] = glum[:, sl]
            mcv_ref[c] = cvm[N_META - HALO_A:, sl]

    @pl.when(i == 0)
    def _():
        gb_ref[:, 0:HALO_B, :] = mgb_ref[...]
        cv_ref[:, 0:HALO_A, :] = mcv_ref[...]

    u_ref[...] = in_proj(x_ref[...])

    for c in range(N_CHUNK):
        lo = c * LANES
        bg = u_ref[:, lo:lo + LANES]
        cg = u_ref[:, C_A + lo:C_A + lo + LANES]
        v = u_ref[:, 2 * C_A + lo:2 * C_A + lo + LANES]
        ga = u_ref[:, 3 * C_A + lo:3 * C_A + lo + LANES]
        gbv = u_ref[:, 3 * C_A + C_B + lo:3 * C_A + C_B + lo + LANES]
        cv_ref[c, HALO_A:, :] = cg * v
        gb_ref[c, HALO_B:, :] = ga * _sigmoid(gbv)
        acc = cv_ref[c, HALO_A - 2:HALO_A - 2 + t, :] * caw_ref[0:1, lo:lo + LANES]
        for k in range(1, K_A):
            acc = acc + cv_ref[c, HALO_A - 2 + k:HALO_A - 2 + k + t, :] * caw_ref[k:k + 1, lo:lo + LANES]
        y_ref[:, lo:lo + LANES] = (bg * acc).astype(BF16)
        off = HALO_B - (K_B - 1)
        accb = gb_ref[c, off:off + t, :] * cbw_ref[0:1, lo:lo + LANES]
        for k in range(1, K_B):
            accb = accb + gb_ref[c, off + k:off + k + t, :] * cbw_ref[k:k + 1, lo:lo + LANES]
        bc_ref[:, lo:lo + LANES] = accb + cbb_ref[:, lo:lo + LANES]

    rb = 64
    for r in range(t // rb):
        xb = bc_ref[r * rb:(r + 1) * rb, :]
        mu = jnp.mean(xb, axis=-1, keepdims=True)
        var = jnp.mean(jnp.square(xb - mu), axis=-1, keepdims=True)
        bn = (xb - mu) * lax.rsqrt(var + LN_EPS) * lng_ref[...] + lnb_ref[...]
        y_ref[r * rb:(r + 1) * rb, C_A:] = (bn * _sigmoid(bn)).astype(BF16)

    @pl.when(i == n_i - 1)
    def _():
        for c in range(N_CHUNK):
            sl = slice(c * LANES, (c + 1) * LANES)
            pa_ref[:, sl] = cv_ref[c, HALO_A + t - (K_A - 1):HALO_A + t, :]
            pb_ref[:, sl] = gb_ref[c, HALO_B + t - (K_B - 1):HALO_B + t, :]

    gb_ref[:, 0:HALO_B, :] = gb_ref[:, t:t + HALO_B, :]
    cv_ref[:, 0:HALO_A, :] = cv_ref[:, t:t + HALO_A, :]


def _mixer_prompt(x_prompt, meta, g1, win_bf, caw, cbw, cbb, lng, lnb):
    bp, seq, d = x_prompt.shape
    n_i = seq // T_MIX
    full = lambda shape: pl.BlockSpec(shape, lambda b, i: (0,) * len(shape))
    return pl.pallas_call(
        _mixer_prompt_kernel,
        grid=(bp, n_i),
        in_specs=[
            pl.BlockSpec((None, T_MIX, d), lambda b, i: (b, i, 0)),
            full((N_META, d)),
            full((1, d)),
            pl.BlockSpec(win_bf.shape, lambda b, i: (0, 0), pipeline_mode=pl.Buffered(1)),
            full((K_A, C_A)),
            full((K_B, C_B)),
            full((1, C_B)),
            full((1, C_B)),
            full((1, C_B)),
        ],
        out_specs=[
            pl.BlockSpec((T_MIX, d), lambda b, i: (b * n_i + i, 0)),
            pl.BlockSpec((None, K_A - 1, C_A), lambda b, i: (b, 0, 0)),
            pl.BlockSpec((None, K_B - 1, C_B), lambda b, i: (b, 0, 0)),
        ],
        out_shape=[
            jax.ShapeDtypeStruct((bp * seq, d), BF16),
            jax.ShapeDtypeStruct((bp, K_A - 1, C_A), F32),
            jax.ShapeDtypeStruct((bp, K_B - 1, C_B), F32),
        ],
        scratch_shapes=[
            pltpu.VMEM((T_MIX, win_bf.shape[1]), F32),
            pltpu.VMEM((N_CHUNK, T_MIX + HALO_B, LANES), F32),
            pltpu.VMEM((N_CHUNK, T_MIX + HALO_A, LANES), F32),
            pltpu.VMEM((N_CHUNK, HALO_B, LANES), F32),
            pltpu.VMEM((N_CHUNK, HALO_A, LANES), F32),
            pltpu.VMEM((T_MIX, C_B), F32),
        ],
        compiler_params=pltpu.CompilerParams(
            dimension_semantics=("arbitrary", "arbitrary"), vmem_limit_bytes=VMEM_LIMIT),
        name="mixer_prompt",
    )(x_prompt, meta, g1, win_bf, caw, cbw, cbb, lng, lnb)


def _mixer_sample_kernel(x_ref, g1_ref, wbg_ref, wcg_ref, wv_ref, wga_ref, wgb_ref, sa_ref, sb_ref,
                         caw_ref, cbw_ref, cbb_ref, lng_ref, lnb_ref,
                         y_ref, na_ref, nb_ref,
                         h_ref, ya_ref, bc_ref, *, n_seq, n_t):
    c = pl.program_id(0)

    @pl.when(c == 0)
    def _():
        h_ref[...] = _rms_rows(x_ref[...], g1_ref[...]).astype(BF16)

    @pl.when(c < N_CHUNK)
    def _():
        h = h_ref[...]
        proj = lambda w_ref: jnp.dot(h, w_ref[...], preferred_element_type=F32)
        bg, cg, v, ga, gbv = proj(wbg_ref), proj(wcg_ref), proj(wv_ref), proj(wga_ref), proj(wgb_ref)
        cv = cg * v
        glu = ga * _sigmoid(gbv)
        row = lambda a, tt: a[tt * n_seq:(tt + 1) * n_seq, :]
        xa = [sa_ref[j] for j in range(K_A - 1)] + [row(cv, tt) for tt in range(n_t)]
        xb = [sb_ref[j] for j in range(K_B - 1)] + [row(glu, tt) for tt in range(n_t)]
        for tt in range(n_t):
            acc = xa[tt] * caw_ref[0:1, :]
            for k in range(1, K_A):
                acc = acc + xa[tt + k] * caw_ref[k:k + 1, :]
            ya_ref[c, tt * n_seq:(tt + 1) * n_seq, :] = row(bg, tt) * acc
            accb = xb[tt] * cbw_ref[0:1, :]
            for k in range(1, K_B):
                accb = accb + xb[tt + k] * cbw_ref[k:k + 1, :]
            bc_ref[c, tt * n_seq:(tt + 1) * n_seq, :] = accb + cbb_ref[...]
        for j in range(K_B - 1):
            nb_ref[j] = xb[n_t + j]
        for j in range(K_A - 1):
            na_ref[j] = row(cv, n_t - (K_A - 1) + j)

    @pl.when(c == N_CHUNK)
    def _():
        xb = jnp.concatenate([bc_ref[cc] for cc in range(N_CHUNK)], axis=1)
        mu = jnp.mean(xb, axis=-1, keepdims=True)
        var = jnp.mean(jnp.square(xb - mu), axis=-1, keepdims=True)
        bn = (xb - mu) * lax.rsqrt(var + LN_EPS) * lng_ref[...] + lnb_ref[...]
        for cc in range(N_CHUNK):
            y_ref[:, cc * LANES:(cc + 1) * LANES] = ya_ref[cc].astype(BF16)
        y_ref[:, C_A:] = (bn * _sigmoid(bn)).astype(BF16)


def _mixer_sample(xs_t, g1, win_bf, sa_t, sb_t, caw, cbw, cbb, lng, lnb, n_seq, n_t):
    rows, d = xs_t.shape
    cc = lambda c: jnp.minimum(c, N_CHUNK - 1)
    wspec = lambda g: pl.BlockSpec((d, LANES), lambda c, g=g: (0, g * N_CHUNK + cc(c)))
    full = lambda shape: pl.BlockSpec(shape, lambda c: (0,) * len(shape))
    kern = functools.partial(_mixer_sample_kernel, n_seq=n_seq, n_t=n_t)
    return pl.pallas_call(
        kern,
        grid=(N_CHUNK + 1,),
        in_specs=[
            full((rows, d)),
            full((1, d)),
            wspec(0), wspec(1), wspec(2), wspec(3), wspec(4),
            pl.BlockSpec((K_A - 1, n_seq, LANES), lambda c: (0, 0, cc(c))),
            pl.BlockSpec((K_B - 1, n_seq, LANES), lambda c: (0, 0, cc(c))),
            pl.BlockSpec((K_A, LANES), lambda c: (0, cc(c))),
            pl.BlockSpec((K_B, LANES), lambda c: (0, cc(c))),
            pl.BlockSpec((1, LANES), lambda c: (0, cc(c))),
            full((1, C_B)),
            full((1, C_B)),
        ],
        out_specs=[
            full((rows, d)),
            pl.BlockSpec((K_A - 1, n_seq, LANES), lambda c: (0, 0, cc(c))),
            pl.BlockSpec((K_B - 1, n_seq, LANES), lambda c: (0, 0, cc(c))),
        ],
        out_shape=[
            jax.ShapeDtypeStruct((rows, d), BF16),
            jax.ShapeDtypeStruct((K_A - 1, n_seq, C_A), F32),
            jax.ShapeDtypeStruct((K_B - 1, n_seq, C_B), F32),
        ],
        scratch_shapes=[
            pltpu.VMEM((rows, d), BF16),
            pltpu.VMEM((N_CHUNK, rows, LANES), F32),
            pltpu.VMEM((N_CHUNK, rows, LANES), F32),
        ],
        compiler_params=pltpu.CompilerParams(
            dimension_semantics=("arbitrary",), vmem_limit_bytes=VMEM_LIMIT),
        name="mixer_sample",
    )(xs_t, g1, win_bf, win_bf, win_bf, win_bf, win_bf, sa_t, sb_t, caw, cbw, cbb, lng, lnb)


def _post_mixer_kernel(yp_ref, ys_ref, xp_ref, xs_ref, wout_ref, g2_ref, wr_ref, br_ref,
                       x1_ref, xn_ref, idx_ref, gate_ref, rank_ref, cnt_ref,
                       run_ref, tri_ref, *, n_prompt_tiles):
    i = pl.program_id(0)
    tm = yp_ref.shape[0]

    @pl.when(i == 0)
    def _():
        run_ref[...] = jnp.zeros(run_ref.shape, F32)
        s_io = lax.broadcasted_iota(jnp.int32, (tm, tm), 0)
        t_io = lax.broadcasted_iota(jnp.int32, (tm, tm), 1)
        tri_ref[...] = (s_io < t_io).astype(F32).astype(BF16)

    is_prompt = i < n_prompt_tiles
    x = jnp.where(is_prompt, xp_ref[...], xs_ref[...])
    y = jnp.where(is_prompt, yp_ref[...], ys_ref[...])
    x1 = x + lax.dot_general(y, wout_ref[...], (((1,), (0,)), ((), ())), preferred_element_type=F32)
    x1_ref[...] = x1
    xn = _rms_rows(x1, g2_ref[...])
    _store_row_tiles(xn_ref, 0, _pack_bf16_pairs(xn))

    logits = lax.dot_general(wr_ref[...], xn, (((1,), (1,)), ((), ())),
                             precision=lax.Precision.HIGHEST, preferred_element_type=F32) + br_ref[...]
    eidx = lax.broadcasted_iota(jnp.int32, logits.shape, 0)
    work = logits
    vals, sels, hots = [], [], []
    for _ in range(TOP_K):
        m = jnp.max(work, axis=0, keepdims=True)
        sel = jnp.min(jnp.where(work == m, eidx, N_EXPERTS), axis=0, keepdims=True)
        hot = eidx == sel
        vals.append(m)
        sels.append(sel)
        hots.append(hot)
        work = jnp.where(hot, -jnp.inf, work)
    exps = [jnp.exp(v - vals[0]) for v in vals]
    denom = exps[0] + exps[1] + exps[2] + exps[3]
    for k in range(TOP_K):
        idx_ref[k:k + 1, :] = sels[k]
        gate_ref[k:k + 1, :] = exps[k] / denom

    chosen = (hots[0] | hots[1] | hots[2] | hots[3])
    chosen_bf = chosen.astype(F32).astype(BF16)
    before = jnp.dot(chosen_bf, tri_ref[...], preferred_element_type=F32) + run_ref[:, 0:1]
    for k in range(TOP_K):
        r = jnp.sum(jnp.where(hots[k], before, 0.0), axis=0, keepdims=True)
        rank_ref[k:k + 1, :] = r.astype(jnp.int32)
    run_ref[...] = run_ref[...] + jnp.sum(chosen.astype(F32), axis=1, keepdims=True)
    cnt_ref[...] = run_ref[...].astype(jnp.int32)


def _post_mixer(yp, ys, xp2, xs_t, wout_bf, g2, wr_t, br_col):
    d = yp.shape[1]
    n = yp.shape[0] + ys.shape[0]
    tm = TM_POST
    n_tiles = n // tm
    n_pt = xp2.shape[0] // tm
    kern = functools.partial(_post_mixer_kernel, n_prompt_tiles=n_pt)
    full = lambda shape: pl.BlockSpec(shape, lambda i: (0,) * len(shape))
    return pl.pallas_call(
        kern,
        grid=(n_tiles,),
        in_specs=[
            pl.BlockSpec((tm, d), lambda i: (jnp.minimum(i, n_pt - 1), 0)),
            pl.BlockSpec((tm, d), lambda i: (jnp.maximum(i - n_pt, 0), 0)),
            pl.BlockSpec((tm, d), lambda i: (jnp.minimum(i, n_pt - 1), 0)),
            pl.BlockSpec((tm, d), lambda i: (jnp.maximum(i - n_pt, 0), 0)),
            pl.BlockSpec(wout_bf.shape, lambda i: (0, 0), pipeline_mode=pl.Buffered(1)),
            full((1, d)),
            full((N_EXPERTS, d)),
            full((N_EXPERTS, 1)),
        ],
        out_specs=[
            pl.BlockSpec((tm, d), lambda i: (i, 0)),
            pl.BlockSpec((tm * ROW_SUB, LANES), lambda i: (i, 0)),
            pl.BlockSpec((TOP_K, tm), lambda i: (0, i)),
            pl.BlockSpec((TOP_K, tm), lambda i: (0, i)),
            pl.BlockSpec((TOP_K, tm), lambda i: (0, i)),
            full((N_EXPERTS, LANES)),
        ],
        out_shape=[
            jax.ShapeDtypeStruct((n, d), F32),
            jax.ShapeDtypeStruct((n * ROW_SUB, LANES), jnp.uint32),
            jax.ShapeDtypeStruct((TOP_K, n), jnp.int32),
            jax.ShapeDtypeStruct((TOP_K, n), F32),
            jax.ShapeDtypeStruct((TOP_K, n), jnp.int32),
            jax.ShapeDtypeStruct((N_EXPERTS, LANES), jnp.int32),
        ],
        scratch_shapes=[pltpu.VMEM((N_EXPERTS, LANES), F32), pltpu.VMEM((tm, tm), BF16)],
        compiler_params=pltpu.CompilerParams(
            dimension_semantics=("arbitrary",), vmem_limit_bytes=VMEM_LIMIT_POST),
        name="post_mixer",
    )(yp, ys, xp2, xs_t, wout_bf, g2, wr_t, br_col)


def _row_tile(ref, row):
    return ref.at[pl.ds(pl.multiple_of(row * ROW_SUB, ROW_SUB), ROW_SUB)]


def _dispatch_kernel(pos_ref, zf_ref, x_ref, xs_hbm, zbuf, sem, zsem):
    i = pl.program_id(0)
    tc = x_ref.shape[0] // ROW_SUB
    zrows = zbuf.shape[0]
    n_tiles = xs_hbm.shape[0] // zrows

    def zero_copy(t):
        return pltpu.make_async_copy(zbuf, xs_hbm.at[pl.ds(pl.multiple_of(t * zrows, zrows), zrows)], zsem)

    @pl.when(i == 0)
    def _():
        zbuf[...] = jnp.zeros(zbuf.shape, zbuf.dtype)

        def start(t, carry):
            @pl.when(zf_ref[t] != 0)
            def _():
                zero_copy(t).start()
            return carry
        lax.fori_loop(0, n_tiles, start, 0)

        def wait(t, carry):
            @pl.when(zf_ref[t] != 0)
            def _():
                zero_copy(t).wait()
            return carry
        lax.fori_loop(0, n_tiles, wait, 0)

    base = i * (TOP_K * tc)

    def body(g, carry):
        r0 = g * ISSUE_UNROLL
        for u in range(ISSUE_UNROLL):
            for k in range(TOP_K):
                p = pos_ref[base + k * tc + r0 + u]
                pltpu.make_async_copy(_row_tile(x_ref, r0 + u), _row_tile(xs_hbm, p), sem).start(
                    priority=(u * TOP_K + k) % 2)
        return carry
    lax.fori_loop(0, tc // ISSUE_UNROLL, body, 0)
    for k in range(TOP_K):
        pltpu.make_async_copy(x_ref, xs_hbm.at[pl.ds(0, tc * ROW_SUB)], sem).wait()


def _dispatch(pos_tiles, zero_flags, xn_tiles, p_max):
    n = xn_tiles.shape[0] // ROW_SUB
    tc = TC_DISP
    return pl.pallas_call(
        _dispatch_kernel,
        grid_spec=pltpu.PrefetchScalarGridSpec(
            num_scalar_prefetch=2,
            grid=(n // tc,),
            in_specs=[pl.BlockSpec((tc * ROW_SUB, LANES), lambda i, p, z: (i, 0))],
            out_specs=pl.BlockSpec(memory_space=pl.ANY),
            scratch_shapes=[pltpu.VMEM((TM_MOE * ROW_SUB, LANES), jnp.uint32), pltpu.SemaphoreType.DMA(()),
                            pltpu.SemaphoreType.DMA(())],
        ),
        out_shape=jax.ShapeDtypeStruct((p_max * ROW_SUB, LANES), jnp.uint32),
        compiler_params=pltpu.CompilerParams(
            dimension_semantics=("arbitrary",), vmem_limit_bytes=VMEM_LIMIT),
        name="dispatch",
    )(pos_tiles, zero_flags, xn_tiles)


_FLAG_NEW_WEIGHTS = 1
WEIGHT_DMA_PRIORITY = 1


def _advance_weights(te_ref, ne_ref, fl_ref, slot_ref, copies):
    j = pl.program_id(0)
    t = pl.program_id(1)
    n_j = pl.num_programs(0)

    @pl.when((fl_ref[t] & _FLAG_NEW_WEIGHTS) != 0)
    def _():
        first = (j == 0) & (t == 0)

        @pl.when(first)
        def _():
            slot_ref[0] = 0
            for c in copies(te_ref[t], j, 0):
                c.start(priority=WEIGHT_DMA_PRIORITY)

        @pl.when(jnp.logical_not(first))
        def _():
            slot_ref[0] = 1 - slot_ref[0]

        s = slot_ref[0]
        for c in copies(te_ref[t], j, s):
            c.wait()
        ne = ne_ref[t]

        @pl.when(ne >= 0)
        def _():
            for c in copies(ne, j, 1 - s):
                c.start(priority=WEIGHT_DMA_PRIORITY)

        @pl.when((ne < 0) & (j + 1 < n_j))
        def _():
            for c in copies(te_ref[0], j + 1, 1 - s):
                c.start(priority=WEIGHT_DMA_PRIORITY)


def _by_valid_rows(n_valid, compute, zero_from):
    for rows in range(0, TM_MOE + 1, SUB_MOE):
        @pl.when((n_valid > rows - SUB_MOE) & (n_valid <= rows))
        def _():
            if rows > 0:
                compute(rows)
            if rows < TM_MOE:
                zero_from(rows)


def _bf16_dot(x_bf, w_f32):
    return lax.dot_general(x_bf, w_f32, (((1,), (0,)), ((), ())), preferred_element_type=F32)


def _moe_up_kernel(te_ref, tb_ref, fl_ref, ne_ref, nv_ref, x_ref, bg_ref, bu_ref, w_hbm, h_ref, wbuf, sem,
                   slot_ref):
    t = pl.program_id(1)
    n_valid = nv_ref[t]

    def copies(e, j, slot):
        cg = pl.multiple_of(j * BF_UP, BF_UP)
        cu = pl.multiple_of(D_FF + j * BF_UP, BF_UP)
        return (pltpu.make_async_copy(w_hbm.at[e, :, pl.ds(cg, BF_UP)], wbuf.at[slot, 0], sem.at[slot]),
                pltpu.make_async_copy(w_hbm.at[e, :, pl.ds(cu, BF_UP)], wbuf.at[slot, 1], sem.at[slot]))

    _advance_weights(te_ref, ne_ref, fl_ref, slot_ref, copies)

    def compute(n_rows):
        s = slot_ref[0]
        halves = [_unpack_hi_lo(c) for c in _load_row_tiles(x_ref, 0, n_rows)]
        x = jnp.concatenate([hl[0].astype(BF16) for hl in halves] + [hl[1].astype(BF16) for hl in halves], axis=1)
        gate = _bf16_dot(x, wbuf[s, 0]) + bg_ref[...]
        up = _bf16_dot(x, wbuf[s, 1]) + bu_ref[...]
        gate = jnp.minimum(gate, SWIGLU_LIMIT)
        up = jnp.clip(up, -SWIGLU_LIMIT, SWIGLU_LIMIT)
        act = gate * _sigmoid(SWIGLU_ALPHA * gate) * (up + 1.0)
        h_ref[0:n_rows, :] = act.astype(BF16)

    def zero_from(r0):
        h_ref[r0:TM_MOE, :] = jnp.zeros((TM_MOE - r0, h_ref.shape[1]), BF16)

    _by_valid_rows(n_valid, compute, zero_from)


def _moe_up(te, tb, fl, ne, nv, xs, w_gate_up, b_gate_up3):
    p_max = xs.shape[0] // ROW_SUB
    d = D_MODEL
    tm = TM_MOE
    n_tiles = p_max // tm
    n_j = D_FF // BF_UP
    return pl.pallas_call(
        _moe_up_kernel,
        grid_spec=pltpu.PrefetchScalarGridSpec(
            num_scalar_prefetch=5,
            grid=(n_j, n_tiles),
            in_specs=[
                pl.BlockSpec((tm * ROW_SUB, LANES), lambda j, t, te, tb, fl, ne, nv: (tb[t], 0)),
                pl.BlockSpec((None, 1, BF_UP), lambda j, t, te, tb, fl, ne, nv: (te[t], 0, j)),
                pl.BlockSpec((None, 1, BF_UP), lambda j, t, te, tb, fl, ne, nv: (te[t], 0, n_j + j)),
                pl.BlockSpec(memory_space=pl.ANY),
            ],
            out_specs=pl.BlockSpec((tm, BF_UP), lambda j, t, te, tb, fl, ne, nv: (t, j)),
            scratch_shapes=[pltpu.VMEM((2, 2, d, BF_UP), F32), pltpu.SemaphoreType.DMA((2,)),
                            pltpu.SMEM((1,), jnp.int32)],
        ),
        out_shape=jax.ShapeDtypeStruct((p_max, D_FF), BF16),
        compiler_params=pltpu.CompilerParams(
            dimension_semantics=("arbitrary", "arbitrary"), vmem_limit_bytes=VMEM_LIMIT),
        name="moe_up",
    )(te, tb, fl, ne, nv, xs, b_gate_up3, b_gate_up3, w_gate_up)


def _moe_down_kernel(te_ref, tb_ref, fl_ref, ne_ref, nv_ref, h_ref, bd_ref, w_hbm, y_ref, wbuf, sem, slot_ref):
    t = pl.program_id(1)
    n_valid = nv_ref[t]

    def copies(e, j, slot):
        c0 = pl.multiple_of(j * BN_DOWN, BN_DOWN)
        return (pltpu.make_async_copy(w_hbm.at[e, :, pl.ds(c0, BN_DOWN)], wbuf.at[slot], sem.at[slot]),)

    _advance_weights(te_ref, ne_ref, fl_ref, slot_ref, copies)

    def compute(n_rows):
        y = _bf16_dot(h_ref[0:n_rows, :], wbuf[slot_ref[0]]) + bd_ref[...]
        _store_row_tiles(y_ref, 0, _pack_bf16_pairs(y))

    def zero_from(r0):
        y_ref[r0 * ROW_SUB:TM_MOE * ROW_SUB, :] = jnp.zeros(((TM_MOE - r0) * ROW_SUB, LANES), y_ref.dtype)

    _by_valid_rows(n_valid, compute, zero_from)


def _moe_down(te, tb, fl, ne, nv, h, w_down, b_down3):
    p_max, f = h.shape
    tm = TM_MOE
    n_tiles = p_max // tm
    n_j = D_MODEL // BN_DOWN
    return pl.pallas_call(
        _moe_down_kernel,
        grid_spec=pltpu.PrefetchScalarGridSpec(
            num_scalar_prefetch=5,
            grid=(n_j, n_tiles),
            in_specs=[
                pl.BlockSpec((tm, f), lambda j, t, te, tb, fl, ne, nv: (tb[t], 0)),
                pl.BlockSpec((None, 1, BN_DOWN), lambda j, t, te, tb, fl, ne, nv: (te[t], 0, j)),
                pl.BlockSpec(memory_space=pl.ANY),
            ],
            out_specs=pl.BlockSpec((tm * ROW_SUB, LANES), lambda j, t, te, tb, fl, ne, nv: (t, 0)),
            scratch_shapes=[pltpu.VMEM((2, f, BN_DOWN), F32), pltpu.SemaphoreType.DMA((2,)),
                            pltpu.SMEM((1,), jnp.int32)],
        ),
        out_shape=jax.ShapeDtypeStruct((p_max * ROW_SUB, LANES), jnp.uint32),
        compiler_params=pltpu.CompilerParams(
            dimension_semantics=("arbitrary", "arbitrary"), vmem_limit_bytes=VMEM_LIMIT),
        name="moe_down",
    )(te, tb, fl, ne, nv, h, b_down3, w_down)


def _combine_kernel(pos_ref, x1_ref, gate_ref, gf_ref, y_hbm, op_ref, os_ref, buf, sem, *, n_prompt_tiles):
    i = pl.program_id(0)
    n_i = pl.num_programs(0)
    tc = x1_ref.shape[0]
    rows = TOP_K * tc

    def issue(tile, slot):
        base = tile * rows

        def body(g, carry):
            r0 = g * ISSUE_UNROLL
            for u in range(ISSUE_UNROLL):
                p = pos_ref[base + r0 + u]
                pltpu.make_async_copy(_row_tile(y_hbm, p), _row_tile(buf.at[slot], r0 + u),
                                      sem.at[slot]).start(priority=u % 2)
            return carry
        lax.fori_loop(0, rows // ISSUE_UNROLL, body, 0)

    @pl.when(i == 0)
    def _():
        issue(0, 0)

    @pl.when(i + 1 < n_i)
    def _():
        issue(i + 1, (i + 1) % 2)

    slot = i % 2
    pltpu.make_async_copy(y_hbm.at[pl.ds(0, rows * ROW_SUB)], buf.at[slot], sem.at[slot]).wait()
    his = [None] * ROW_SUB
    los = [None] * ROW_SUB
    for k in range(TOP_K):
        g = gate_ref[:, k:k + 1]
        for s, chunk in enumerate(_load_row_tiles(buf.at[slot], k * tc, tc)):
            hi, lo = _unpack_hi_lo(chunk)
            his[s] = g * hi if k == 0 else his[s] + g * hi
            los[s] = g * lo if k == 0 else los[s] + g * lo
    acc = x1_ref[...] + jnp.concatenate(his + los, axis=1)
    out = _rms_rows(acc, gf_ref[...])

    @pl.when(i < n_prompt_tiles)
    def _():
        op_ref[...] = out

    @pl.when(i >= n_prompt_tiles)
    def _():
        os_ref[...] = out


def _combine(pos_flat, x1, gates_nk, gf, y_sorted, n_prompt):
    n, d = x1.shape
    tc = TC_COMB
    n_tiles = n // tc
    n_pt = n_prompt // tc
    kern = functools.partial(_combine_kernel, n_prompt_tiles=n_pt)
    return pl.pallas_call(
        kern,
        grid_spec=pltpu.PrefetchScalarGridSpec(
            num_scalar_prefetch=1,
            grid=(n_tiles,),
            in_specs=[
                pl.BlockSpec((tc, d), lambda i, p: (i, 0)),
                pl.BlockSpec((tc, TOP_K), lambda i, p: (i, 0)),
                pl.BlockSpec((1, d), lambda i, p: (0, 0)),
                pl.BlockSpec(memory_space=pl.ANY),
            ],
            out_specs=[
                pl.BlockSpec((tc, d), lambda i, p: (jnp.minimum(i, n_pt - 1), 0)),
                pl.BlockSpec((tc, d), lambda i, p: (jnp.maximum(i - n_pt, 0), 0)),
            ],
            scratch_shapes=[pltpu.VMEM((2, TOP_K * tc * ROW_SUB, LANES), jnp.uint32),
                            pltpu.SemaphoreType.DMA((2,))],
        ),
        out_shape=[
            jax.ShapeDtypeStruct((n_prompt, d), F32),
            jax.ShapeDtypeStruct((n - n_prompt, d), F32),
        ],
        compiler_params=pltpu.CompilerParams(
            dimension_semantics=("arbitrary",), vmem_limit_bytes=VMEM_LIMIT),
        name="combine",
    )(pos_flat, x1, gates_nk, gf, y_sorted)


def kernel(x_prompt, x_sample, state_conv_a, state_conv_b, meta_tokens, norm1_g, w_in, conv_a_w, conv_b_w,
           conv_b_b, ln_b_g, ln_b_b, w_out, norm2_g, w_router, b_router, w_gate_up, b_gate_up, w_down,
           b_down, final_norm_g):
    bp, seq, d = x_prompt.shape
    n_seq, n_t, _ = x_sample.shape
    n_prompt = bp * seq
    n_sample = n_seq * n_t
    n = n_prompt + n_sample
    assert norm1_g.shape[0] == 1, "single layer"
    assert seq % T_MIX == 0 and n_prompt % TM_POST == 0 and n_sample == TM_POST
    assert n_prompt % n_sample == 0 and n % TC_COMB == 0 and n_prompt % TC_COMB == 0

    g1 = norm1_g[0][None]
    win_bf = w_in[0].astype(BF16)
    wout_bf = w_out[0]
    caw, cbw = conv_a_w[0], conv_b_w[0]
    cbb, lng, lnb = conv_b_b[0][None], ln_b_g[0][None], ln_b_b[0][None]

    xs_t = jnp.transpose(x_sample, (1, 0, 2)).reshape(n_sample, d)
    sa_t = jnp.transpose(state_conv_a[0], (1, 0, 2))
    sb_t = jnp.transpose(state_conv_b[0], (1, 0, 2))

    ymix_p, pa, pb = _mixer_prompt(x_prompt, meta_tokens, g1, win_bf, caw, cbw, cbb, lng, lnb)
    ymix_s, na_t, nb_t = _mixer_sample(xs_t, g1, win_bf, sa_t, sb_t, caw, cbw, cbb, lng, lnb, n_seq, n_t)

    x1, xn, idx, gates, rank, cnt = _post_mixer(
        ymix_p, ymix_s, x_prompt.reshape(n_prompt, d), xs_t, wout_bf, norm2_g[0][None],
        jnp.transpose(w_router[0]), b_router[0][:, None])

    tm = TM_MOE
    n_assign = n * TOP_K
    n_tiles = (n_assign + N_EXPERTS * (tm - 1)) // tm
    p_max = n_tiles * tm
    counts = cnt[:, 0]
    tiles_per_e = (counts + tm - 1) // tm
    tile_end = jnp.cumsum(tiles_per_e)
    tile_start = tile_end - tiles_per_e
    n_used = tile_end[-1]
    e_ar = jnp.arange(N_EXPERTS, dtype=jnp.int32)
    start_of = jnp.sum(jnp.where(idx[None] == e_ar[:, None, None], tile_start[:, None, None], 0), axis=0)
    pos = (start_of * tm + rank).astype(jnp.int32)
    tid = jnp.arange(n_tiles, dtype=jnp.int32)
    tb = jnp.maximum(jnp.minimum(tid, n_used - 1), 0).astype(jnp.int32)
    expert_of = lambda tile: jnp.minimum(
        jnp.sum((tile_end[None, :] <= tile[:, None]).astype(jnp.int32), axis=1), N_EXPERTS - 1).astype(jnp.int32)
    te = expert_of(tb)
    valid = tid < n_used
    new_w = valid & ((tid == 0) | (te != jnp.roll(te, 1)))
    fl = new_w.astype(jnp.int32) * _FLAG_NEW_WEIGHTS
    mine = te[:, None] == e_ar[None, :]
    rows_left = (jnp.sum(jnp.where(mine, counts[None, :], 0), axis=1)
                 - (tid - jnp.sum(jnp.where(mine, tile_start[None, :], 0), axis=1)) * tm)
    nv = jnp.where(valid, jnp.clip(rows_left, 0, tm), 0).astype(jnp.int32)
    next_tile = jnp.sum(jnp.where(te[:, None] == e_ar[None, :], tile_end[None, :], 0), axis=1)
    ne = jnp.where(next_tile < n_used, expert_of(next_tile), -1).astype(jnp.int32)
    partial = (counts % tm) != 0
    zero_flags = ((tid >= n_used) | jnp.any((tid[:, None] == (tile_end - 1)[None, :]) & partial[None, :],
                                            axis=1)).astype(jnp.int32)

    by_tile = lambda tc: pos.reshape(TOP_K, n // tc, tc).transpose(1, 0, 2).reshape(-1)
    xs_sorted = _dispatch(by_tile(TC_DISP), zero_flags, xn, p_max)
    h = _moe_up(te, tb, fl, ne, nv, xs_sorted, w_gate_up[0], b_gate_up[0][:, None, :])
    y_sorted = _moe_down(te, tb, fl, ne, nv, h, w_down[0], b_down[0][:, None, :])

    yp, ys_t = _combine(by_tile(TC_COMB), x1, jnp.transpose(gates), final_norm_g[None], y_sorted, n_prompt)

    y_prompt = yp.reshape(bp, seq, d)
    y_sample = jnp.transpose(ys_t.reshape(n_t, n_seq, d), (1, 0, 2))
    new_a_prompt = pa[None]
    new_b_prompt = pb[None]
    new_a_sample = jnp.transpose(na_t, (1, 0, 2))[None]
    new_b_sample = jnp.transpose(nb_t, (1, 0, 2))[None]
    return (y_prompt, y_sample, new_a_prompt, new_b_prompt, new_a_sample, new_b_sample)
```
